```python
import jax, jax.numpy as jnp
from jax import lax
import numpy as np

D_MODEL = 2048
BATCH = 4
SEQ = 4096
DEPTH = 1

HEAD_DIM = 64
N_HEADS_SWA = 16
N_KV_SWA = 2
N_HEADS_FOX = 16
WINDOW = 128
BLOCK = 128
D_FF = 5632
CONV_WIDTH = 3
EPS = 1e-6
D_SWA = N_HEADS_SWA * HEAD_DIM
D_KV_SWA = N_KV_SWA * HEAD_DIM
D_FOX = N_HEADS_FOX * HEAD_DIM
D_MIX = D_SWA + D_FOX
D_IN = D_SWA + 2 * D_KV_SWA + 3 * D_FOX + N_HEADS_FOX

kernel_name = "hymba_swa_sink_fox_alibi_convffn"


def rmsnorm(x, g):
    xf = x.astype(jnp.float32)
    y = xf * lax.rsqrt(jnp.mean(xf * xf, axis=-1, keepdims=True) + EPS)
    return (y * g.astype(jnp.float32)).astype(x.dtype)


def alibi_slopes(n):
    return jnp.asarray(2.0 ** (-8.0 * np.arange(1, n + 1) / n), dtype=jnp.float32)


def swa_sink_attention(q, k, v, sinks):
    B, S, H, D = q.shape
    KV = k.shape[2]
    G = H // KV
    nB = S // BLOCK
    scale = 1.0 / np.sqrt(D)
    qb = q.reshape(B, nB, BLOCK, KV, G, D)

    def band(t):
        tp = jnp.pad(t, ((0, 0), (BLOCK, 0), (0, 0), (0, 0)))[:, :S]
        return jnp.concatenate([tp.reshape(B, nB, BLOCK, KV, D), t.reshape(B, nB, BLOCK, KV, D)], axis=2)

    kb, vb = band(k), band(v)
    s = jnp.einsum('bnqkgd,bnskd->bnkgqs', qb, kb).astype(jnp.float32) * scale
    q_loc = BLOCK + jnp.arange(BLOCK)
    k_loc = jnp.arange(2 * BLOCK)
    dist = (q_loc[:, None] - k_loc[None, :]).astype(jnp.float32)
    blk_start = jnp.arange(nB) * BLOCK - BLOCK
    k_abs = blk_start[:, None] + k_loc[None, :]
    mask = (dist >= 0)[None] & (dist < WINDOW)[None] & (k_abs >= 0)[:, None, :]
    slopes = alibi_slopes(H).reshape(KV, G)
    s = s - slopes[:, :, None, None] * dist
    s = jnp.where(mask[None, :, None, None], s, -jnp.inf)
    sink = sinks.astype(jnp.float32).reshape(KV, G)[None, None, :, :, None, None]
    m = jnp.maximum(jnp.max(s, axis=-1, keepdims=True), sink)
    p = jnp.exp(s - m)
    p = p / (jnp.sum(p, axis=-1, keepdims=True) + jnp.exp(sink - m))
    o = jnp.einsum('bnkgqs,bnskd->bnqkgd', p.astype(v.dtype), vb)
    return o.reshape(B, S, H, D)


def forgetting_attention(q, k, v, log_f):
    B, S, H, D = q.shape
    nB = S // BLOCK
    scale = 1.0 / np.sqrt(D)
    c = lax.cumsum(log_f, axis=1)
    cT = jnp.transpose(c, (0, 2, 1))
    qb = jnp.moveaxis(q.reshape(B, nB, BLOCK, H, D), 1, 0)
    cb = jnp.moveaxis(cT.reshape(B, H, nB, BLOCK), 2, 0)
    k_pos = jnp.arange(S)

    def one_block(args):
        q_blk, c_blk, n = args
        s = jnp.einsum('bqhd,bshd->bhqs', q_blk, k).astype(jnp.float32) * scale
        s = s + c_blk[..., None] - cT[:, :, None, :]
        q_pos = n * BLOCK + jnp.arange(BLOCK)
        s = jnp.where((q_pos[:, None] >= k_pos[None, :])[None, None], s, -jnp.inf)
        p = jax.nn.softmax(s, axis=-1)
        return jnp.einsum('bhqs,bshd->bqhd', p.astype(v.dtype), v)

    o = lax.map(one_block, (qb, cb, jnp.arange(nB)))
    return jnp.moveaxis(o, 0, 1).reshape(B, S, H, D)


def causal_dwconv(u, w, b):
    S = u.shape[1]
    up = jnp.pad(u, ((0, 0), (CONV_WIDTH - 1, 0), (0, 0)))
    y = b
    for kk in range(CONV_WIDTH):
        y = y + up[:, kk:kk + S] * w[kk]
    return y


def setup_inputs(seed: int = 0) -> dict:
    key = jax.random.key(seed)
    ks = jax.random.split(key, 17)
    f32 = jnp.float32
    nrm = lambda k, shape, s: jax.random.normal(k, shape, f32) * s
    gain = lambda k, n: 1.0 + 0.05 * jax.random.normal(k, (DEPTH, n), f32)
    return {
        "x": jax.random.normal(ks[0], (BATCH, SEQ, D_MODEL), f32),
        "pre_mix_g": gain(ks[1], D_MODEL),
        "w_in": nrm(ks[2], (DEPTH, D_MODEL, D_IN), D_MODEL ** -0.5),
        "b_forget": 2.0 + 0.5 * jax.random.normal(ks[3], (DEPTH, N_HEADS_FOX), f32),
        "sinks": nrm(ks[4], (DEPTH, N_HEADS_SWA), 1.0),
        "grp_swa_g": gain(ks[5], D_SWA),
        "grp_fox_g": gain(ks[6], D_FOX),
        "w_out": nrm(ks[7], (DEPTH, D_MIX, D_MODEL), D_MIX ** -0.5),
        "post_mix_g": gain(ks[8], D_MODEL),
        "pre_ffn_g": gain(ks[9], D_MODEL),
        "w_up": nrm(ks[10], (DEPTH, D_MODEL, 2 * D_FF), D_MODEL ** -0.5),
        "conv_w": nrm(ks[11], (DEPTH, CONV_WIDTH, 2 * D_FF), CONV_WIDTH ** -0.5),
        "conv_b": nrm(ks[12], (DEPTH, 2 * D_FF), 0.02),
        "w_down": nrm(ks[13], (DEPTH, D_FF, D_MODEL), D_FF ** -0.5),
        "post_ffn_g": gain(ks[14], D_MODEL),
    }


def reference(x, pre_mix_g, w_in, b_forget, sinks, grp_swa_g, grp_fox_g, w_out, post_mix_g,
              pre_ffn_g, w_up, conv_w, conv_b, w_down, post_ffn_g):
    B, S, _ = x.shape
    cuts = np.cumsum([D_SWA, D_KV_SWA, D_KV_SWA, D_FOX, D_FOX, D_FOX])
    for l in range(DEPTH):
        h = rmsnorm(x, pre_mix_g[l])
        proj = h @ w_in[l]
        q_a, k_a, v_a, q_b, k_b, v_b, f_b = jnp.split(proj, cuts, axis=-1)
        o_a = swa_sink_attention(q_a.reshape(B, S, N_HEADS_SWA, HEAD_DIM),
                                 k_a.reshape(B, S, N_KV_SWA, HEAD_DIM),
                                 v_a.reshape(B, S, N_KV_SWA, HEAD_DIM), sinks[l])
        log_f = jax.nn.log_sigmoid(f_b.astype(jnp.float32) + b_forget[l].astype(jnp.float32))
        o_b = forgetting_attention(q_b.reshape(B, S, N_HEADS_FOX, HEAD_DIM),
                                   k_b.reshape(B, S, N_HEADS_FOX, HEAD_DIM),
                                   v_b.reshape(B, S, N_HEADS_FOX, HEAD_DIM), log_f)
        o_a = rmsnorm(o_a.reshape(B, S, D_SWA), grp_swa_g[l])
        o_b = rmsnorm(o_b.reshape(B, S, D_FOX), grp_fox_g[l])
        mix = jnp.concatenate([o_a, o_b], axis=-1) @ w_out[l]
        x = x + rmsnorm(mix, post_mix_g[l])
        h = rmsnorm(x, pre_ffn_g[l])
        u = causal_dwconv(h @ w_up[l], conv_w[l], conv_b[l])
        gate, val = jnp.split(u, 2, axis=-1)
        y = (jax.nn.gelu(gate, approximate=True) * val) @ w_down[l]
        x = x + rmsnorm(y, post_ffn_g[l])
    return x
```

```python
import functools

import numpy as np
import jax
import jax.numpy as jnp
from jax import lax
from jax.experimental import pallas as pl
from jax.experimental.pallas import tpu as pltpu

D_MODEL = 2048
HEAD_DIM = 64
N_HEADS_SWA = 16
N_KV_SWA = 2
N_HEADS_FOX = 16
WINDOW = 128
BLOCK = 128
D_FF = 5632
CONV_WIDTH = 3
EPS = 1e-6
D_SWA = N_HEADS_SWA * HEAD_DIM
D_KV_SWA = N_KV_SWA * HEAD_DIM
D_FOX = N_HEADS_FOX * HEAD_DIM
D_MIX = D_SWA + D_FOX
D_QKV = D_SWA + 2 * D_KV_SWA + 3 * D_FOX

LANES = 128
COL_QA = 0
COL_KA = D_SWA // LANES
COL_VA = COL_KA + 1
COL_QB = COL_VA + 1
COL_KB = COL_QB + D_FOX // LANES
COL_VB = COL_KB + D_FOX // LANES

NEG = -1e30
VMEM_LIMIT = 56 * 1024 * 1024

ALIBI_SLOPES = [float(v) for v in np.asarray(2.0 ** (-8.0 * np.arange(1, N_HEADS_SWA + 1) / N_HEADS_SWA),
                                            dtype=np.float32)]

_NT = (((1,), (1,)), ((), ()))


def _rms(xf, g):
    return xf * lax.rsqrt(jnp.mean(xf * xf, axis=-1, keepdims=True) + EPS) * g


def _params(*sem):
    return pltpu.CompilerParams(dimension_semantics=sem, vmem_limit_bytes=VMEM_LIMIT)


IN_TM = 512
IN_CHUNK = 1024


def _in_proj_kernel(x_ref, g_ref, w_ref, wf_ref, bf_ref, proj_ref, logf_ref):
    h = _rms(x_ref[...], g_ref[...]).astype(jnp.bfloat16)
    for n0 in range(0, D_QKV, IN_CHUNK):
        n1 = min(n0 + IN_CHUNK, D_QKV)
        proj_ref[:, n0:n1] = jnp.dot(h, w_ref[:, n0:n1],
                                     preferred_element_type=jnp.float32).astype(jnp.bfloat16)
    f = jnp.dot(h, wf_ref[...], preferred_element_type=jnp.float32) + bf_ref[...]
    logf_ref[...] = jnp.minimum(f, 0.0) - jnp.log1p(jnp.exp(-jnp.abs(f)))


def _in_proj(x2d, g, w_qkv, w_f, b_f):
    m = x2d.shape[0]
    return pl.pallas_call(
        _in_proj_kernel,
        grid=(m // IN_TM,),
        in_specs=[
            pl.BlockSpec((IN_TM, D_MODEL), lambda i: (i, 0)),
            pl.BlockSpec((1, D_MODEL), lambda i: (0, 0)),
            pl.BlockSpec((D_MODEL, D_QKV), lambda i: (0, 0), pipeline_mode=pl.Buffered(1)),
            pl.BlockSpec((D_MODEL, LANES), lambda i: (0, 0), pipeline_mode=pl.Buffered(1)),
            pl.BlockSpec((1, LANES), lambda i: (0, 0)),
        ],
        out_specs=[
            pl.BlockSpec((IN_TM, D_QKV), lambda i: (i, 0)),
            pl.BlockSpec((IN_TM, LANES), lambda i: (i, 0)),
        ],
        out_shape=[
            jax.ShapeDtypeStruct((m, D_QKV), jnp.bfloat16),
            jax.ShapeDtypeStruct((m, LANES), jnp.float32),
        ],
        compiler_params=_params("arbitrary"),
        name="in_proj",
    )(x2d, g, w_qkv, w_f, b_f)


CS_BLK = 128


def _cumsum_kernel(logf_ref, c_ref, carry_ref):
    s = logf_ref.shape[1]
    r = lax.broadcasted_iota(jnp.int32, (CS_BLK, CS_BLK), 0)
    c = lax.broadcasted_iota(jnp.int32, (CS_BLK, CS_BLK), 1)
    upper = (r <= c).astype(jnp.float32)
    carry_ref[...] = jnp.zeros_like(carry_ref)

    def body(i, _):
        t0 = pl.multiple_of(i * CS_BLK, CS_BLK)
        blk = logf_ref[0, pl.ds(t0, CS_BLK), :]
        pre = lax.dot_general(blk, upper, (((0,), (0,)), ((), ())),
                              precision=lax.Precision.HIGHEST,
                              preferred_element_type=jnp.float32)
        cur = pre + carry_ref[...]
        c_ref[0, :, pl.ds(t0, CS_BLK)] = cur
        carry_ref[...] = jnp.broadcast_to(cur[:, CS_BLK - 1:CS_BLK], carry_ref.shape)
        return 0

    lax.fori_loop(0, s // CS_BLK, body, 0)


def _cumsum(logf3):
    b, s, _ = logf3.shape
    return pl.pallas_call(
        _cumsum_kernel,
        grid=(b,),
        in_specs=[pl.BlockSpec((1, s, LANES), lambda i: (i, 0, 0))],
        out_specs=pl.BlockSpec((1, LANES, s), lambda i: (i, 0, 0)),
        out_shape=jax.ShapeDtypeStruct((b, LANES, s), jnp.float32),
        scratch_shapes=[pltpu.VMEM((LANES, CS_BLK), jnp.float32)],
        compiler_params=_params("arbitrary"),
        name="cumsum_logf",
    )(logf3)


SWA_TQ = 512
PAIRS_PER_KV = (N_HEADS_SWA // N_KV_SWA) // 2


def _swa_kernel(sink_ref, q_ref, k_ref, v_ref, o_ref, kpad, kswap, vpad, vswap):
    qi = pl.program_id(1)
    s_len = k_ref.shape[1]

    @pl.when(qi == 0)
    def _():
        zeros = jnp.zeros((BLOCK, LANES), jnp.bfloat16)
        for src, pad, swp in ((k_ref, kpad, kswap), (v_ref, vpad, vswap)):
            pad[0:BLOCK, :] = zeros
            swp[0:BLOCK, :] = zeros
            val = src[0]
            pad[BLOCK:, :] = val
            swp[BLOCK:, :] = pltpu.roll(val.astype(jnp.float32), HEAD_DIM, 1).astype(jnp.bfloat16)

    lane = lax.broadcasted_iota(jnp.int32, (2 * BLOCK, LANES), 1)
    low = lane < HEAD_DIM
    out_low = lax.broadcasted_iota(jnp.int32, (BLOCK, LANES), 1) < HEAD_DIM
    q_loc = BLOCK + lax.broadcasted_iota(jnp.int32, (BLOCK, 2 * BLOCK), 0)
    k_loc = lax.broadcasted_iota(jnp.int32, (BLOCK, 2 * BLOCK), 1)
    dist_i = q_loc - k_loc
    dist = dist_i.astype(jnp.float32)
    in_win = (dist_i >= 0) & (dist_i < WINDOW)
    scale = 1.0 / np.sqrt(HEAD_DIM)

    def block_body(r, _):
        n = qi * (SWA_TQ // BLOCK) + r
        row0 = pl.multiple_of(r * BLOCK, BLOCK)
        band0 = pl.multiple_of(n * BLOCK, BLOCK)
        k_abs = (n - 1) * BLOCK + k_loc
        mask = in_win & (k_abs >= 0)
        kp = kpad[pl.ds(band0, 2 * BLOCK), :]
        ks = kswap[pl.ds(band0, 2 * BLOCK), :]
        vp = vpad[pl.ds(band0, 2 * BLOCK), :]
        vs = vswap[pl.ds(band0, 2 * BLOCK), :]
        zero = jnp.zeros_like(kp)
        for kk in range(N_KV_SWA):
            k_lo = jnp.where(low, kp if kk == 0 else ks, zero)
            k_hi = jnp.where(low, zero, ks if kk == 0 else kp)
            v_lo = jnp.where(low, vp if kk == 0 else vs, zero)
            v_hi = jnp.where(low, zero, vs if kk == 0 else vp)
            qs = jnp.concatenate(
                [q_ref[0, pl.ds(row0, BLOCK), (kk * PAIRS_PER_KV + pi) * LANES:(kk * PAIRS_PER_KV + pi + 1) * LANES]
                 for pi in range(PAIRS_PER_KV)], axis=0)
            s_even = lax.dot_general(qs, k_lo, _NT, preferred_element_type=jnp.float32)
            s_odd = lax.dot_general(qs, k_hi, _NT, preferred_element_type=jnp.float32)
            p_parts = ([], [])
            inv_parts = []
            for pi in range(PAIRS_PER_KV):
                invs = []
                for par, s_all in enumerate((s_even, s_odd)):
                    h = kk * 2 * PAIRS_PER_KV + 2 * pi + par
                    sc = s_all[pi * BLOCK:(pi + 1) * BLOCK, :] * scale - ALIBI_SLOPES[h] * dist
                    sc = jnp.where(mask, sc, NEG)
                    sink = sink_ref[h]
                    m = jnp.maximum(jnp.max(sc, axis=-1, keepdims=True), sink)
                    p = jnp.exp(sc - m)
                    den = jnp.sum(p, axis=-1, keepdims=True) + jnp.exp(sink - m)
                    p_parts[par].append(p.astype(jnp.bfloat16))
                    invs.append(1.0 / den)
                inv_parts.append(invs)
            p_even = jnp.concatenate(p_parts[0], axis=0)
            p_odd = jnp.concatenate(p_parts[1], axis=0)
            o = (jnp.dot(p_even, v_lo, preferred_element_type=jnp.float32)
                 + jnp.dot(p_odd, v_hi, preferred_element_type=jnp.float32))
            for pi in range(PAIRS_PER_KV):
                inv = jnp.where(out_low, inv_parts[pi][0], inv_parts[pi][1])
                col = (kk * PAIRS_PER_KV + pi) * LANES
                o_ref[0, pl.ds(row0, BLOCK), col:col + LANES] = o[pi * BLOCK:(pi + 1) * BLOCK, :] * inv
        return 0

    lax.fori_loop(0, SWA_TQ // BLOCK, block_body, 0)
    del s_len


def _swa(proj3, sinks):
    b, s, _ = proj3.shape
    pad_shape = pltpu.VMEM((s + BLOCK, LANES), jnp.bfloat16)
    return pl.pallas_call(
        _swa_kernel,
        grid_spec=pltpu.PrefetchScalarGridSpec(
            num_scalar_prefetch=0,
            grid=(b, s // SWA_TQ),
            in_specs=[
                pl.BlockSpec(memory_space=pltpu.SMEM),
                pl.BlockSpec((1, SWA_TQ, D_SWA), lambda bi, qi: (bi, qi, 0)),
                pl.BlockSpec((1, s, LANES), lambda bi, qi: (bi, 0, COL_KA)),
                pl.BlockSpec((1, s, LANES), lambda bi, qi: (bi, 0, COL_VA)),
            ],
            out_specs=pl.BlockSpec((1, SWA_TQ, D_SWA), lambda bi, qi: (bi, qi, 0)),
            scratch_shapes=[pad_shape, pad_shape, pad_shape, pad_shape],
        ),
        out_shape=jax.ShapeDtypeStruct((b, s, D_SWA), jnp.float32),
        compiler_params=_params("arbitrary", "arbitrary"),
        name="swa_attention",
    )(sinks, proj3, proj3, proj3)


FOX_T = 512


def _fox_kernel(q_ref, k_ref, v_ref, c_ref, o_ref, m_ref, l_ref, acc_ref):
    qi = pl.program_id(2)
    scale = 1.0 / np.sqrt(HEAD_DIM)
    q = q_ref[0] * jnp.asarray(scale, jnp.bfloat16)
    lane = lax.broadcasted_iota(jnp.int32, (FOX_T, LANES), 1)
    low = lane < HEAD_DIM
    row = lax.broadcasted_iota(jnp.int32, (FOX_T, FOX_T), 0)
    col = lax.broadcasted_iota(jnp.int32, (FOX_T, FOX_T), 1)
    causal = row >= col

    m_ref[...] = jnp.full(m_ref.shape, NEG, jnp.float32)
    l_ref[...] = jnp.zeros(l_ref.shape, jnp.float32)
    acc_ref[...] = jnp.zeros(acc_ref.shape, jnp.float32)

    c_end = [jnp.min(c_ref[0, par, pl.ds(qi, 1), :], axis=-1, keepdims=True) for par in range(2)]

    def chunk(j, diagonal):
        k0 = pl.multiple_of(j * FOX_T, FOX_T)
        kc = k_ref[0, pl.ds(k0, FOX_T), :]
        vc = v_ref[0, pl.ds(k0, FOX_T), :]
        zero = jnp.zeros_like(kc)
        for par in range(2):
            keep = low if par == 0 else jnp.logical_not(low)
            kh = jnp.where(keep, kc, zero)
            vh = jnp.where(keep, vc, zero)
            s = lax.dot_general(q, kh, _NT, preferred_element_type=jnp.float32)
            s = s + (c_end[par] - c_ref[0, par, pl.ds(j, 1), :])
            if diagonal:
                s = jnp.where(causal, s, NEG)
            m_old = m_ref[par]
            m_new = jnp.maximum(m_old, jnp.max(s, axis=-1, keepdims=True))
            alpha = jnp.exp(m_old - m_new)
            p = jnp.exp(s - m_new)
            l_ref[par] = alpha * l_ref[par] + jnp.sum(p, axis=-1, keepdims=True)
            acc_ref[par] = alpha * acc_ref[par] + jnp.dot(p.astype(jnp.bfloat16), vh,
                                                          preferred_element_type=jnp.float32)
            m_ref[par] = m_new

    def body(j, _):
        chunk(j, False)
        return 0

    lax.fori_loop(0, qi, body, 0)
    chunk(qi, True)

    o_ref[0] = jnp.where(low, acc_ref[0] / l_ref[0], acc_ref[1] / l_ref[1])


def _fox(proj3, c4):
    b, s, _ = proj3.shape
    npairs = N_HEADS_FOX // 2
    return pl.pallas_call(
        _fox_kernel,
        grid=(b, npairs, s // FOX_T),
        in_specs=[
            pl.BlockSpec((1, FOX_T, LANES), lambda bi, p, qi: (bi, qi, COL_QB + p)),
            pl.BlockSpec((1, s, LANES), lambda bi, p, qi: (bi, 0, COL_KB + p)),
            pl.BlockSpec((1, s, LANES), lambda bi, p, qi: (bi, 0, COL_VB + p)),
            pl.BlockSpec((1, 2, s // FOX_T, FOX_T), lambda bi, p, qi: (bi, p, 0, 0)),
        ],
        out_specs=pl.BlockSpec((1, FOX_T, LANES), lambda bi, p, qi: (bi, qi, p)),
        out_shape=jax.ShapeDtypeStruct((b, s, D_FOX), jnp.float32),
        scratch_shapes=[
            pltpu.VMEM((2, FOX_T, 1), jnp.float32),
            pltpu.VMEM((2, FOX_T, 1), jnp.float32),
            pltpu.VMEM((2, FOX_T, LANES), jnp.float32),
        ],
        compiler_params=_params("arbitrary", "arbitrary", "arbitrary"),
        name="fox_attention",
    )(proj3, proj3, proj3, c4)


OUT_TM = 512


def _out_proj_kernel(oa_ref, ob_ref, x_ref, ga_ref, gb_ref, w_ref, gp_ref, x1_ref):
    na = _rms(oa_ref[...], ga_ref[...]).astype(jnp.bfloat16)
    nb = _rms(ob_ref[...], gb_ref[...]).astype(jnp.bfloat16)
    mix = (jnp.dot(na, w_ref[0:D_SWA, :], preferred_element_type=jnp.float32)
           + jnp.dot(nb, w_ref[D_SWA:D_MIX, :], preferred_element_type=jnp.float32))
    x1_ref[...] = x_ref[...] + _rms(mix, gp_ref[...])


def _out_proj(oa, ob, x2d, ga, gb, w_out, gp):
    m = x2d.shape[0]
    row = lambda i: (i, 0)
    fixed = lambda i: (0, 0)
    return pl.pallas_call(
        _out_proj_kernel,
        grid=(m // OUT_TM,),
        in_specs=[
            pl.BlockSpec((OUT_TM, D_SWA), row),
            pl.BlockSpec((OUT_TM, D_FOX), row),
            pl.BlockSpec((OUT_TM, D_MODEL), row),
            pl.BlockSpec((1, D_SWA), fixed),
            pl.BlockSpec((1, D_FOX), fixed),
            pl.BlockSpec((D_MIX, D_MODEL), fixed, pipeline_mode=pl.Buffered(1)),
            pl.BlockSpec((1, D_MODEL), fixed),
        ],
        out_specs=pl.BlockSpec((OUT_TM, D_MODEL), row),
        out_shape=jax.ShapeDtypeStruct((m, D_MODEL), jnp.float32),
        compiler_params=_params("arbitrary"),
        name="out_proj",
    )(oa, ob, x2d, ga, gb, w_out, gp)


FFN_TM = 512
FFN_TF = 512
HALO = 16
N_FT = D_FF // FFN_TF


def _gelu_tanh(x):
    return 0.5 * x * (1.0 + jnp.tanh(np.sqrt(2.0 / np.pi) * (x + 0.044715 * (x * x * x))))


def _ffn_kernel(x_ref, halo_ref, g_ref, wg_ref, wv_ref, cwg_ref, cwv_ref, cbg_ref, cbv_ref, wd_ref, gp_ref,
                o_ref, h_scr, ug_scr, uv_scr, *, tiles_per_seq):
    i = pl.program_id(0)
    j = pl.program_id(1)

    @pl.when(j == 0)
    def _():
        g = g_ref[...]
        hh = _rms(halo_ref[...], g)
        hh = jnp.where(i % tiles_per_seq == 0, 0.0, hh)
        top = jnp.concatenate([jnp.zeros_like(hh), hh], axis=0)
        h_scr[0:HALO, :] = top.astype(jnp.bfloat16)
        h_scr[HALO:, :] = _rms(x_ref[...], g).astype(jnp.bfloat16)

    h = h_scr[...]
    ug_scr[...] = jnp.dot(h, wg_ref[...], preferred_element_type=jnp.float32)
    uv_scr[...] = jnp.dot(h, wv_ref[...], preferred_element_type=jnp.float32)

    def conv(u_scr, cw_ref, cb_ref):
        y = cb_ref[...]
        for kk in range(CONV_WIDTH):
            y = y + u_scr[pl.ds(HALO - (CONV_WIDTH - 1) + kk, FFN_TM), :] * cw_ref[kk:kk + 1, :]
        return y

    gate = conv(ug_scr, cwg_ref, cbg_ref)
    val = conv(uv_scr, cwv_ref, cbv_ref)
    a = (_gelu_tanh(gate) * val).astype(jnp.bfloat16)
    part = jnp.dot(a, wd_ref[...], preferred_element_type=jnp.float32)

    @pl.when(j == 0)
    def _():
        o_ref[...] = part

    @pl.when(j > 0)
    def _():
        o_ref[...] += part

    @pl.when(j == N_FT - 1)
    def _():
        o_ref[...] = x_ref[...] + _rms(o_ref[...], gp_ref[...])


def _ffn(x1, g, w_up, conv_w, conv_b, w_down, gp, seq_len):
    m = x1.shape[0]
    halo_blocks = FFN_TM // 8
    kern = functools.partial(_ffn_kernel, tiles_per_seq=seq_len // FFN_TM)
    return pl.pallas_call(
        kern,
        grid=(m // FFN_TM, N_FT),
        in_specs=[
            pl.BlockSpec((FFN_TM, D_MODEL), lambda i, j: (i, 0)),
            pl.BlockSpec((8, D_MODEL), lambda i, j: (jnp.maximum(i * halo_blocks - 1, 0), 0)),
            pl.BlockSpec((1, D_MODEL), lambda i, j: (0, 0)),
            pl.BlockSpec((D_MODEL, FFN_TF), lambda i, j: (0, j)),
            pl.BlockSpec((D_MODEL, FFN_TF), lambda i, j: (0, j + N_FT)),
            pl.BlockSpec((CONV_WIDTH, FFN_TF), lambda i, j: (0, j)),
            pl.BlockSpec((CONV_WIDTH, FFN_TF), lambda i, j: (0, j + N_FT)),
            pl.BlockSpec((1, FFN_TF), lambda i, j: (0, j)),
            pl.BlockSpec((1, FFN_TF), lambda i, j: (0, j + N_FT)),
            pl.BlockSpec((FFN_TF, D_MODEL), lambda i, j: (j, 0)),
            pl.BlockSpec((1, D_MODEL), lambda i, j: (0, 0)),
        ],
        out_specs=pl.BlockSpec((FFN_TM, D_MODEL), lambda i, j: (i, 0)),
        out_shape=jax.ShapeDtypeStruct((m, D_MODEL), jnp.float32),
        scratch_shapes=[
            pltpu.VMEM((FFN_TM + HALO, D_MODEL), jnp.bfloat16),
            pltpu.VMEM((FFN_TM + HALO, FFN_TF), jnp.float32),
            pltpu.VMEM((FFN_TM + HALO, FFN_TF), jnp.float32),
        ],
        compiler_params=_params("arbitrary", "arbitrary"),
        name="conv_geglu_ffn",
    )(x1, x1, g, w_up, w_up, conv_w, conv_w, conv_b, conv_b, w_down, gp)


def kernel(x, pre_mix_g, w_in, b_forget, sinks, grp_swa_g, grp_fox_g, w_out, post_mix_g,
           pre_ffn_g, w_up, conv_w, conv_b, w_down, post_ffn_g):
    b, s, d = x.shape
    depth = w_in.shape[0]
    bf16 = jnp.bfloat16
    xf = x.reshape(b * s, d)
    for l in range(depth):
        w_qkv = w_in[l, :, :D_QKV].astype(bf16)
        w_f = jnp.pad(w_in[l, :, D_QKV:], ((0, 0), (0, LANES - N_HEADS_FOX))).astype(bf16)
        b_f = jnp.pad(b_forget[l], (0, LANES - N_HEADS_FOX)).reshape(1, LANES)

        proj, logf = _in_proj(xf, pre_mix_g[l].reshape(1, d), w_qkv, w_f, b_f)
        proj3 = proj.reshape(b, s, D_QKV)
        c = _cumsum(logf.reshape(b, s, LANES))
        c4 = c[:, :N_HEADS_FOX, :].reshape(b, N_HEADS_FOX, s // FOX_T, FOX_T)

        o_a = _swa(proj3, sinks[l])
        o_b = _fox(proj3, c4)

        x1 = _out_proj(o_a.reshape(b * s, D_SWA), o_b.reshape(b * s, D_FOX), xf,
                       grp_swa_g[l].reshape(1, D_SWA), grp_fox_g[l].reshape(1, D_FOX),
                       w_out[l].astype(bf16), post_mix_g[l].reshape(1, d))
        xf = _ffn(x1, pre_ffn_g[l].reshape(1, d), w_up[l].astype(bf16), conv_w[l],
                  conv_b[l].reshape(1, 2 * D_FF), w_down[l].astype(bf16), post_ffn_g[l].reshape(1, d), s)
    return xf.reshape(b, s, d)
```

```python
import functools

import numpy as np
import jax
import jax.numpy as jnp
from jax import lax
from jax.experimental import pallas as pl
from jax.experimental.pallas import tpu as pltpu

D_MODEL = 2048
HEAD_DIM = 64
N_HEADS_SWA = 16
N_KV_SWA = 2
N_HEADS_FOX = 16
WINDOW = 128
BLOCK = 128
D_FF = 5632
CONV_WIDTH = 3
EPS = 1e-6
D_SWA = N_HEADS_SWA * HEAD_DIM
D_KV_SWA = N_KV_SWA * HEAD_DIM
D_FOX = N_HEADS_FOX * HEAD_DIM
D_MIX = D_SWA + D_FOX
D_QKV = D_SWA + 2 * D_KV_SWA + 3 * D_FOX

LANES = 128
COL_QB = 0
COL_KB = COL_QB + D_FOX // LANES
COL_VB = COL_KB + D_FOX // LANES
COL_QA = COL_VB + D_FOX // LANES
COL_KA = COL_QA + D_SWA // LANES
COL_VA = COL_KA + 1

NEG = -1e30
VMEM_LIMIT = 56 * 1024 * 1024

ALIBI_SLOPES = [float(v) for v in np.asarray(2.0 ** (-8.0 * np.arange(1, N_HEADS_SWA + 1) / N_HEADS_SWA),
                                            dtype=np.float32)]

_NT = (((1,), (1,)), ((), ()))


def _rms(xf, g):
    return xf * lax.rsqrt(jnp.mean(xf * xf, axis=-1, keepdims=True) + EPS) * g


def _params(*sem):
    return pltpu.CompilerParams(dimension_semantics=sem, vmem_limit_bytes=VMEM_LIMIT)


IN_TM = 512
IN_CHUNK = 1024
LOG2E = float(np.log2(np.e))
FOX_Q_MULT = LOG2E / float(np.sqrt(HEAD_DIM))


def _in_segments():
    qb0, qb1 = COL_QB * LANES, COL_KB * LANES
    segs = []
    for lo, hi, mult in ((0, qb0, 1.0), (qb0, qb1, FOX_Q_MULT), (qb1, D_QKV, 1.0)):
        for n0 in range(lo, hi, IN_CHUNK):
            segs.append((n0, min(n0 + IN_CHUNK, hi), mult))
    return tuple(segs)


IN_SEGMENTS = _in_segments()


def _in_proj_kernel(x_ref, g_ref, w_ref, wf_ref, bf_ref, proj_ref, logf_ref):
    h = _rms(x_ref[...], g_ref[...]).astype(jnp.bfloat16)
    for n0, n1, mult in IN_SEGMENTS:
        acc = jnp.dot(h, w_ref[:, n0:n1], preferred_element_type=jnp.float32)
        if mult != 1.0:
            acc = acc * mult
        proj_ref[:, n0:n1] = acc.astype(jnp.bfloat16)
    f = jnp.dot(h, wf_ref[...], preferred_element_type=jnp.float32) + bf_ref[...]
    logf_ref[...] = jnp.minimum(f, 0.0) - jnp.log1p(jnp.exp(-jnp.abs(f)))


def _in_proj(x2d, g, w_qkv, w_f, b_f):
    m = x2d.shape[0]
    return pl.pallas_call(
        _in_proj_kernel,
        grid=(m // IN_TM,),
        in_specs=[
            pl.BlockSpec((IN_TM, D_MODEL), lambda i: (i, 0)),
            pl.BlockSpec((1, D_MODEL), lambda i: (0, 0)),
            pl.BlockSpec((D_MODEL, D_QKV), lambda i: (0, 0), pipeline_mode=pl.Buffered(1)),
            pl.BlockSpec((D_MODEL, LANES), lambda i: (0, 0), pipeline_mode=pl.Buffered(1)),
            pl.BlockSpec((1, LANES), lambda i: (0, 0)),
        ],
        out_specs=[
            pl.BlockSpec((IN_TM, D_QKV), lambda i: (i, 0)),
            pl.BlockSpec((IN_TM, LANES), lambda i: (i, 0)),
        ],
        out_shape=[
            jax.ShapeDtypeStruct((m, D_QKV), jnp.bfloat16),
            jax.ShapeDtypeStruct((m, LANES), jnp.float32),
        ],
        compiler_params=_params("arbitrary"),
        name="in_proj",
    )(x2d, g, w_qkv, w_f, b_f)


CS_BLK = 128
N_SPLIT = 3


def _bias_placement():
    place = np.zeros((N_SPLIT, LANES, D_FOX), np.float32)
    for h in range(N_HEADS_FOX):
        base = (h // 2) * LANES + (HEAD_DIM if h % 2 == 0 else 0)
        for i in range(N_SPLIT):
            place[i, h, base + i] = 1.0
    return place


def _cumsum_kernel(logf_ref, place_ref, a_ref, carry_ref):
    s = logf_ref.shape[1]
    r = lax.broadcasted_iota(jnp.int32, (CS_BLK, CS_BLK), 0)
    c = lax.broadcasted_iota(jnp.int32, (CS_BLK, CS_BLK), 1)
    lower = (r >= c).astype(jnp.float32)
    carry_ref[...] = jnp.zeros_like(carry_ref)

    def body(i, _):
        t0 = pl.multiple_of(i * CS_BLK, CS_BLK)
        blk = logf_ref[0, pl.ds(t0, CS_BLK), :]
        cur = jnp.dot(lower, blk, precision=lax.Precision.HIGHEST,
                      preferred_element_type=jnp.float32) + carry_ref[...]
        carry_ref[...] = cur[CS_BLK - 1:CS_BLK, :]
        rest = cur * (-LOG2E)
        out = jnp.zeros((CS_BLK, D_FOX), jnp.float32)
        for piece in range(N_SPLIT):
            part = rest.astype(jnp.bfloat16)
            rest = rest - part.astype(jnp.float32)
            out = out + jnp.dot(part, place_ref[piece], preferred_element_type=jnp.float32)
        a_ref[0, pl.ds(t0, CS_BLK), :] = out.astype(jnp.bfloat16)
        return 0

    lax.fori_loop(0, s // CS_BLK, body, 0)


def _cumsum(logf3):
    b, s, _ = logf3.shape
    place = jnp.asarray(_bias_placement(), jnp.bfloat16)
    return pl.pallas_call(
        _cumsum_kernel,
        grid=(b,),
        in_specs=[pl.BlockSpec((1, s, LANES), lambda i: (i, 0, 0)),
                  pl.BlockSpec((N_SPLIT, LANES, D_FOX), lambda i: (0, 0, 0))],
        out_specs=pl.BlockSpec((1, s, D_FOX), lambda i: (i, 0, 0)),
        out_shape=jax.ShapeDtypeStruct((b, s, D_FOX), jnp.bfloat16),
        scratch_shapes=[pltpu.VMEM((1, LANES), jnp.float32)],
        compiler_params=_params("arbitrary"),
        name="cumsum_logf",
    )(logf3, place)


SWA_TQ = 512
PAIRS_PER_KV = (N_HEADS_SWA // N_KV_SWA) // 2


def _swa_kernel(sink_ref, q_ref, k_ref, v_ref, o_ref, kpad, kswap, vpad, vswap):
    qi = pl.program_id(1)
    s_len = k_ref.shape[1]

    @pl.when(qi == 0)
    def _():
        zeros = jnp.zeros((BLOCK, LANES), jnp.bfloat16)
        for src, pad, swp in ((k_ref, kpad, kswap), (v_ref, vpad, vswap)):
            pad[0:BLOCK, :] = zeros
            swp[0:BLOCK, :] = zeros
            val = src[0]
            pad[BLOCK:, :] = val
            swp[BLOCK:, :] = pltpu.roll(val.astype(jnp.float32), HEAD_DIM, 1).astype(jnp.bfloat16)

    lane = lax.broadcasted_iota(jnp.int32, (2 * BLOCK, LANES), 1)
    low = lane < HEAD_DIM
    out_low = lax.broadcasted_iota(jnp.int32, (BLOCK, LANES), 1) < HEAD_DIM
    q_loc = BLOCK + lax.broadcasted_iota(jnp.int32, (BLOCK, 2 * BLOCK), 0)
    k_loc = lax.broadcasted_iota(jnp.int32, (BLOCK, 2 * BLOCK), 1)
    dist_i = q_loc - k_loc
    dist = dist_i.astype(jnp.float32)
    in_win = (dist_i >= 0) & (dist_i < WINDOW)
    scale = 1.0 / np.sqrt(HEAD_DIM)

    def block_body(r, _):
        n = qi * (SWA_TQ // BLOCK) + r
        row0 = pl.multiple_of(r * BLOCK, BLOCK)
        band0 = pl.multiple_of(n * BLOCK, BLOCK)
        k_abs = (n - 1) * BLOCK + k_loc
        mask = in_win & (k_abs >= 0)
        kp = kpad[pl.ds(band0, 2 * BLOCK), :]
        ks = kswap[pl.ds(band0, 2 * BLOCK), :]
        vp = vpad[pl.ds(band0, 2 * BLOCK), :]
        vs = vswap[pl.ds(band0, 2 * BLOCK), :]
        zero = jnp.zeros_like(kp)
        for kk in range(N_KV_SWA):
            k_lo = jnp.where(low, kp if kk == 0 else ks, zero)
            k_hi = jnp.where(low, zero, ks if kk == 0 else kp)
            v_lo = jnp.where(low, vp if kk == 0 else vs, zero)
            v_hi = jnp.where(low, zero, vs if kk == 0 else vp)
            qs = jnp.concatenate(
                [q_ref[0, pl.ds(row0, BLOCK), (kk * PAIRS_PER_KV + pi) * LANES:(kk * PAIRS_PER_KV + pi + 1) * LANES]
                 for pi in range(PAIRS_PER_KV)], axis=0)
            s_even = lax.dot_general(qs, k_lo, _NT, preferred_element_type=jnp.float32)
            s_odd = lax.dot_general(qs, k_hi, _NT, preferred_element_type=jnp.float32)
            p_parts = ([], [])
            inv_parts = []
            for pi in range(PAIRS_PER_KV):
                invs = []
                for par, s_all in enumerate((s_even, s_odd)):
                    h = kk * 2 * PAIRS_PER_KV + 2 * pi + par
                    sc = s_all[pi * BLOCK:(pi + 1) * BLOCK, :] * scale - ALIBI_SLOPES[h] * dist
                    sc = jnp.where(mask, sc, NEG)
                    sink = sink_ref[h]
                    m = jnp.maximum(jnp.max(sc, axis=-1, keepdims=True), sink)
                    p = jnp.exp(sc - m)
                    den = jnp.sum(p, axis=-1, keepdims=True) + jnp.exp(sink - m)
                    p_parts[par].append(p.astype(jnp.bfloat16))
                    invs.append(1.0 / den)
                inv_parts.append(invs)
            p_even = jnp.concatenate(p_parts[0], axis=0)
            p_odd = jnp.concatenate(p_parts[1], axis=0)
            o = (jnp.dot(p_even, v_lo, preferred_element_type=jnp.float32)
                 + jnp.dot(p_odd, v_hi, preferred_element_type=jnp.float32))
            for pi in range(PAIRS_PER_KV):
                inv = jnp.where(out_low, inv_parts[pi][0], inv_parts[pi][1])
                col = (kk * PAIRS_PER_KV + pi) * LANES
                o_ref[0, pl.ds(row0, BLOCK), col:col + LANES] = o[pi * BLOCK:(pi + 1) * BLOCK, :] * inv
        return 0

    lax.fori_loop(0, SWA_TQ // BLOCK, block_body, 0)
    del s_len


def _swa(proj3, sinks):
    b, s, _ = proj3.shape
    pad_shape = pltpu.VMEM((s + BLOCK, LANES), jnp.bfloat16)
    return pl.pallas_call(
        _swa_kernel,
        grid_spec=pltpu.PrefetchScalarGridSpec(
            num_scalar_prefetch=0,
            grid=(b, s // SWA_TQ),
            in_specs=[
                pl.BlockSpec(memory_space=pltpu.SMEM),
                pl.BlockSpec((1, SWA_TQ, D_SWA), lambda bi, qi: (bi, qi, COL_QA * LANES // D_SWA)),
                pl.BlockSpec((1, s, LANES), lambda bi, qi: (bi, 0, COL_KA)),
                pl.BlockSpec((1, s, LANES), lambda bi, qi: (bi, 0, COL_VA)),
            ],
            out_specs=pl.BlockSpec((1, SWA_TQ, D_SWA), lambda bi, qi: (bi, qi, 0)),
            scratch_shapes=[pad_shape, pad_shape, pad_shape, pad_shape],
        ),
        out_shape=jax.ShapeDtypeStruct((b, s, D_SWA), jnp.float32),
        compiler_params=_params("arbitrary", "arbitrary"),
        name="swa_attention",
    )(sinks, proj3, proj3, proj3)


FOX_TQ = 512
FOX_TK = 512
FOX_PAIRS = 2
assert all(col % FOX_PAIRS == 0 for col in (COL_QB, COL_KB, COL_VB))


def _fox_kernel(q_ref, k_ref, v_ref, a_ref, o_ref, ka_scr, vat_scr, m_scr, acc_scr):
    qi = pl.program_id(2)
    s_len = k_ref.shape[1]
    low_k = lax.broadcasted_iota(jnp.int32, (FOX_TK, LANES), 1) < HEAD_DIM
    keeps = (low_k, jnp.logical_not(low_k))

    heads = range(2 * FOX_PAIRS)

    @pl.when(qi == 0)
    def _():
        for c0 in range(0, s_len, FOX_TK):
            for pair in range(FOX_PAIRS):
                cols = slice(pair * LANES, (pair + 1) * LANES)
                kc = k_ref[0, c0:c0 + FOX_TK, cols]
                vc = v_ref[0, c0:c0 + FOX_TK, cols].astype(jnp.float32)
                ac = a_ref[0, c0:c0 + FOX_TK, cols]
                for par in range(2):
                    h = 2 * pair + par
                    ka_scr[h, c0:c0 + FOX_TK, :] = jnp.where(keeps[par], kc, ac)
                    vat_scr[h, :, c0:c0 + FOX_TK] = jnp.where(keeps[par], vc, 1.0).T.astype(jnp.bfloat16)

    lane_q = lax.broadcasted_iota(jnp.int32, (FOX_TQ, LANES), 1)
    ones_hi = jnp.where((lane_q >= HEAD_DIM) & (lane_q < HEAD_DIM + N_SPLIT), 1.0, 0.0).astype(jnp.bfloat16)
    ones_lo = jnp.where(lane_q < N_SPLIT, 1.0, 0.0).astype(jnp.bfloat16)
    qa = []
    for pair in range(FOX_PAIRS):
        q = q_ref[0, :, pair * LANES:(pair + 1) * LANES]
        qa += [jnp.where(lane_q < HEAD_DIM, q, ones_hi), jnp.where(lane_q < HEAD_DIM, ones_lo, q)]
    key_row = lax.broadcasted_iota(jnp.int32, (FOX_TK, FOX_TQ), 0)
    qry_col = lax.broadcasted_iota(jnp.int32, (FOX_TK, FOX_TQ), 1)

    m_scr[...] = jnp.full(m_scr.shape, NEG, jnp.float32)
    acc_scr[...] = jnp.zeros(acc_scr.shape, jnp.float32)

    def chunk(j, key_offset):
        k0 = pl.multiple_of(j * FOX_TK, FOX_TK)
        sts = [lax.dot_general(ka_scr[h, pl.ds(k0, FOX_TK), :], qa[h], _NT,
                               preferred_element_type=jnp.float32) for h in heads]
        for h in heads:
            st = sts[h]
            if key_offset is not None:
                st = jnp.where(key_row + key_offset <= qry_col, st, NEG)
            m_old = m_scr[h]
            m_new = jnp.maximum(m_old, jnp.max(st, axis=0, keepdims=True))
            alpha = jnp.exp2(m_old - m_new)
            pt = jnp.exp2(st - m_new).astype(jnp.bfloat16)
            acc_scr[h] = alpha * acc_scr[h] + jnp.dot(vat_scr[h, :, pl.ds(k0, FOX_TK)], pt,
                                                      preferred_element_type=jnp.float32)
            m_scr[h] = m_new

    def body(j, _):
        chunk(j, None)
        return 0

    chunks_per_tile = FOX_TQ // FOX_TK
    lax.fori_loop(0, qi * chunks_per_tile, body, 0)
    for d in range(chunks_per_tile):
        chunk(qi * chunks_per_tile + d, d * FOX_TK)

    for pair in range(FOX_PAIRS):
        acc0 = acc_scr[2 * pair]
        acc1 = acc_scr[2 * pair + 1]
        ot = jnp.concatenate([acc0[0:HEAD_DIM] / acc0[HEAD_DIM:HEAD_DIM + 1],
                              acc1[HEAD_DIM:LANES] / acc1[0:1]], axis=0)
        o_ref[0, :, pair * LANES:(pair + 1) * LANES] = ot.T


def _fox(proj3, aug):
    b, s, _ = proj3.shape
    width = FOX_PAIRS * LANES
    ngroups = D_FOX // width
    nheads = 2 * FOX_PAIRS
    once = pl.Buffered(1)
    return pl.pallas_call(
        _fox_kernel,
        grid=(b, ngroups, s // FOX_TQ),
        in_specs=[
            pl.BlockSpec((1, FOX_TQ, width), lambda bi, p, qi: (bi, qi, COL_QB // FOX_PAIRS + p)),
            pl.BlockSpec((1, s, width), lambda bi, p, qi: (bi, 0, COL_KB // FOX_PAIRS + p), pipeline_mode=once),
            pl.BlockSpec((1, s, width), lambda bi, p, qi: (bi, 0, COL_VB // FOX_PAIRS + p), pipeline_mode=once),
            pl.BlockSpec((1, s, width), lambda bi, p, qi: (bi, 0, p), pipeline_mode=once),
        ],
        out_specs=pl.BlockSpec((1, FOX_TQ, width), lambda bi, p, qi: (bi, qi, p)),
        out_shape=jax.ShapeDtypeStruct((b, s, D_FOX), jnp.float32),
        scratch_shapes=[
            pltpu.VMEM((nheads, s, LANES), jnp.bfloat16),
            pltpu.VMEM((nheads, LANES, s), jnp.bfloat16),
            pltpu.VMEM((nheads, 1, FOX_TQ), jnp.float32),
            pltpu.VMEM((nheads, LANES, FOX_TQ), jnp.float32),
        ],
        compiler_params=_params("arbitrary", "arbitrary", "arbitrary"),
        name="fox_attention",
    )(proj3, proj3, proj3, aug)


OUT_TM = 512


def _out_proj_kernel(oa_ref, ob_ref, x_ref, ga_ref, gb_ref, w_ref, gp_ref, x1_ref):
    na = _rms(oa_ref[...], ga_ref[...]).astype(jnp.bfloat16)
    nb = _rms(ob_ref[...], gb_ref[...]).astype(jnp.bfloat16)
    mix = (jnp.dot(na, w_ref[0:D_SWA, :], preferred_element_type=jnp.float32)
           + jnp.dot(nb, w_ref[D_SWA:D_MIX, :], preferred_element_type=jnp.float32))
    x1_ref[...] = x_ref[...] + _rms(mix, gp_ref[...])


def _out_proj(oa, ob, x2d, ga, gb, w_out, gp):
    m = x2d.shape[0]
    row = lambda i: (i, 0)
    fixed = lambda i: (0, 0)
    return pl.pallas_call(
        _out_proj_kernel,
        grid=(m // OUT_TM,),
        in_specs=[
            pl.BlockSpec((OUT_TM, D_SWA), row),
            pl.BlockSpec((OUT_TM, D_FOX), row),
            pl.BlockSpec((OUT_TM, D_MODEL), row),
            pl.BlockSpec((1, D_SWA), fixed),
            pl.BlockSpec((1, D_FOX), fixed),
            pl.BlockSpec((D_MIX, D_MODEL), fixed, pipeline_mode=pl.Buffered(1)),
            pl.BlockSpec((1, D_MODEL), fixed),
        ],
        out_specs=pl.BlockSpec((OUT_TM, D_MODEL), row),
        out_shape=jax.ShapeDtypeStruct((m, D_MODEL), jnp.float32),
        compiler_params=_params("arbitrary"),
        name="out_proj",
    )(oa, ob, x2d, ga, gb, w_out, gp)


FFN_TM = 512
FFN_TF = 512
HALO = 16
N_FT = D_FF // FFN_TF


def _gelu_tanh(x):
    return 0.5 * x * (1.0 + jnp.tanh(np.sqrt(2.0 / np.pi) * (x + 0.044715 * (x * x * x))))


def _ffn_kernel(x_ref, halo_ref, g_ref, wg_ref, wv_ref, cwg_ref, cwv_ref, cbg_ref, cbv_ref, wd_ref, gp_ref,
                o_ref, h_scr, ug_scr, uv_scr, *, tiles_per_seq):
    i = pl.program_id(0)
    j = pl.program_id(1)

    @pl.when(j == 0)
    def _():
        g = g_ref[...]
        hh = _rms(halo_ref[...], g)
        hh = jnp.where(i % tiles_per_seq == 0, 0.0, hh)
        top = jnp.concatenate([jnp.zeros_like(hh), hh], axis=0)
        h_scr[0:HALO, :] = top.astype(jnp.bfloat16)
        h_scr[HALO:, :] = _rms(x_ref[...], g).astype(jnp.bfloat16)

    h = h_scr[...]
    ug_scr[...] = jnp.dot(h, wg_ref[...], preferred_element_type=jnp.float32)
    uv_scr[...] = jnp.dot(h, wv_ref[...], preferred_element_type=jnp.float32)

    def conv(u_scr, cw_ref, cb_ref):
        y = cb_ref[...]
        for kk in range(CONV_WIDTH):
            y = y + u_scr[pl.ds(HALO - (CONV_WIDTH - 1) + kk, FFN_TM), :] * cw_ref[kk:kk + 1, :]
        return y

    gate = conv(ug_scr, cwg_ref, cbg_ref)
    val = conv(uv_scr, cwv_ref, cbv_ref)
    a = (_gelu_tanh(gate) * val).astype(jnp.bfloat16)
    part = jnp.dot(a, wd_ref[...], preferred_element_type=jnp.float32)

    @pl.when(j == 0)
    def _():
        o_ref[...] = part

    @pl.when(j > 0)
    def _():
        o_ref[...] += part

    @pl.when(j == N_FT - 1)
    def _():
        o_ref[...] = x_ref[...] + _rms(o_ref[...], gp_ref[...])


def _ffn(x1, g, w_up, conv_w, conv_b, w_down, gp, seq_len):
    m = x1.shape[0]
    halo_blocks = FFN_TM // 8
    kern = functools.partial(_ffn_kernel, tiles_per_seq=seq_len // FFN_TM)
    return pl.pallas_call(
        kern,
        grid=(m // FFN_TM, N_FT),
        in_specs=[
            pl.BlockSpec((FFN_TM, D_MODEL), lambda i, j: (i, 0)),
            pl.BlockSpec((8, D_MODEL), lambda i, j: (jnp.maximum(i * halo_blocks - 1, 0), 0)),
            pl.BlockSpec((1, D_MODEL), lambda i, j: (0, 0)),
            pl.BlockSpec((D_MODEL, FFN_TF), lambda i, j: (0, j)),
            pl.BlockSpec((D_MODEL, FFN_TF), lambda i, j: (0, j + N_FT)),
            pl.BlockSpec((CONV_WIDTH, FFN_TF), lambda i, j: (0, j)),
            pl.BlockSpec((CONV_WIDTH, FFN_TF), lambda i, j: (0, j + N_FT)),
            pl.BlockSpec((1, FFN_TF), lambda i, j: (0, j)),
            pl.BlockSpec((1, FFN_TF), lambda i, j: (0, j + N_FT)),
            pl.BlockSpec((FFN_TF, D_MODEL), lambda i, j: (j, 0)),
            pl.BlockSpec((1, D_MODEL), lambda i, j: (0, 0)),
        ],
        out_specs=pl.BlockSpec((FFN_TM, D_MODEL), lambda i, j: (i, 0)),
        out_shape=jax.ShapeDtypeStruct((m, D_MODEL), jnp.float32),
        scratch_shapes=[
            pltpu.VMEM((FFN_TM + HALO, D_MODEL), jnp.bfloat16),
            pltpu.VMEM((FFN_TM + HALO, FFN_TF), jnp.float32),
            pltpu.VMEM((FFN_TM + HALO, FFN_TF), jnp.float32),
        ],
        compiler_params=_params("arbitrary", "arbitrary"),
        name="conv_geglu_ffn",
    )(x1, x1, g, w_up, w_up, conv_w, conv_w, conv_b, conv_b, w_down, gp)


def kernel(x, pre_mix_g, w_in, b_forget, sinks, grp_swa_g, grp_fox_g, w_out, post_mix_g,
           pre_ffn_g, w_up, conv_w, conv_b, w_down, post_ffn_g):
    b, s, d = x.shape
    depth = w_in.shape[0]
    bf16 = jnp.bfloat16
    xf = x.reshape(b * s, d)
    for l in range(depth):
        w_qkv = jnp.concatenate([w_in[l, :, D_SWA + 2 * D_KV_SWA:D_QKV], w_in[l, :, :D_SWA + 2 * D_KV_SWA]],
                                axis=1).astype(bf16)
        w_f = jnp.pad(w_in[l, :, D_QKV:], ((0, 0), (0, LANES - N_HEADS_FOX))).astype(bf16)
        b_f = jnp.pad(b_forget[l], (0, LANES - N_HEADS_FOX)).reshape(1, LANES)

        proj, logf = _in_proj(xf, pre_mix_g[l].reshape(1, d), w_qkv, w_f, b_f)
        proj3 = proj.reshape(b, s, D_QKV)
        aug = _cumsum(logf.reshape(b, s, LANES))

        o_a = _swa(proj3, sinks[l])
        o_b = _fox(proj3, aug)

        x1 = _out_proj(o_a.reshape(b * s, D_SWA), o_b.reshape(b * s, D_FOX), xf,
                       grp_swa_g[l].reshape(1, D_SWA), grp_fox_g[l].reshape(1, D_FOX),
                       w_out[l].astype(bf16), post_mix_g[l].reshape(1, d))
        xf = _ffn(x1, pre_ffn_g[l].reshape(1, d), w_up[l].astype(bf16), conv_w[l],
                  conv_b[l].reshape(1, 2 * D_FF), w_down[l].astype(bf16), post_ffn_g[l].reshape(1, d), s)
    return xf.reshape(b, s, d)
```

```python
import functools

import numpy as np
import jax
import jax.numpy as jnp
from jax import lax
from jax.experimental import pallas as pl
from jax.experimental.pallas import tpu as pltpu

D_MODEL = 2048
HEAD_DIM = 64
N_HEADS_SWA = 16
N_KV_SWA = 2
N_HEADS_FOX = 16
WINDOW = 128
BLOCK = 128
D_FF = 5632
CONV_WIDTH = 3
EPS = 1e-6
D_SWA = N_HEADS_SWA * HEAD_DIM
D_KV_SWA = N_KV_SWA * HEAD_DIM
D_FOX = N_HEADS_FOX * HEAD_DIM
D_MIX = D_SWA + D_FOX
D_QKV = D_SWA + 2 * D_KV_SWA + 3 * D_FOX

LANES = 128
COL_QB = 0
COL_KB = COL_QB + D_FOX // LANES
COL_VB = COL_KB + D_FOX // LANES
COL_QA = COL_VB + D_FOX // LANES
COL_KA = COL_QA + D_SWA // LANES
COL_VA = COL_KA + 1

NEG = -1e30
VMEM_LIMIT = 56 * 1024 * 1024

ALIBI_SLOPES = [float(v) for v in np.asarray(2.0 ** (-8.0 * np.arange(1, N_HEADS_SWA + 1) / N_HEADS_SWA),
                                            dtype=np.float32)]

_NT = (((1,), (1,)), ((), ()))


def _rms(xf, g):
    return xf * lax.rsqrt(jnp.mean(xf * xf, axis=-1, keepdims=True) + EPS) * g


def _params(*sem):
    return pltpu.CompilerParams(dimension_semantics=sem, vmem_limit_bytes=VMEM_LIMIT)


IN_TM = 512
IN_CHUNK = 1024
LOG2E = float(np.log2(np.e))
FOX_Q_MULT = LOG2E / float(np.sqrt(HEAD_DIM))


def _in_segments():
    qb0, qb1 = COL_QB * LANES, COL_KB * LANES
    segs = []
    for lo, hi, mult in ((0, qb0, 1.0), (qb0, qb1, FOX_Q_MULT), (qb1, D_QKV, 1.0)):
        for n0 in range(lo, hi, IN_CHUNK):
            segs.append((n0, min(n0 + IN_CHUNK, hi), mult))
    return tuple(segs)


IN_SEGMENTS = _in_segments()


def _in_proj_kernel(x_ref, g_ref, w_ref, wf_ref, bf_ref, proj_ref, logf_ref):
    h = _rms(x_ref[...], g_ref[...]).astype(jnp.bfloat16)
    for n0, n1, mult in IN_SEGMENTS:
        acc = jnp.dot(h, w_ref[:, n0:n1], preferred_element_type=jnp.float32)
        if mult != 1.0:
            acc = acc * mult
        proj_ref[:, n0:n1] = acc.astype(jnp.bfloat16)
    f = jnp.dot(h, wf_ref[...], preferred_element_type=jnp.float32) + bf_ref[...]
    logf_ref[...] = jnp.minimum(f, 0.0) - jnp.log1p(jnp.exp(-jnp.abs(f)))


def _in_proj(x2d, g, w_qkv, w_f, b_f):
    m = x2d.shape[0]
    return pl.pallas_call(
        _in_proj_kernel,
        grid=(m // IN_TM,),
        in_specs=[
            pl.BlockSpec((IN_TM, D_MODEL), lambda i: (i, 0)),
            pl.BlockSpec((1, D_MODEL), lambda i: (0, 0)),
            pl.BlockSpec((D_MODEL, D_QKV), lambda i: (0, 0), pipeline_mode=pl.Buffered(1)),
            pl.BlockSpec((D_MODEL, LANES), lambda i: (0, 0), pipeline_mode=pl.Buffered(1)),
            pl.BlockSpec((1, LANES), lambda i: (0, 0)),
        ],
        out_specs=[
            pl.BlockSpec((IN_TM, D_QKV), lambda i: (i, 0)),
            pl.BlockSpec((IN_TM, LANES), lambda i: (i, 0)),
        ],
        out_shape=[
            jax.ShapeDtypeStruct((m, D_QKV), jnp.bfloat16),
            jax.ShapeDtypeStruct((m, LANES), jnp.float32),
        ],
        compiler_params=_params("arbitrary"),
        name="in_proj",
    )(x2d, g, w_qkv, w_f, b_f)


CS_BLK = 128
N_SPLIT = 3


def _bias_placement():
    place = np.zeros((N_SPLIT, LANES, D_FOX), np.float32)
    for h in range(N_HEADS_FOX):
        base = (h // 2) * LANES + (HEAD_DIM if h % 2 == 0 else 0)
        for i in range(N_SPLIT):
            place[i, h, base + i] = 1.0
    return place


def _cumsum_kernel(logf_ref, place_ref, a_ref, w_scr):
    s = logf_ref.shape[1]
    nblk = s // CS_BLK
    exact = dict(precision=lax.Precision.HIGHEST, preferred_element_type=jnp.float32)
    r = lax.broadcasted_iota(jnp.int32, (CS_BLK, CS_BLK), 0)
    c = lax.broadcasted_iota(jnp.int32, (CS_BLK, CS_BLK), 1)
    lower = (r >= c).astype(jnp.float32)
    blocks = [slice(b * CS_BLK, (b + 1) * CS_BLK) for b in range(nblk)]
    for rows in blocks:
        w_scr[rows, :] = jnp.dot(lower, logf_ref[0, rows, :], **exact)
    totals = w_scr[pl.ds(CS_BLK - 1, nblk, stride=CS_BLK), :]
    rb = lax.broadcasted_iota(jnp.int32, (nblk, nblk), 0)
    cb = lax.broadcasted_iota(jnp.int32, (nblk, nblk), 1)
    carry = jnp.dot((rb > cb).astype(jnp.float32), totals, **exact)
    for b, rows in enumerate(blocks):
        rest = (w_scr[rows, :] + carry[b:b + 1, :]) * (-LOG2E)
        parts = []
        for _ in range(N_SPLIT):
            part = rest.astype(jnp.bfloat16)
            rest = rest - part.astype(jnp.float32)
            parts.append(part)
        a_ref[0, rows, :] = jnp.dot(jnp.concatenate(parts, axis=1), place_ref[...],
                                    preferred_element_type=jnp.float32).astype(jnp.bfloat16)


def _cumsum(logf3):
    b, s, _ = logf3.shape
    place = jnp.asarray(_bias_placement().reshape(N_SPLIT * LANES, D_FOX), jnp.bfloat16)
    return pl.pallas_call(
        _cumsum_kernel,
        grid=(b,),
        in_specs=[pl.BlockSpec((1, s, LANES), lambda i: (i, 0, 0)),
                  pl.BlockSpec((N_SPLIT * LANES, D_FOX), lambda i: (0, 0))],
        out_specs=pl.BlockSpec((1, s, D_FOX), lambda i: (i, 0, 0)),
        out_shape=jax.ShapeDtypeStruct((b, s, D_FOX), jnp.bfloat16),
        scratch_shapes=[pltpu.VMEM((s, LANES), jnp.float32)],
        compiler_params=_params("arbitrary"),
        name="cumsum_logf",
    )(logf3, place)


SWA_TQ = 512
PAIRS_PER_KV = (N_HEADS_SWA // N_KV_SWA) // 2


def _swa_kernel(sink_ref, q_ref, k_ref, v_ref, o_ref, kpad, kswap, vpad, vswap):
    qi = pl.program_id(1)
    s_len = k_ref.shape[1]

    @pl.when(qi == 0)
    def _():
        zeros = jnp.zeros((BLOCK, LANES), jnp.bfloat16)
        for src, pad, swp in ((k_ref, kpad, kswap), (v_ref, vpad, vswap)):
            pad[0:BLOCK, :] = zeros
            swp[0:BLOCK, :] = zeros
            val = src[0]
            pad[BLOCK:, :] = val
            swp[BLOCK:, :] = pltpu.roll(val.astype(jnp.float32), HEAD_DIM, 1).astype(jnp.bfloat16)

    lane = lax.broadcasted_iota(jnp.int32, (2 * BLOCK, LANES), 1)
    low = lane < HEAD_DIM
    out_low = lax.broadcasted_iota(jnp.int32, (BLOCK, LANES), 1) < HEAD_DIM
    q_loc = BLOCK + lax.broadcasted_iota(jnp.int32, (BLOCK, 2 * BLOCK), 0)
    k_loc = lax.broadcasted_iota(jnp.int32, (BLOCK, 2 * BLOCK), 1)
    dist_i = q_loc - k_loc
    dist = dist_i.astype(jnp.float32)
    in_win = (dist_i >= 0) & (dist_i < WINDOW)
    scale = 1.0 / np.sqrt(HEAD_DIM)

    def block_body(r, _):
        n = qi * (SWA_TQ // BLOCK) + r
        row0 = pl.multiple_of(r * BLOCK, BLOCK)
        band0 = pl.multiple_of(n * BLOCK, BLOCK)
        k_abs = (n - 1) * BLOCK + k_loc
        mask = in_win & (k_abs >= 0)
        kp = kpad[pl.ds(band0, 2 * BLOCK), :]
        ks = kswap[pl.ds(band0, 2 * BLOCK), :]
        vp = vpad[pl.ds(band0, 2 * BLOCK), :]
        vs = vswap[pl.ds(band0, 2 * BLOCK), :]
        zero = jnp.zeros_like(kp)
        for kk in range(N_KV_SWA):
            k_lo = jnp.where(low, kp if kk == 0 else ks, zero)
            k_hi = jnp.where(low, zero, ks if kk == 0 else kp)
            v_lo = jnp.where(low, vp if kk == 0 else vs, zero)
            v_hi = jnp.where(low, zero, vs if kk == 0 else vp)
            qs = jnp.concatenate(
                [q_ref[0, pl.ds(row0, BLOCK), (kk * PAIRS_PER_KV + pi) * LANES:(kk * PAIRS_PER_KV + pi + 1) * LANES]
                 for pi in range(PAIRS_PER_KV)], axis=0)
            s_even = lax.dot_general(qs, k_lo, _NT, preferred_element_type=jnp.float32)
            s_odd = lax.dot_general(qs, k_hi, _NT, preferred_element_type=jnp.float32)
            p_parts = ([], [])
            inv_parts = []
            for pi in range(PAIRS_PER_KV):
                invs = []
                for par, s_all in enumerate((s_even, s_odd)):
                    h = kk * 2 * PAIRS_PER_KV + 2 * pi + par
                    sc = s_all[pi * BLOCK:(pi + 1) * BLOCK, :] * scale - ALIBI_SLOPES[h] * dist
                    sc = jnp.where(mask, sc, NEG)
                    sink = sink_ref[h]
                    m = jnp.maximum(jnp.max(sc, axis=-1, keepdims=True), sink)
                    p = jnp.exp(sc - m)
                    den = jnp.sum(p, axis=-1, keepdims=True) + jnp.exp(sink - m)
                    p_parts[par].append(p.astype(jnp.bfloat16))
                    invs.append(1.0 / den)
                inv_parts.append(invs)
            p_even = jnp.concatenate(p_parts[0], axis=0)
            p_odd = jnp.concatenate(p_parts[1], axis=0)
            o = (jnp.dot(p_even, v_lo, preferred_element_type=jnp.float32)
                 + jnp.dot(p_odd, v_hi, preferred_element_type=jnp.float32))
            for pi in range(PAIRS_PER_KV):
                inv = jnp.where(out_low, inv_parts[pi][0], inv_parts[pi][1])
                col = (kk * PAIRS_PER_KV + pi) * LANES
                o_ref[0, pl.ds(row0, BLOCK), col:col + LANES] = o[pi * BLOCK:(pi + 1) * BLOCK, :] * inv
        return 0

    lax.fori_loop(0, SWA_TQ // BLOCK, block_body, 0)
    del s_len


def _swa(proj3, sinks):
    b, s, _ = proj3.shape
    pad_shape = pltpu.VMEM((s + BLOCK, LANES), jnp.bfloat16)
    return pl.pallas_call(
        _swa_kernel,
        grid_spec=pltpu.PrefetchScalarGridSpec(
            num_scalar_prefetch=0,
            grid=(b, s // SWA_TQ),
            in_specs=[
                pl.BlockSpec(memory_space=pltpu.SMEM),
                pl.BlockSpec((1, SWA_TQ, D_SWA), lambda bi, qi: (bi, qi, COL_QA * LANES // D_SWA)),
                pl.BlockSpec((1, s, LANES), lambda bi, qi: (bi, 0, COL_KA)),
                pl.BlockSpec((1, s, LANES), lambda bi, qi: (bi, 0, COL_VA)),
            ],
            out_specs=pl.BlockSpec((1, SWA_TQ, D_SWA), lambda bi, qi: (bi, qi, 0)),
            scratch_shapes=[pad_shape, pad_shape, pad_shape, pad_shape],
        ),
        out_shape=jax.ShapeDtypeStruct((b, s, D_SWA), jnp.float32),
        compiler_params=_params("arbitrary", "arbitrary"),
        name="swa_attention",
    )(sinks, proj3, proj3, proj3)


FOX_TQ = 512
FOX_TK = FOX_TQ
FOX_PAIRS = 2
assert all(col % FOX_PAIRS == 0 for col in (COL_QB, COL_KB, COL_VB))


def _fox_kernel(q_ref, k_ref, v_ref, a_ref, o_ref, ka_scr, vat_scr, m_scr, acc_scr, st_scr):
    qi = pl.program_id(2)
    s_len = k_ref.shape[1]
    low_k = lax.broadcasted_iota(jnp.int32, (FOX_TK, LANES), 1) < HEAD_DIM
    keeps = (low_k, jnp.logical_not(low_k))

    heads = range(2 * FOX_PAIRS)

    @pl.when(qi == 0)
    def _():
        for c0 in range(0, s_len, FOX_TK):
            for pair in range(FOX_PAIRS):
                cols = slice(pair * LANES, (pair + 1) * LANES)
                kc = k_ref[0, c0:c0 + FOX_TK, cols]
                vc = v_ref[0, c0:c0 + FOX_TK, cols].astype(jnp.float32)
                ac = a_ref[0, c0:c0 + FOX_TK, cols]
                for par in range(2):
                    h = 2 * pair + par
                    ka_scr[h, c0:c0 + FOX_TK, :] = jnp.where(keeps[par], kc, ac)
                    vat_scr[h, :, c0:c0 + FOX_TK] = jnp.where(keeps[par], vc, 1.0).T.astype(jnp.bfloat16)

    lane_q = lax.broadcasted_iota(jnp.int32, (FOX_TQ, LANES), 1)
    ones_hi = jnp.where((lane_q >= HEAD_DIM) & (lane_q < HEAD_DIM + N_SPLIT), 1.0, 0.0).astype(jnp.bfloat16)
    ones_lo = jnp.where(lane_q < N_SPLIT, 1.0, 0.0).astype(jnp.bfloat16)
    qa = []
    for pair in range(FOX_PAIRS):
        q = q_ref[0, :, pair * LANES:(pair + 1) * LANES]
        qa += [jnp.where(lane_q < HEAD_DIM, q, ones_hi), jnp.where(lane_q < HEAD_DIM, ones_lo, q)]
    key_row = lax.broadcasted_iota(jnp.int32, (FOX_TK, FOX_TQ), 0)
    qry_col = lax.broadcasted_iota(jnp.int32, (FOX_TK, FOX_TQ), 1)

    m_scr[...] = jnp.full(m_scr.shape, NEG, jnp.float32)
    acc_scr[...] = jnp.zeros(acc_scr.shape, jnp.float32)

    def scores(j, slot):
        k0 = pl.multiple_of(j * FOX_TK, FOX_TK)
        for h in heads:
            st_scr[slot, h] = lax.dot_general(ka_scr[h, pl.ds(k0, FOX_TK), :], qa[h], _NT,
                                              preferred_element_type=jnp.float32)

    def update(j, slot, diagonal):
        k0 = pl.multiple_of(j * FOX_TK, FOX_TK)
        for h in heads:
            st = st_scr[slot, h]
            if diagonal:
                st = jnp.where(key_row <= qry_col, st, NEG)
            m_old = m_scr[h]
            m_new = jnp.maximum(m_old, jnp.max(st, axis=0, keepdims=True))
            alpha = jnp.exp2(m_old - m_new)
            pt = jnp.exp2(st - m_new).astype(jnp.bfloat16)
            acc_scr[h] = alpha * acc_scr[h] + jnp.dot(vat_scr[h, :, pl.ds(k0, FOX_TK)], pt,
                                                      preferred_element_type=jnp.float32)
            m_scr[h] = m_new

    scores(0, 0)

    def body(jj, _):
        j = 2 * jj
        scores(j + 1, 1)
        update(j, 0, False)
        scores(j + 2, 0)
        update(j + 1, 1, False)
        return 0

    lax.fori_loop(0, qi // 2, body, 0)

    @pl.when(qi % 2 == 1)
    def _():
        scores(qi, 1)
        update(qi - 1, 0, False)
        update(qi, 1, True)

    @pl.when(qi % 2 == 0)
    def _():
        update(qi, 0, True)

    for pair in range(FOX_PAIRS):
        acc0 = acc_scr[2 * pair]
        acc1 = acc_scr[2 * pair + 1]
        ot = jnp.concatenate([acc0[0:HEAD_DIM] / acc0[HEAD_DIM:HEAD_DIM + 1],
                              acc1[HEAD_DIM:LANES] / acc1[0:1]], axis=0)
        o_ref[0, :, pair * LANES:(pair + 1) * LANES] = ot.T


def _fox(proj3, aug):
    b, s, _ = proj3.shape
    width = FOX_PAIRS * LANES
    ngroups = D_FOX // width
    nheads = 2 * FOX_PAIRS
    once = pl.Buffered(1)
    return pl.pallas_call(
        _fox_kernel,
        grid=(b, ngroups, s // FOX_TQ),
        in_specs=[
            pl.BlockSpec((1, FOX_TQ, width), lambda bi, p, qi: (bi, qi, COL_QB // FOX_PAIRS + p)),
            pl.BlockSpec((1, s, width), lambda bi, p, qi: (bi, 0, COL_KB // FOX_PAIRS + p), pipeline_mode=once),
            pl.BlockSpec((1, s, width), lambda bi, p, qi: (bi, 0, COL_VB // FOX_PAIRS + p), pipeline_mode=once),
            pl.BlockSpec((1, s, width), lambda bi, p, qi: (bi, 0, p), pipeline_mode=once),
        ],
        out_specs=pl.BlockSpec((1, FOX_TQ, width), lambda bi, p, qi: (bi, qi, p)),
        out_shape=jax.ShapeDtypeStruct((b, s, D_FOX), jnp.float32),
        scratch_shapes=[
            pltpu.VMEM((nheads, s, LANES), jnp.bfloat16),
            pltpu.VMEM((nheads, LANES, s), jnp.bfloat16),
            pltpu.VMEM((nheads, 1, FOX_TQ), jnp.float32),
            pltpu.VMEM((nheads, LANES, FOX_TQ), jnp.float32),
            pltpu.VMEM((2, nheads, FOX_TK, FOX_TQ), jnp.float32),
        ],
        compiler_params=_params("arbitrary", "arbitrary", "arbitrary"),
        name="fox_attention",
    )(proj3, proj3, proj3, aug)


OUT_TM = 512


def _out_proj_kernel(oa_ref, ob_ref, x_ref, ga_ref, gb_ref, w_ref, gp_ref, x1_ref):
    na = _rms(oa_ref[...], ga_ref[...]).astype(jnp.bfloat16)
    nb = _rms(ob_ref[...], gb_ref[...]).astype(jnp.bfloat16)
    mix = (jnp.dot(na, w_ref[0:D_SWA, :], preferred_element_type=jnp.float32)
           + jnp.dot(nb, w_ref[D_SWA:D_MIX, :], preferred_element_type=jnp.float32))
    x1_ref[...] = x_ref[...] + _rms(mix, gp_ref[...])


def _out_proj(oa, ob, x2d, ga, gb, w_out, gp):
    m = x2d.shape[0]
    row = lambda i: (i, 0)
    fixed = lambda i: (0, 0)
    return pl.pallas_call(
        _out_proj_kernel,
        grid=(m // OUT_TM,),
        in_specs=[
            pl.BlockSpec((OUT_TM, D_SWA), row),
            pl.BlockSpec((OUT_TM, D_FOX), row),
            pl.BlockSpec((OUT_TM, D_MODEL), row),
            pl.BlockSpec((1, D_SWA), fixed),
            pl.BlockSpec((1, D_FOX), fixed),
            pl.BlockSpec((D_MIX, D_MODEL), fixed, pipeline_mode=pl.Buffered(1)),
            pl.BlockSpec((1, D_MODEL), fixed),
        ],
        out_specs=pl.BlockSpec((OUT_TM, D_MODEL), row),
        out_shape=jax.ShapeDtypeStruct((m, D_MODEL), jnp.float32),
        compiler_params=_params("arbitrary"),
        name="out_proj",
    )(oa, ob, x2d, ga, gb, w_out, gp)


FFN_TM = 512
FFN_TF = 512
HALO = 16
N_FT = D_FF // FFN_TF


def _gelu_tanh(x):
    return 0.5 * x * (1.0 + jnp.tanh(np.sqrt(2.0 / np.pi) * (x + 0.044715 * (x * x * x))))


def _ffn_kernel(x_ref, halo_ref, g_ref, wg_ref, wv_ref, cwg_ref, cwv_ref, cbg_ref, cbv_ref, wd_ref, gp_ref,
                o_ref, h_scr, ug_scr, uv_scr, *, tiles_per_seq):
    i = pl.program_id(0)
    j = pl.program_id(1)

    @pl.when(j == 0)
    def _():
        g = g_ref[...]
        hh = _rms(halo_ref[...], g)
        hh = jnp.where(i % tiles_per_seq == 0, 0.0, hh)
        top = jnp.concatenate([jnp.zeros_like(hh), hh], axis=0)
        h_scr[0:HALO, :] = top.astype(jnp.bfloat16)
        h_scr[HALO:, :] = _rms(x_ref[...], g).astype(jnp.bfloat16)
        o_ref[...] = jnp.zeros_like(o_ref)

    h = h_scr[...]
    ug_scr[...] = jnp.dot(h, wg_ref[...], preferred_element_type=jnp.float32)
    uv_scr[...] = jnp.dot(h, wv_ref[...], preferred_element_type=jnp.float32)

    def conv(u_scr, cw_ref, cb_ref):
        y = cb_ref[...]
        for kk in range(CONV_WIDTH):
            y = y + u_scr[pl.ds(HALO - (CONV_WIDTH - 1) + kk, FFN_TM), :] * cw_ref[kk:kk + 1, :]
        return y

    gate = conv(ug_scr, cwg_ref, cbg_ref)
    val = conv(uv_scr, cwv_ref, cbv_ref)
    a = (_gelu_tanh(gate) * val).astype(jnp.bfloat16)
    o_ref[...] += jnp.dot(a, wd_ref[...], preferred_element_type=jnp.float32)

    @pl.when(j == N_FT - 1)
    def _():
        o_ref[...] = x_ref[...] + _rms(o_ref[...], gp_ref[...])


def _ffn(x1, g, w_up, conv_w, conv_b, w_down, gp, seq_len):
    m = x1.shape[0]
    halo_blocks = FFN_TM // 8
    kern = functools.partial(_ffn_kernel, tiles_per_seq=seq_len // FFN_TM)
    return pl.pallas_call(
        kern,
        grid=(m // FFN_TM, N_FT),
        in_specs=[
            pl.BlockSpec((FFN_TM, D_MODEL), lambda i, j: (i, 0)),
            pl.BlockSpec((8, D_MODEL), lambda i, j: (jnp.maximum(i * halo_blocks - 1, 0), 0)),
            pl.BlockSpec((1, D_MODEL), lambda i, j: (0, 0)),
            pl.BlockSpec((D_MODEL, FFN_TF), lambda i, j: (0, j)),
            pl.BlockSpec((D_MODEL, FFN_TF), lambda i, j: (0, j + N_FT)),
            pl.BlockSpec((CONV_WIDTH, FFN_TF), lambda i, j: (0, j)),
            pl.BlockSpec((CONV_WIDTH, FFN_TF), lambda i, j: (0, j + N_FT)),
            pl.BlockSpec((1, FFN_TF), lambda i, j: (0, j)),
            pl.BlockSpec((1, FFN_TF), lambda i, j: (0, j + N_FT)),
            pl.BlockSpec((FFN_TF, D_MODEL), lambda i, j: (j, 0)),
            pl.BlockSpec((1, D_MODEL), lambda i, j: (0, 0)),
        ],
        out_specs=pl.BlockSpec((FFN_TM, D_MODEL), lambda i, j: (i, 0)),
        out_shape=jax.ShapeDtypeStruct((m, D_MODEL), jnp.float32),
        scratch_shapes=[
            pltpu.VMEM((FFN_TM + HALO, D_MODEL), jnp.bfloat16),
            pltpu.VMEM((FFN_TM + HALO, FFN_TF), jnp.float32),
            pltpu.VMEM((FFN_TM + HALO, FFN_TF), jnp.float32),
        ],
        compiler_params=_params("arbitrary", "arbitrary"),
        name="conv_geglu_ffn",
    )(x1, x1, g, w_up, w_up, conv_w, conv_w, conv_b, conv_b, w_down, gp)


def kernel(x, pre_mix_g, w_in, b_forget, sinks, grp_swa_g, grp_fox_g, w_out, post_mix_g,
           pre_ffn_g, w_up, conv_w, conv_b, w_down, post_ffn_g):
    b, s, d = x.shape
    depth = w_in.shape[0]
    bf16 = jnp.bfloat16
    xf = x.reshape(b * s, d)
    for l in range(depth):
        w_qkv = jnp.concatenate([w_in[l, :, D_SWA + 2 * D_KV_SWA:D_QKV], w_in[l, :, :D_SWA + 2 * D_KV_SWA]],
                                axis=1).astype(bf16)
        w_f = jnp.pad(w_in[l, :, D_QKV:], ((0, 0), (0, LANES - N_HEADS_FOX))).astype(bf16)
        b_f = jnp.pad(b_forget[l], (0, LANES - N_HEADS_FOX)).reshape(1, LANES)

        proj, logf = _in_proj(xf, pre_mix_g[l].reshape(1, d), w_qkv, w_f, b_f)
        proj3 = proj.reshape(b, s, D_QKV)
        aug = _cumsum(logf.reshape(b, s, LANES))

        o_a = _swa(proj3, sinks[l])
        o_b = _fox(proj3, aug)

        x1 = _out_proj(o_a.reshape(b * s, D_SWA), o_b.reshape(b * s, D_FOX), xf,
                       grp_swa_g[l].reshape(1, D_SWA), grp_fox_g[l].reshape(1, D_FOX),
                       w_out[l].astype(bf16), post_mix_g[l].reshape(1, d))
        xf = _ffn(x1, pre_ffn_g[l].reshape(1, d), w_up[l].astype(bf16), conv_w[l],
                  conv_b[l].reshape(1, 2 * D_FF), w_down[l].astype(bf16), post_ffn_g[l].reshape(1, d), s)
    return xf.reshape(b, s, d)
```

```python
import functools

import numpy as np
import jax
import jax.numpy as jnp
from jax import lax
from jax.experimental import pallas as pl
from jax.experimental.pallas import tpu as pltpu

D_MODEL = 2048
HEAD_DIM = 64
N_HEADS_SWA = 16
N_KV_SWA = 2
N_HEADS_FOX = 16
WINDOW = 128
BLOCK = 128
D_FF = 5632
CONV_WIDTH = 3
EPS = 1e-6
D_SWA = N_HEADS_SWA * HEAD_DIM
D_KV_SWA = N_KV_SWA * HEAD_DIM
D_FOX = N_HEADS_FOX * HEAD_DIM
D_MIX = D_SWA + D_FOX
D_QKV = D_SWA + 2 * D_KV_SWA + 3 * D_FOX

LANES = 128
COL_QB = 0
COL_KB = COL_QB + D_FOX // LANES
COL_VB = COL_KB + D_FOX // LANES
COL_QA = COL_VB + D_FOX // LANES
COL_KA = COL_QA + D_SWA // LANES
COL_VA = COL_KA + 1

NEG = -1e30
VMEM_LIMIT = 56 * 1024 * 1024

ALIBI_SLOPES = [float(v) for v in np.asarray(2.0 ** (-8.0 * np.arange(1, N_HEADS_SWA + 1) / N_HEADS_SWA),
                                            dtype=np.float32)]

_NT = (((1,), (1,)), ((), ()))


def _rms(xf, g):
    return xf * lax.rsqrt(jnp.mean(xf * xf, axis=-1, keepdims=True) + EPS) * g


def _params(*sem):
    return pltpu.CompilerParams(dimension_semantics=sem, vmem_limit_bytes=VMEM_LIMIT)


IN_TM = 512
IN_CHUNK = 1024
LOG2E = float(np.log2(np.e))
Q_MULT = LOG2E / float(np.sqrt(HEAD_DIM))


def _in_segments():
    bounds = ((COL_QB * LANES, COL_KB * LANES, Q_MULT), (COL_KB * LANES, COL_QA * LANES, 1.0),
              (COL_QA * LANES, COL_KA * LANES, Q_MULT), (COL_KA * LANES, D_QKV, 1.0))
    segs = []
    for lo, hi, mult in bounds:
        for n0 in range(lo, hi, IN_CHUNK):
            segs.append((n0, min(n0 + IN_CHUNK, hi), mult))
    return tuple(segs)


IN_SEGMENTS = _in_segments()


def _in_proj_kernel(x_ref, g_ref, w_ref, wf_ref, bf_ref, proj_ref, logf_ref):
    h = _rms(x_ref[...], g_ref[...]).astype(jnp.bfloat16)
    for n0, n1, mult in IN_SEGMENTS:
        acc = jnp.dot(h, w_ref[:, n0:n1], preferred_element_type=jnp.float32)
        if mult != 1.0:
            acc = acc * mult
        proj_ref[:, n0:n1] = acc.astype(jnp.bfloat16)
    f = jnp.dot(h, wf_ref[...], preferred_element_type=jnp.float32) + bf_ref[...]
    logf_ref[...] = jnp.minimum(f, 0.0) - jnp.log1p(jnp.exp(-jnp.abs(f)))


def _in_proj(x2d, g, w_qkv, w_f, b_f):
    m = x2d.shape[0]
    return pl.pallas_call(
        _in_proj_kernel,
        grid=(m // IN_TM,),
        in_specs=[
            pl.BlockSpec((IN_TM, D_MODEL), lambda i: (i, 0)),
            pl.BlockSpec((1, D_MODEL), lambda i: (0, 0)),
            pl.BlockSpec((D_MODEL, D_QKV), lambda i: (0, 0), pipeline_mode=pl.Buffered(1)),
            pl.BlockSpec((D_MODEL, LANES), lambda i: (0, 0), pipeline_mode=pl.Buffered(1)),
            pl.BlockSpec((1, LANES), lambda i: (0, 0)),
        ],
        out_specs=[
            pl.BlockSpec((IN_TM, D_QKV), lambda i: (i, 0)),
            pl.BlockSpec((IN_TM, LANES), lambda i: (i, 0)),
        ],
        out_shape=[
            jax.ShapeDtypeStruct((m, D_QKV), jnp.bfloat16),
            jax.ShapeDtypeStruct((m, LANES), jnp.float32),
        ],
        compiler_params=_params("arbitrary"),
        name="in_proj",
    )(x2d, g, w_qkv, w_f, b_f)


CS_BLK = 128
N_SPLIT = 3


def _bias_placement():
    place = np.zeros((N_SPLIT, LANES, D_FOX), np.float32)
    for h in range(N_HEADS_FOX):
        base = (h // 2) * LANES + (HEAD_DIM if h % 2 == 0 else 0)
        for i in range(N_SPLIT):
            place[i, h, base + i] = 1.0
    return place


def _cumsum_kernel(logf_ref, place_ref, a_ref, w_scr):
    s = logf_ref.shape[1]
    nblk = s // CS_BLK
    exact = dict(precision=lax.Precision.HIGHEST, preferred_element_type=jnp.float32)
    r = lax.broadcasted_iota(jnp.int32, (CS_BLK, CS_BLK), 0)
    c = lax.broadcasted_iota(jnp.int32, (CS_BLK, CS_BLK), 1)
    lower = (r >= c).astype(jnp.float32)
    blocks = [slice(b * CS_BLK, (b + 1) * CS_BLK) for b in range(nblk)]
    for rows in blocks:
        w_scr[rows, :] = jnp.dot(lower, logf_ref[0, rows, :], **exact)
    totals = w_scr[pl.ds(CS_BLK - 1, nblk, stride=CS_BLK), :]
    rb = lax.broadcasted_iota(jnp.int32, (nblk, nblk), 0)
    cb = lax.broadcasted_iota(jnp.int32, (nblk, nblk), 1)
    carry = jnp.dot((rb > cb).astype(jnp.float32), totals, **exact)
    for b, rows in enumerate(blocks):
        rest = (w_scr[rows, :] + carry[b:b + 1, :]) * (-LOG2E)
        parts = []
        for _ in range(N_SPLIT):
            part = rest.astype(jnp.bfloat16)
            rest = rest - part.astype(jnp.float32)
            parts.append(part)
        a_ref[0, rows, :] = jnp.dot(jnp.concatenate(parts, axis=1), place_ref[...],
                                    preferred_element_type=jnp.float32).astype(jnp.bfloat16)


def _cumsum(logf3):
    b, s, _ = logf3.shape
    place = jnp.asarray(_bias_placement().reshape(N_SPLIT * LANES, D_FOX), jnp.bfloat16)
    return pl.pallas_call(
        _cumsum_kernel,
        grid=(b,),
        in_specs=[pl.BlockSpec((1, s, LANES), lambda i: (i, 0, 0)),
                  pl.BlockSpec((N_SPLIT * LANES, D_FOX), lambda i: (0, 0))],
        out_specs=pl.BlockSpec((1, s, D_FOX), lambda i: (i, 0, 0)),
        out_shape=jax.ShapeDtypeStruct((b, s, D_FOX), jnp.bfloat16),
        scratch_shapes=[pltpu.VMEM((s, LANES), jnp.float32)],
        compiler_params=_params("arbitrary"),
        name="cumsum_logf",
    )(logf3, place)


SWA_TQ = 512
PAIRS_PER_KV = (N_HEADS_SWA // N_KV_SWA) // 2


SWA_R = SWA_TQ // BLOCK
SWA_COLS = PAIRS_PER_KV * BLOCK


def _swa_kernel(sink_ref, q_ref, k_ref, v_ref, o_ref, k_scr, vt_scr, bias_scr, st_scr):
    bi = pl.program_id(0)
    qi = pl.program_id(1)
    s_len = k_ref.shape[1]
    log2_block = int(np.log2(BLOCK))

    @pl.when((bi == 0) & (qi == 0))
    def _():
        key_loc = lax.broadcasted_iota(jnp.int32, (2 * BLOCK, SWA_COLS), 0)
        col = lax.broadcasted_iota(jnp.int32, (2 * BLOCK, SWA_COLS), 1)
        dist = BLOCK + (col & (BLOCK - 1)) - key_loc
        pair = lax.shift_right_logical(col, log2_block)
        distf = dist.astype(jnp.float32)
        in_win = (dist >= 0) & (dist < WINDOW)
        for kk in range(N_KV_SWA):
            for par in range(2):
                slope = jnp.zeros((2 * BLOCK, SWA_COLS), jnp.float32)
                for pi in range(PAIRS_PER_KV):
                    h = kk * 2 * PAIRS_PER_KV + 2 * pi + par
                    slope = jnp.where(pair == pi, ALIBI_SLOPES[h] * LOG2E, slope)
                bias_scr[kk, par] = jnp.where(in_win, -slope * distf, NEG)

    @pl.when(qi == 0)
    def _():
        low = lax.broadcasted_iota(jnp.int32, (SWA_TQ, LANES), 1) < HEAD_DIM
        halves = (low, jnp.logical_not(low))
        zero_k = jnp.zeros((BLOCK, LANES), jnp.bfloat16)
        zero_v = jnp.zeros((LANES, BLOCK), jnp.bfloat16)
        for kk in range(N_KV_SWA):
            for par in range(2):
                k_scr[kk, par, 0:BLOCK, :] = zero_k
                vt_scr[kk, par, :, 0:BLOCK] = zero_v
        for c0 in range(0, s_len, SWA_TQ):
            kc = k_ref[0, c0:c0 + SWA_TQ, :].astype(jnp.float32)
            vc = v_ref[0, c0:c0 + SWA_TQ, :].astype(jnp.float32)
            ksw = pltpu.roll(kc, HEAD_DIM, 1)
            vsw = pltpu.roll(vc, HEAD_DIM, 1)
            rows = slice(BLOCK + c0, BLOCK + c0 + SWA_TQ)
            for kk in range(N_KV_SWA):
                for par in range(2):
                    ksrc, vsrc = (kc, vc) if kk == par else (ksw, vsw)
                    k_scr[kk, par, rows, :] = jnp.where(halves[par], ksrc, 0.0).astype(jnp.bfloat16)
                    vt_scr[kk, par, :, rows] = jnp.where(halves[par], vsrc, 1.0).T.astype(jnp.bfloat16)

    pair1 = lax.shift_right_logical(lax.broadcasted_iota(jnp.int32, (1, SWA_COLS), 1), log2_block)
    pad_rows = lax.broadcasted_iota(jnp.int32, (2 * BLOCK, SWA_COLS), 0) < BLOCK
    first_tile = qi == 0
    sinks2 = {}
    for kk in range(N_KV_SWA):
        for par in range(2):
            sink = jnp.zeros((1, SWA_COLS), jnp.float32)
            for pi in range(PAIRS_PER_KV):
                sink = jnp.where(pair1 == pi, sink_ref[kk * 2 * PAIRS_PER_KV + 2 * pi + par] * LOG2E, sink)
            sinks2[kk, par] = sink

    def band_start(r):
        return pl.multiple_of(qi * SWA_TQ + r * BLOCK, BLOCK)

    def scores(r):
        rows = slice(r * BLOCK, (r + 1) * BLOCK)
        for kk in range(N_KV_SWA):
            qs = jnp.concatenate([q_ref[0, rows, (kk * PAIRS_PER_KV + pi) * LANES:(kk * PAIRS_PER_KV + pi + 1) * LANES]
                                  for pi in range(PAIRS_PER_KV)], axis=0)
            for par in range(2):
                st_scr[r, kk, par] = lax.dot_general(k_scr[kk, par, pl.ds(band_start(r), 2 * BLOCK), :], qs, _NT,
                                                     preferred_element_type=jnp.float32)

    def finish(r):
        rows = slice(r * BLOCK, (r + 1) * BLOCK)
        for kk in range(N_KV_SWA):
            outs = []
            for par in range(2):
                sink = sinks2[kk, par]
                st = st_scr[r, kk, par] + bias_scr[kk, par]
                if r == 0:
                    st = jnp.where(pad_rows & first_tile, NEG, st)
                m = jnp.maximum(jnp.max(st, axis=0, keepdims=True), sink)
                p = jnp.exp2(st - m).astype(jnp.bfloat16)
                o = jnp.dot(vt_scr[kk, par, :, pl.ds(band_start(r), 2 * BLOCK)], p,
                            preferred_element_type=jnp.float32)
                extra = jnp.exp2(sink - m)
                if par == 0:
                    outs.append(o[0:HEAD_DIM] / (o[HEAD_DIM:HEAD_DIM + 1] + extra))
                else:
                    outs.append(o[HEAD_DIM:LANES] / (o[0:1] + extra))
            ot = jnp.concatenate(outs, axis=0).T
            for pi in range(PAIRS_PER_KV):
                c0 = (kk * PAIRS_PER_KV + pi) * LANES
                o_ref[0, rows, c0:c0 + LANES] = ot[pi * BLOCK:(pi + 1) * BLOCK, :]

    scores(0)
    for r in range(SWA_R):
        if r + 1 < SWA_R:
            scores(r + 1)
        finish(r)


def _swa(proj3, sinks):
    b, s, _ = proj3.shape
    return pl.pallas_call(
        _swa_kernel,
        grid=(b, s // SWA_TQ),
        in_specs=[
            pl.BlockSpec(memory_space=pltpu.SMEM),
            pl.BlockSpec((1, SWA_TQ, D_SWA), lambda bi, qi: (bi, qi, COL_QA * LANES // D_SWA)),
            pl.BlockSpec((1, s, LANES), lambda bi, qi: (bi, 0, COL_KA)),
            pl.BlockSpec((1, s, LANES), lambda bi, qi: (bi, 0, COL_VA)),
        ],
        out_specs=pl.BlockSpec((1, SWA_TQ, D_SWA), lambda bi, qi: (bi, qi, 0)),
        scratch_shapes=[
            pltpu.VMEM((N_KV_SWA, 2, s + BLOCK, LANES), jnp.bfloat16),
            pltpu.VMEM((N_KV_SWA, 2, LANES, s + BLOCK), jnp.bfloat16),
            pltpu.VMEM((N_KV_SWA, 2, 2 * BLOCK, SWA_COLS), jnp.float32),
            pltpu.VMEM((SWA_R, N_KV_SWA, 2, 2 * BLOCK, SWA_COLS), jnp.float32),
        ],
        out_shape=jax.ShapeDtypeStruct((b, s, D_SWA), jnp.float32),
        compiler_params=_params("arbitrary", "arbitrary"),
        name="swa_attention",
    )(sinks, proj3, proj3, proj3)


FOX_TQ = 512
FOX_TK = FOX_TQ
FOX_PAIRS = 2
assert all(col % FOX_PAIRS == 0 for col in (COL_QB, COL_KB, COL_VB))


def _fox_kernel(q_ref, k_ref, v_ref, a_ref, o_ref, ka_scr, vat_scr, m_scr, acc_scr, st_scr):
    qi = pl.program_id(2)
    s_len = k_ref.shape[1]
    low_k = lax.broadcasted_iota(jnp.int32, (FOX_TK, LANES), 1) < HEAD_DIM
    keeps = (low_k, jnp.logical_not(low_k))

    heads = range(2 * FOX_PAIRS)

    @pl.when(qi == 0)
    def _():
        for c0 in range(0, s_len, FOX_TK):
            for pair in range(FOX_PAIRS):
                cols = slice(pair * LANES, (pair + 1) * LANES)
                kc = k_ref[0, c0:c0 + FOX_TK, cols]
                vc = v_ref[0, c0:c0 + FOX_TK, cols].astype(jnp.float32)
                ac = a_ref[0, c0:c0 + FOX_TK, cols]
                for par in range(2):
                    h = 2 * pair + par
                    ka_scr[h, c0:c0 + FOX_TK, :] = jnp.where(keeps[par], kc, ac)
                    vat_scr[h, :, c0:c0 + FOX_TK] = jnp.where(keeps[par], vc, 1.0).T.astype(jnp.bfloat16)

    lane_q = lax.broadcasted_iota(jnp.int32, (FOX_TQ, LANES), 1)
    ones_hi = jnp.where((lane_q >= HEAD_DIM) & (lane_q < HEAD_DIM + N_SPLIT), 1.0, 0.0).astype(jnp.bfloat16)
    ones_lo = jnp.where(lane_q < N_SPLIT, 1.0, 0.0).astype(jnp.bfloat16)
    qa = []
    for pair in range(FOX_PAIRS):
        q = q_ref[0, :, pair * LANES:(pair + 1) * LANES]
        qa += [jnp.where(lane_q < HEAD_DIM, q, ones_hi), jnp.where(lane_q < HEAD_DIM, ones_lo, q)]
    key_row = lax.broadcasted_iota(jnp.int32, (FOX_TK, FOX_TQ), 0)
    qry_col = lax.broadcasted_iota(jnp.int32, (FOX_TK, FOX_TQ), 1)

    m_scr[...] = jnp.full(m_scr.shape, NEG, jnp.float32)
    acc_scr[...] = jnp.zeros(acc_scr.shape, jnp.float32)

    def scores(j, slot):
        k0 = pl.multiple_of(j * FOX_TK, FOX_TK)
        for h in heads:
            st_scr[slot, h] = lax.dot_general(ka_scr[h, pl.ds(k0, FOX_TK), :], qa[h], _NT,
                                              preferred_element_type=jnp.float32)

    def update(j, slot, diagonal):
        k0 = pl.multiple_of(j * FOX_TK, FOX_TK)
        for h in heads:
            st = st_scr[slot, h]
            if diagonal:
                st = jnp.where(key_row <= qry_col, st, NEG)
            m_old = m_scr[h]
            m_new = jnp.maximum(m_old, jnp.max(st, axis=0, keepdims=True))
            alpha = jnp.exp2(m_old - m_new)
            pt = jnp.exp2(st - m_new).astype(jnp.bfloat16)
            acc_scr[h] = alpha * acc_scr[h] + jnp.dot(vat_scr[h, :, pl.ds(k0, FOX_TK)], pt,
                                                      preferred_element_type=jnp.float32)
            m_scr[h] = m_new

    scores(0, 0)

    def body(jj, _):
        j = 2 * jj
        scores(j + 1, 1)
        update(j, 0, False)
        scores(j + 2, 0)
        update(j + 1, 1, False)
        return 0

    lax.fori_loop(0, qi // 2, body, 0)

    @pl.when(qi % 2 == 1)
    def _():
        scores(qi, 1)
        update(qi - 1, 0, False)
        update(qi, 1, True)

    @pl.when(qi % 2 == 0)
    def _():
        update(qi, 0, True)

    for pair in range(FOX_PAIRS):
        acc0 = acc_scr[2 * pair]
        acc1 = acc_scr[2 * pair + 1]
        ot = jnp.concatenate([acc0[0:HEAD_DIM] / acc0[HEAD_DIM:HEAD_DIM + 1],
                              acc1[HEAD_DIM:LANES] / acc1[0:1]], axis=0)
        o_ref[0, :, pair * LANES:(pair + 1) * LANES] = ot.T


def _fox(proj3, aug):
    b, s, _ = proj3.shape
    width = FOX_PAIRS * LANES
    ngroups = D_FOX // width
    nheads = 2 * FOX_PAIRS
    once = pl.Buffered(1)
    return pl.pallas_call(
        _fox_kernel,
        grid=(b, ngroups, s // FOX_TQ),
        in_specs=[
            pl.BlockSpec((1, FOX_TQ, width), lambda bi, p, qi: (bi, qi, COL_QB // FOX_PAIRS + p)),
            pl.BlockSpec((1, s, width), lambda bi, p, qi: (bi, 0, COL_KB // FOX_PAIRS + p), pipeline_mode=once),
            pl.BlockSpec((1, s, width), lambda bi, p, qi: (bi, 0, COL_VB // FOX_PAIRS + p), pipeline_mode=once),
            pl.BlockSpec((1, s, width), lambda bi, p, qi: (bi, 0, p), pipeline_mode=once),
        ],
        out_specs=pl.BlockSpec((1, FOX_TQ, width), lambda bi, p, qi: (bi, qi, p)),
        out_shape=jax.ShapeDtypeStruct((b, s, D_FOX), jnp.float32),
        scratch_shapes=[
            pltpu.VMEM((nheads, s, LANES), jnp.bfloat16),
            pltpu.VMEM((nheads, LANES, s), jnp.bfloat16),
            pltpu.VMEM((nheads, 1, FOX_TQ), jnp.float32),
            pltpu.VMEM((nheads, LANES, FOX_TQ), jnp.float32),
            pltpu.VMEM((2, nheads, FOX_TK, FOX_TQ), jnp.float32),
        ],
        compiler_params=_params("arbitrary", "arbitrary", "arbitrary"),
        name="fox_attention",
    )(proj3, proj3, proj3, aug)


OUT_TM = 512


def _out_proj_kernel(oa_ref, ob_ref, x_ref, ga_ref, gb_ref, w_ref, gp_ref, x1_ref):
    na = _rms(oa_ref[...], ga_ref[...]).astype(jnp.bfloat16)
    nb = _rms(ob_ref[...], gb_ref[...]).astype(jnp.bfloat16)
    mix = (jnp.dot(na, w_ref[0:D_SWA, :], preferred_element_type=jnp.float32)
           + jnp.dot(nb, w_ref[D_SWA:D_MIX, :], preferred_element_type=jnp.float32))
    x1_ref[...] = x_ref[...] + _rms(mix, gp_ref[...])


def _out_proj(oa, ob, x2d, ga, gb, w_out, gp):
    m = x2d.shape[0]
    row = lambda i: (i, 0)
    fixed = lambda i: (0, 0)
    return pl.pallas_call(
        _out_proj_kernel,
        grid=(m // OUT_TM,),
        in_specs=[
            pl.BlockSpec((OUT_TM, D_SWA), row),
            pl.BlockSpec((OUT_TM, D_FOX), row),
            pl.BlockSpec((OUT_TM, D_MODEL), row),
            pl.BlockSpec((1, D_SWA), fixed),
            pl.BlockSpec((1, D_FOX), fixed),
            pl.BlockSpec((D_MIX, D_MODEL), fixed, pipeline_mode=pl.Buffered(1)),
            pl.BlockSpec((1, D_MODEL), fixed),
        ],
        out_specs=pl.BlockSpec((OUT_TM, D_MODEL), row),
        out_shape=jax.ShapeDtypeStruct((m, D_MODEL), jnp.float32),
        compiler_params=_params("arbitrary"),
        name="out_proj",
    )(oa, ob, x2d, ga, gb, w_out, gp)


FFN_TM = 512
FFN_TF = 512
HALO = 16
N_FT = D_FF // FFN_TF


def _gelu_tanh(x):
    return 0.5 * x * (1.0 + jnp.tanh(np.sqrt(2.0 / np.pi) * (x + 0.044715 * (x * x * x))))


def _ffn_kernel(x_ref, halo_ref, g_ref, wg_ref, wv_ref, cwg_ref, cwv_ref, cbg_ref, cbv_ref, wd_ref, gp_ref,
                o_ref, h_scr, ug_scr, uv_scr, *, tiles_per_seq):
    i = pl.program_id(0)
    j = pl.program_id(1)

    @pl.when(j == 0)
    def _():
        g = g_ref[...]
        hh = _rms(halo_ref[...], g)
        hh = jnp.where(i % tiles_per_seq == 0, 0.0, hh)
        top = jnp.concatenate([jnp.zeros_like(hh), hh], axis=0)
        h_scr[0:HALO, :] = top.astype(jnp.bfloat16)
        h_scr[HALO:, :] = _rms(x_ref[...], g).astype(jnp.bfloat16)
        o_ref[...] = jnp.zeros_like(o_ref)

    h = h_scr[...]
    ug_scr[...] = jnp.dot(h, wg_ref[...], preferred_element_type=jnp.float32)
    uv_scr[...] = jnp.dot(h, wv_ref[...], preferred_element_type=jnp.float32)

    def conv(u_scr, cw_ref, cb_ref):
        y = cb_ref[...]
        for kk in range(CONV_WIDTH):
            y = y + u_scr[pl.ds(HALO - (CONV_WIDTH - 1) + kk, FFN_TM), :] * cw_ref[kk:kk + 1, :]
        return y

    gate = conv(ug_scr, cwg_ref, cbg_ref)
    val = conv(uv_scr, cwv_ref, cbv_ref)
    a = (_gelu_tanh(gate) * val).astype(jnp.bfloat16)
    o_ref[...] += jnp.dot(a, wd_ref[...], preferred_element_type=jnp.float32)

    @pl.when(j == N_FT - 1)
    def _():
        o_ref[...] = x_ref[...] + _rms(o_ref[...], gp_ref[...])


def _ffn(x1, g, w_up, conv_w, conv_b, w_down, gp, seq_len):
    m = x1.shape[0]
    halo_blocks = FFN_TM // 8
    kern = functools.partial(_ffn_kernel, tiles_per_seq=seq_len // FFN_TM)
    return pl.pallas_call(
        kern,
        grid=(m // FFN_TM, N_FT),
        in_specs=[
            pl.BlockSpec((FFN_TM, D_MODEL), lambda i, j: (i, 0)),
            pl.BlockSpec((8, D_MODEL), lambda i, j: (jnp.maximum(i * halo_blocks - 1, 0), 0)),
            pl.BlockSpec((1, D_MODEL), lambda i, j: (0, 0)),
            pl.BlockSpec((D_MODEL, FFN_TF), lambda i, j: (0, j)),
            pl.BlockSpec((D_MODEL, FFN_TF), lambda i, j: (0, j + N_FT)),
            pl.BlockSpec((CONV_WIDTH, FFN_TF), lambda i, j: (0, j)),
            pl.BlockSpec((CONV_WIDTH, FFN_TF), lambda i, j: (0, j + N_FT)),
            pl.BlockSpec((1, FFN_TF), lambda i, j: (0, j)),
            pl.BlockSpec((1, FFN_TF), lambda i, j: (0, j + N_FT)),
            pl.BlockSpec((FFN_TF, D_MODEL), lambda i, j: (j, 0)),
            pl.BlockSpec((1, D_MODEL), lambda i, j: (0, 0)),
        ],
        out_specs=pl.BlockSpec((FFN_TM, D_MODEL), lambda i, j: (i, 0)),
        out_shape=jax.ShapeDtypeStruct((m, D_MODEL), jnp.float32),
        scratch_shapes=[
            pltpu.VMEM((FFN_TM + HALO, D_MODEL), jnp.bfloat16),
            pltpu.VMEM((FFN_TM + HALO, FFN_TF), jnp.float32),
            pltpu.VMEM((FFN_TM + HALO, FFN_TF), jnp.float32),
        ],
        compiler_params=_params("arbitrary", "arbitrary"),
        name="conv_geglu_ffn",
    )(x1, x1, g, w_up, w_up, conv_w, conv_w, conv_b, conv_b, w_down, gp)


def kernel(x, pre_mix_g, w_in, b_forget, sinks, grp_swa_g, grp_fox_g, w_out, post_mix_g,
           pre_ffn_g, w_up, conv_w, conv_b, w_down, post_ffn_g):
    b, s, d = x.shape
    depth = w_in.shape[0]
    bf16 = jnp.bfloat16
    xf = x.reshape(b * s, d)
    for l in range(depth):
        w_qkv = jnp.concatenate([w_in[l, :, D_SWA + 2 * D_KV_SWA:D_QKV], w_in[l, :, :D_SWA + 2 * D_KV_SWA]],
                                axis=1).astype(bf16)
        w_f = jnp.pad(w_in[l, :, D_QKV:], ((0, 0), (0, LANES - N_HEADS_FOX))).astype(bf16)
        b_f = jnp.pad(b_forget[l], (0, LANES - N_HEADS_FOX)).reshape(1, LANES)

        proj, logf = _in_proj(xf, pre_mix_g[l].reshape(1, d), w_qkv, w_f, b_f)
        proj3 = proj.reshape(b, s, D_QKV)
        aug = _cumsum(logf.reshape(b, s, LANES))

        o_a = _swa(proj3, sinks[l])
        o_b = _fox(proj3, aug)

        x1 = _out_proj(o_a.reshape(b * s, D_SWA), o_b.reshape(b * s, D_FOX), xf,
                       grp_swa_g[l].reshape(1, D_SWA), grp_fox_g[l].reshape(1, D_FOX),
                       w_out[l].astype(bf16), post_mix_g[l].reshape(1, d))
        xf = _ffn(x1, pre_ffn_g[l].reshape(1, d), w_up[l].astype(bf16), conv_w[l],
                  conv_b[l].reshape(1, 2 * D_FF), w_down[l].astype(bf16), post_ffn_g[l].reshape(1, d), s)
    return xf.reshape(b, s, d)
```

```python
import functools

import numpy as np
import jax
import jax.numpy as jnp
from jax import lax
from jax.experimental import pallas as pl
from jax.experimental.pallas import tpu as pltpu

D_MODEL = 2048
HEAD_DIM = 64
N_HEADS_SWA = 16
N_KV_SWA = 2
N_HEADS_FOX = 16
WINDOW = 128
BLOCK = 128
D_FF = 5632
CONV_WIDTH = 3
EPS = 1e-6
D_SWA = N_HEADS_SWA * HEAD_DIM
D_KV_SWA = N_KV_SWA * HEAD_DIM
D_FOX = N_HEADS_FOX * HEAD_DIM
D_MIX = D_SWA + D_FOX
D_QKV = D_SWA + 2 * D_KV_SWA + 3 * D_FOX

LANES = 128
COL_QA = 0
COL_KA = COL_QA + D_SWA // LANES
COL_VA = COL_KA + 1
COL_QB = COL_VA + 1
COL_KB = COL_QB + D_FOX // LANES
COL_VB = COL_KB + D_FOX // LANES
D_IN_PAD = D_QKV + LANES

NEG = -1e30
VMEM_LIMIT = 56 * 1024 * 1024

ALIBI_SLOPES = [float(v) for v in np.asarray(2.0 ** (-8.0 * np.arange(1, N_HEADS_SWA + 1) / N_HEADS_SWA),
                                            dtype=np.float32)]

_NT = (((1,), (1,)), ((), ()))


def _rms(xf, g):
    return xf * lax.rsqrt(jnp.mean(xf * xf, axis=-1, keepdims=True) + EPS) * g


def _params(*sem):
    return pltpu.CompilerParams(dimension_semantics=sem, vmem_limit_bytes=VMEM_LIMIT)


CAST_TN = 5 * LANES


def _cast_pad_kernel(w_ref, o_ref, *, valid_cols):
    col = pl.program_id(0) * CAST_TN + lax.broadcasted_iota(jnp.int32, w_ref.shape, 1)
    o_ref[...] = jnp.where(col < valid_cols, w_ref[...], 0.0).astype(jnp.bfloat16)


def _cast_pad_cols(w, padded_cols):
    rows, cols = w.shape
    assert padded_cols % CAST_TN == 0 and padded_cols - cols < CAST_TN
    return pl.pallas_call(
        functools.partial(_cast_pad_kernel, valid_cols=cols),
        grid=(padded_cols // CAST_TN,),
        in_specs=[pl.BlockSpec((rows, CAST_TN), lambda j: (0, j))],
        out_specs=pl.BlockSpec((rows, CAST_TN), lambda j: (0, j)),
        out_shape=jax.ShapeDtypeStruct((rows, padded_cols), jnp.bfloat16),
        compiler_params=_params("arbitrary"),
        name="cast_w_in",
    )(w)


IN_TM = 512
IN_CHUNK = 1024
LOG2E = float(np.log2(np.e))
Q_MULT = LOG2E / float(np.sqrt(HEAD_DIM))


Q_RANGES = ((COL_QA * LANES, COL_QA * LANES + D_SWA), (COL_QB * LANES, COL_QB * LANES + D_FOX))


def _in_proj_kernel(x_ref, g_ref, w_ref, wf_ref, bf_ref, proj_ref, logf_ref):
    h = _rms(x_ref[...], g_ref[...]).astype(jnp.bfloat16)
    for n0 in range(0, D_QKV, IN_CHUNK):
        n1 = min(n0 + IN_CHUNK, D_QKV)
        acc = jnp.dot(h, w_ref[:, n0:n1], preferred_element_type=jnp.float32)
        overlaps = [(max(lo, n0), min(hi, n1)) for lo, hi in Q_RANGES if max(lo, n0) < min(hi, n1)]
        if overlaps:
            col = n0 + lax.broadcasted_iota(jnp.int32, (1, n1 - n0), 1)
            mult = jnp.ones((1, n1 - n0), jnp.float32)
            for lo, hi in overlaps:
                mult = jnp.where((col >= lo) & (col < hi), Q_MULT, mult)
            acc = acc * mult
        proj_ref[:, n0:n1] = acc.astype(jnp.bfloat16)
    f = jnp.dot(h, wf_ref[...], preferred_element_type=jnp.float32) + bf_ref[...]
    logf_ref[...] = jnp.minimum(f, 0.0) - jnp.log1p(jnp.exp(-jnp.abs(f)))


def _in_proj(x2d, g, w_in_bf, b_f):
    m = x2d.shape[0]
    return pl.pallas_call(
        _in_proj_kernel,
        grid=(m // IN_TM,),
        in_specs=[
            pl.BlockSpec((IN_TM, D_MODEL), lambda i: (i, 0)),
            pl.BlockSpec((1, D_MODEL), lambda i: (0, 0)),
            pl.BlockSpec((D_MODEL, D_QKV), lambda i: (0, 0), pipeline_mode=pl.Buffered(1)),
            pl.BlockSpec((D_MODEL, LANES), lambda i: (0, D_QKV // LANES), pipeline_mode=pl.Buffered(1)),
            pl.BlockSpec((1, LANES), lambda i: (0, 0)),
        ],
        out_specs=[
            pl.BlockSpec((IN_TM, D_QKV), lambda i: (i, 0)),
            pl.BlockSpec((IN_TM, LANES), lambda i: (i, 0)),
        ],
        out_shape=[
            jax.ShapeDtypeStruct((m, D_QKV), jnp.bfloat16),
            jax.ShapeDtypeStruct((m, LANES), jnp.float32),
        ],
        compiler_params=_params("arbitrary"),
        name="in_proj",
    )(x2d, g, w_in_bf, w_in_bf, b_f)


CS_BLK = 128
N_SPLIT = 3


def _bias_placement():
    place = np.zeros((N_SPLIT, LANES, D_FOX), np.float32)
    for h in range(N_HEADS_FOX):
        base = (h // 2) * LANES + (HEAD_DIM if h % 2 == 0 else 0)
        for i in range(N_SPLIT):
            place[i, h, base + i] = 1.0
    return place


def _cumsum_kernel(logf_ref, place_ref, a_ref, w_scr):
    s = logf_ref.shape[1]
    nblk = s // CS_BLK
    exact = dict(precision=lax.Precision.HIGHEST, preferred_element_type=jnp.float32)
    r = lax.broadcasted_iota(jnp.int32, (CS_BLK, CS_BLK), 0)
    c = lax.broadcasted_iota(jnp.int32, (CS_BLK, CS_BLK), 1)
    lower = (r >= c).astype(jnp.float32)
    blocks = [slice(b * CS_BLK, (b + 1) * CS_BLK) for b in range(nblk)]
    for rows in blocks:
        w_scr[rows, :] = jnp.dot(lower, logf_ref[0, rows, :], **exact)
    totals = w_scr[pl.ds(CS_BLK - 1, nblk, stride=CS_BLK), :]
    rb = lax.broadcasted_iota(jnp.int32, (nblk, nblk), 0)
    cb = lax.broadcasted_iota(jnp.int32, (nblk, nblk), 1)
    carry = jnp.dot((rb > cb).astype(jnp.float32), totals, **exact)
    for b, rows in enumerate(blocks):
        rest = (w_scr[rows, :] + carry[b:b + 1, :]) * (-LOG2E)
        parts = []
        for _ in range(N_SPLIT):
            part = rest.astype(jnp.bfloat16)
            rest = rest - part.astype(jnp.float32)
            parts.append(part)
        a_ref[0, rows, :] = jnp.dot(jnp.concatenate(parts, axis=1), place_ref[...],
                                    preferred_element_type=jnp.float32).astype(jnp.bfloat16)


def _cumsum(logf3):
    b, s, _ = logf3.shape
    place = jnp.asarray(_bias_placement().reshape(N_SPLIT * LANES, D_FOX), jnp.bfloat16)
    return pl.pallas_call(
        _cumsum_kernel,
        grid=(b,),
        in_specs=[pl.BlockSpec((1, s, LANES), lambda i: (i, 0, 0)),
                  pl.BlockSpec((N_SPLIT * LANES, D_FOX), lambda i: (0, 0))],
        out_specs=pl.BlockSpec((1, s, D_FOX), lambda i: (i, 0, 0)),
        out_shape=jax.ShapeDtypeStruct((b, s, D_FOX), jnp.bfloat16),
        scratch_shapes=[pltpu.VMEM((s, LANES), jnp.float32)],
        compiler_params=_params("arbitrary"),
        name="cumsum_logf",
    )(logf3, place)


SWA_TQ = 512
PAIRS_PER_KV = (N_HEADS_SWA // N_KV_SWA) // 2


SWA_R = SWA_TQ // BLOCK
SWA_COLS = PAIRS_PER_KV * BLOCK


def _swa_kernel(sink_ref, q_ref, k_ref, v_ref, o_ref, k_scr, vt_scr, bias_scr, st_scr):
    bi = pl.program_id(0)
    qi = pl.program_id(1)
    s_len = k_ref.shape[1]
    log2_block = int(np.log2(BLOCK))

    @pl.when((bi == 0) & (qi == 0))
    def _():
        key_loc = lax.broadcasted_iota(jnp.int32, (2 * BLOCK, SWA_COLS), 0)
        col = lax.broadcasted_iota(jnp.int32, (2 * BLOCK, SWA_COLS), 1)
        dist = BLOCK + (col & (BLOCK - 1)) - key_loc
        pair = lax.shift_right_logical(col, log2_block)
        distf = dist.astype(jnp.float32)
        in_win = (dist >= 0) & (dist < WINDOW)
        for kk in range(N_KV_SWA):
            for par in range(2):
                slope = jnp.zeros((2 * BLOCK, SWA_COLS), jnp.float32)
                for pi in range(PAIRS_PER_KV):
                    h = kk * 2 * PAIRS_PER_KV + 2 * pi + par
                    slope = jnp.where(pair == pi, ALIBI_SLOPES[h] * LOG2E, slope)
                bias_scr[kk, par] = jnp.where(in_win, -slope * distf, NEG)

    @pl.when(qi == 0)
    def _():
        low = lax.broadcasted_iota(jnp.int32, (SWA_TQ, LANES), 1) < HEAD_DIM
        halves = (low, jnp.logical_not(low))
        zero_k = jnp.zeros((BLOCK, LANES), jnp.bfloat16)
        zero_v = jnp.zeros((LANES, BLOCK), jnp.bfloat16)
        for kk in range(N_KV_SWA):
            for par in range(2):
                k_scr[kk, par, 0:BLOCK, :] = zero_k
                vt_scr[kk, par, :, 0:BLOCK] = zero_v
        for c0 in range(0, s_len, SWA_TQ):
            kc = k_ref[0, c0:c0 + SWA_TQ, :].astype(jnp.float32)
            vc = v_ref[0, c0:c0 + SWA_TQ, :].astype(jnp.float32)
            ksw = pltpu.roll(kc, HEAD_DIM, 1)
            vsw = pltpu.roll(vc, HEAD_DIM, 1)
            rows = slice(BLOCK + c0, BLOCK + c0 + SWA_TQ)
            for kk in range(N_KV_SWA):
                for par in range(2):
                    ksrc, vsrc = (kc, vc) if kk == par else (ksw, vsw)
                    k_scr[kk, par, rows, :] = jnp.where(halves[par], ksrc, 0.0).astype(jnp.bfloat16)
                    vt_scr[kk, par, :, rows] = jnp.where(halves[par], vsrc, 1.0).T.astype(jnp.bfloat16)

    pair1 = lax.shift_right_logical(lax.broadcasted_iota(jnp.int32, (1, SWA_COLS), 1), log2_block)
    pad_rows = lax.broadcasted_iota(jnp.int32, (2 * BLOCK, SWA_COLS), 0) < BLOCK
    first_tile = qi == 0
    sinks2 = {}
    for kk in range(N_KV_SWA):
        for par in range(2):
            sink = jnp.zeros((1, SWA_COLS), jnp.float32)
            for pi in range(PAIRS_PER_KV):
                sink = jnp.where(pair1 == pi, sink_ref[kk * 2 * PAIRS_PER_KV + 2 * pi + par] * LOG2E, sink)
            sinks2[kk, par] = sink

    def band_start(r):
        return pl.multiple_of(qi * SWA_TQ + r * BLOCK, BLOCK)

    def scores(r):
        rows = slice(r * BLOCK, (r + 1) * BLOCK)
        for kk in range(N_KV_SWA):
            qs = jnp.concatenate([q_ref[0, rows, (kk * PAIRS_PER_KV + pi) * LANES:(kk * PAIRS_PER_KV + pi + 1) * LANES]
                                  for pi in range(PAIRS_PER_KV)], axis=0)
            for par in range(2):
                st_scr[r, kk, par] = lax.dot_general(k_scr[kk, par, pl.ds(band_start(r), 2 * BLOCK), :], qs, _NT,
                                                     preferred_element_type=jnp.float32)

    def finish(r):
        rows = slice(r * BLOCK, (r + 1) * BLOCK)
        for kk in range(N_KV_SWA):
            outs = []
            for par in range(2):
                sink = sinks2[kk, par]
                st = st_scr[r, kk, par] + bias_scr[kk, par]
                if r == 0:
                    st = jnp.where(pad_rows & first_tile, NEG, st)
                m = jnp.maximum(jnp.max(st, axis=0, keepdims=True), sink)
                p = jnp.exp2(st - m).astype(jnp.bfloat16)
                o = jnp.dot(vt_scr[kk, par, :, pl.ds(band_start(r), 2 * BLOCK)], p,
                            preferred_element_type=jnp.float32)
                extra = jnp.exp2(sink - m)
                if par == 0:
                    outs.append(o[0:HEAD_DIM] / (o[HEAD_DIM:HEAD_DIM + 1] + extra))
                else:
                    outs.append(o[HEAD_DIM:LANES] / (o[0:1] + extra))
            ot = jnp.concatenate(outs, axis=0).T
            for pi in range(PAIRS_PER_KV):
                c0 = (kk * PAIRS_PER_KV + pi) * LANES
                o_ref[0, rows, c0:c0 + LANES] = ot[pi * BLOCK:(pi + 1) * BLOCK, :]

    scores(0)
    for r in range(SWA_R):
        if r + 1 < SWA_R:
            scores(r + 1)
        finish(r)


def _swa(proj3, sinks):
    b, s, _ = proj3.shape
    return pl.pallas_call(
        _swa_kernel,
        grid=(b, s // SWA_TQ),
        in_specs=[
            pl.BlockSpec(memory_space=pltpu.SMEM),
            pl.BlockSpec((1, SWA_TQ, D_SWA), lambda bi, qi: (bi, qi, COL_QA * LANES // D_SWA)),
            pl.BlockSpec((1, s, LANES), lambda bi, qi: (bi, 0, COL_KA)),
            pl.BlockSpec((1, s, LANES), lambda bi, qi: (bi, 0, COL_VA)),
        ],
        out_specs=pl.BlockSpec((1, SWA_TQ, D_SWA), lambda bi, qi: (bi, qi, 0)),
        scratch_shapes=[
            pltpu.VMEM((N_KV_SWA, 2, s + BLOCK, LANES), jnp.bfloat16),
            pltpu.VMEM((N_KV_SWA, 2, LANES, s + BLOCK), jnp.bfloat16),
            pltpu.VMEM((N_KV_SWA, 2, 2 * BLOCK, SWA_COLS), jnp.float32),
            pltpu.VMEM((SWA_R, N_KV_SWA, 2, 2 * BLOCK, SWA_COLS), jnp.float32),
        ],
        out_shape=jax.ShapeDtypeStruct((b, s, D_SWA), jnp.float32),
        compiler_params=_params("arbitrary", "arbitrary"),
        name="swa_attention",
    )(sinks, proj3, proj3, proj3)


FOX_TQ = 512
FOX_TK = FOX_TQ
FOX_PAIRS = 2
assert all(col % FOX_PAIRS == 0 for col in (COL_QB, COL_KB, COL_VB))


def _fox_kernel(q_ref, k_ref, v_ref, a_ref, o_ref, ka_scr, vat_scr, m_scr, acc_scr, st_scr, cmax_scr):
    qi = pl.program_id(2)
    s_len = k_ref.shape[1]
    low_k = lax.broadcasted_iota(jnp.int32, (FOX_TK, LANES), 1) < HEAD_DIM
    keeps = (low_k, jnp.logical_not(low_k))

    heads = range(2 * FOX_PAIRS)

    @pl.when(qi == 0)
    def _():
        for c0 in range(0, s_len, FOX_TK):
            for pair in range(FOX_PAIRS):
                cols = slice(pair * LANES, (pair + 1) * LANES)
                kc = k_ref[0, c0:c0 + FOX_TK, cols]
                vc = v_ref[0, c0:c0 + FOX_TK, cols].astype(jnp.float32)
                ac = a_ref[0, c0:c0 + FOX_TK, cols]
                for par in range(2):
                    h = 2 * pair + par
                    ka_scr[h, c0:c0 + FOX_TK, :] = jnp.where(keeps[par], kc, ac)
                    vat_scr[h, :, c0:c0 + FOX_TK] = jnp.where(keeps[par], vc, 1.0).T.astype(jnp.bfloat16)

    lane_q = lax.broadcasted_iota(jnp.int32, (FOX_TQ, LANES), 1)
    ones_hi = jnp.where((lane_q >= HEAD_DIM) & (lane_q < HEAD_DIM + N_SPLIT), 1.0, 0.0).astype(jnp.bfloat16)
    ones_lo = jnp.where(lane_q < N_SPLIT, 1.0, 0.0).astype(jnp.bfloat16)
    qa = []
    for pair in range(FOX_PAIRS):
        q = q_ref[0, :, pair * LANES:(pair + 1) * LANES]
        qa += [jnp.where(lane_q < HEAD_DIM, q, ones_hi), jnp.where(lane_q < HEAD_DIM, ones_lo, q)]
    key_row = lax.broadcasted_iota(jnp.int32, (FOX_TK, FOX_TQ), 0)
    qry_col = lax.broadcasted_iota(jnp.int32, (FOX_TK, FOX_TQ), 1)

    m_scr[...] = jnp.full(m_scr.shape, NEG, jnp.float32)
    acc_scr[...] = jnp.zeros(acc_scr.shape, jnp.float32)

    def scores(j, slot):
        k0 = pl.multiple_of(j * FOX_TK, FOX_TK)
        for h in heads:
            st = lax.dot_general(ka_scr[h, pl.ds(k0, FOX_TK), :], qa[h], _NT,
                                 preferred_element_type=jnp.float32)
            st_scr[slot, h] = st
            cmax_scr[slot, h] = jnp.max(st, axis=0, keepdims=True)

    def update(j, slot, diagonal):
        k0 = pl.multiple_of(j * FOX_TK, FOX_TK)
        for h in heads:
            st = st_scr[slot, h]
            if diagonal:
                st = jnp.where(key_row <= qry_col, st, NEG)
                cmax = jnp.max(st, axis=0, keepdims=True)
            else:
                cmax = cmax_scr[slot, h]
            m_old = m_scr[h]
            m_new = jnp.maximum(m_old, cmax)
            alpha = jnp.exp2(m_old - m_new)
            pt = jnp.exp2(st - m_new).astype(jnp.bfloat16)
            acc_scr[h] = alpha * acc_scr[h] + jnp.dot(vat_scr[h, :, pl.ds(k0, FOX_TK)], pt,
                                                      preferred_element_type=jnp.float32)
            m_scr[h] = m_new

    scores(0, 0)

    def body(jj, _):
        j = 2 * jj
        scores(j + 1, 1)
        update(j, 0, False)
        scores(j + 2, 0)
        update(j + 1, 1, False)
        return 0

    lax.fori_loop(0, qi // 2, body, 0)

    @pl.when(qi % 2 == 1)
    def _():
        scores(qi, 1)
        update(qi - 1, 0, False)
        update(qi, 1, True)

    @pl.when(qi % 2 == 0)
    def _():
        update(qi, 0, True)

    for pair in range(FOX_PAIRS):
        acc0 = acc_scr[2 * pair]
        acc1 = acc_scr[2 * pair + 1]
        ot = jnp.concatenate([acc0[0:HEAD_DIM] / acc0[HEAD_DIM:HEAD_DIM + 1],
                              acc1[HEAD_DIM:LANES] / acc1[0:1]], axis=0)
        o_ref[0, :, pair * LANES:(pair + 1) * LANES] = ot.T


def _fox(proj3, aug):
    b, s, _ = proj3.shape
    width = FOX_PAIRS * LANES
    ngroups = D_FOX // width
    nheads = 2 * FOX_PAIRS
    return pl.pallas_call(
        _fox_kernel,
        grid=(b, ngroups, s // FOX_TQ),
        in_specs=[
            pl.BlockSpec((1, FOX_TQ, width), lambda bi, p, qi: (bi, qi, COL_QB // FOX_PAIRS + p)),
            pl.BlockSpec((1, s, width), lambda bi, p, qi: (bi, 0, COL_KB // FOX_PAIRS + p)),
            pl.BlockSpec((1, s, width), lambda bi, p, qi: (bi, 0, COL_VB // FOX_PAIRS + p)),
            pl.BlockSpec((1, s, width), lambda bi, p, qi: (bi, 0, p)),
        ],
        out_specs=pl.BlockSpec((1, FOX_TQ, width), lambda bi, p, qi: (bi, qi, p)),
        out_shape=jax.ShapeDtypeStruct((b, s, D_FOX), jnp.float32),
        scratch_shapes=[
            pltpu.VMEM((nheads, s, LANES), jnp.bfloat16),
            pltpu.VMEM((nheads, LANES, s), jnp.bfloat16),
            pltpu.VMEM((nheads, 1, FOX_TQ), jnp.float32),
            pltpu.VMEM((nheads, LANES, FOX_TQ), jnp.float32),
            pltpu.VMEM((2, nheads, FOX_TK, FOX_TQ), jnp.float32),
            pltpu.VMEM((2, nheads, 1, FOX_TQ), jnp.float32),
        ],
        compiler_params=_params("arbitrary", "arbitrary", "arbitrary"),
        name="fox_attention",
    )(proj3, proj3, proj3, aug)


OUT_TM = 512


def _out_proj_kernel(oa_ref, ob_ref, x_ref, ga_ref, gb_ref, w_ref, gp_ref, x1_ref):
    na = _rms(oa_ref[...], ga_ref[...]).astype(jnp.bfloat16)
    nb = _rms(ob_ref[...], gb_ref[...]).astype(jnp.bfloat16)
    mix = (jnp.dot(na, w_ref[0:D_SWA, :], preferred_element_type=jnp.float32)
           + jnp.dot(nb, w_ref[D_SWA:D_MIX, :], preferred_element_type=jnp.float32))
    x1_ref[...] = x_ref[...] + _rms(mix, gp_ref[...])


def _out_proj(oa, ob, x2d, ga, gb, w_out, gp):
    m = x2d.shape[0]
    row = lambda i: (i, 0)
    fixed = lambda i: (0, 0)
    return pl.pallas_call(
        _out_proj_kernel,
        grid=(m // OUT_TM,),
        in_specs=[
            pl.BlockSpec((OUT_TM, D_SWA), row),
            pl.BlockSpec((OUT_TM, D_FOX), row),
            pl.BlockSpec((OUT_TM, D_MODEL), row),
            pl.BlockSpec((1, D_SWA), fixed),
            pl.BlockSpec((1, D_FOX), fixed),
            pl.BlockSpec((D_MIX, D_MODEL), fixed, pipeline_mode=pl.Buffered(1)),
            pl.BlockSpec((1, D_MODEL), fixed),
        ],
        out_specs=pl.BlockSpec((OUT_TM, D_MODEL), row),
        out_shape=jax.ShapeDtypeStruct((m, D_MODEL), jnp.float32),
        compiler_params=_params("arbitrary"),
        name="out_proj",
    )(oa, ob, x2d, ga, gb, w_out, gp)


FFN_TM = 512
FFN_TF = 512
HALO = 16
N_FT = D_FF // FFN_TF


def _gelu_tanh(x):
    return 0.5 * x * (1.0 + jnp.tanh(np.sqrt(2.0 / np.pi) * (x + 0.044715 * (x * x * x))))


def _ffn_kernel(x_ref, halo_ref, g_ref, wg_ref, wv_ref, cwg_ref, cwv_ref, cbg_ref, cbv_ref, wd_ref, gp_ref,
                o_ref, h_scr, ug_scr, uv_scr, *, tiles_per_seq):
    i = pl.program_id(0)
    j = pl.program_id(1)

    @pl.when(j == 0)
    def _():
        g = g_ref[...]
        hh = _rms(halo_ref[...], g)
        hh = jnp.where(i % tiles_per_seq == 0, 0.0, hh)
        top = jnp.concatenate([jnp.zeros_like(hh), hh], axis=0)
        h_scr[0:HALO, :] = top.astype(jnp.bfloat16)
        h_scr[HALO:, :] = _rms(x_ref[...], g).astype(jnp.bfloat16)
        o_ref[...] = jnp.zeros_like(o_ref)

    h = h_scr[...]
    ug_scr[...] = jnp.dot(h, wg_ref[...], preferred_element_type=jnp.float32)
    uv_scr[...] = jnp.dot(h, wv_ref[...], preferred_element_type=jnp.float32)

    def conv(u_scr, cw_ref, cb_ref):
        y = cb_ref[...]
        for kk in range(CONV_WIDTH):
            y = y + u_scr[pl.ds(HALO - (CONV_WIDTH - 1) + kk, FFN_TM), :] * cw_ref[kk:kk + 1, :]
        return y

    gate = conv(ug_scr, cwg_ref, cbg_ref)
    val = conv(uv_scr, cwv_ref, cbv_ref)
    a = (_gelu_tanh(gate) * val).astype(jnp.bfloat16)
    o_ref[...] += jnp.dot(a, wd_ref[...], preferred_element_type=jnp.float32)

    @pl.when(j == N_FT - 1)
    def _():
        o_ref[...] = x_ref[...] + _rms(o_ref[...], gp_ref[...])


def _ffn(x1, g, w_up, conv_w, conv_b, w_down, gp, seq_len):
    m = x1.shape[0]
    halo_blocks = FFN_TM // 8
    kern = functools.partial(_ffn_kernel, tiles_per_seq=seq_len // FFN_TM)
    return pl.pallas_call(
        kern,
        grid=(m // FFN_TM, N_FT),
        in_specs=[
            pl.BlockSpec((FFN_TM, D_MODEL), lambda i, j: (i, 0)),
            pl.BlockSpec((8, D_MODEL), lambda i, j: (jnp.maximum(i * halo_blocks - 1, 0), 0)),
            pl.BlockSpec((1, D_MODEL), lambda i, j: (0, 0)),
            pl.BlockSpec((D_MODEL, FFN_TF), lambda i, j: (0, j)),
            pl.BlockSpec((D_MODEL, FFN_TF), lambda i, j: (0, j + N_FT)),
            pl.BlockSpec((CONV_WIDTH, FFN_TF), lambda i, j: (0, j)),
            pl.BlockSpec((CONV_WIDTH, FFN_TF), lambda i, j: (0, j + N_FT)),
            pl.BlockSpec((1, FFN_TF), lambda i, j: (0, j)),
            pl.BlockSpec((1, FFN_TF), lambda i, j: (0, j + N_FT)),
            pl.BlockSpec((FFN_TF, D_MODEL), lambda i, j: (j, 0)),
            pl.BlockSpec((1, D_MODEL), lambda i, j: (0, 0)),
        ],
        out_specs=pl.BlockSpec((FFN_TM, D_MODEL), lambda i, j: (i, 0)),
        out_shape=jax.ShapeDtypeStruct((m, D_MODEL), jnp.float32),
        scratch_shapes=[
            pltpu.VMEM((FFN_TM + HALO, D_MODEL), jnp.bfloat16),
            pltpu.VMEM((FFN_TM + HALO, FFN_TF), jnp.float32),
            pltpu.VMEM((FFN_TM + HALO, FFN_TF), jnp.float32),
        ],
        compiler_params=_params("arbitrary", "arbitrary"),
        name="conv_geglu_ffn",
    )(x1, x1, g, w_up, w_up, conv_w, conv_w, conv_b, conv_b, w_down, gp)


def kernel(x, pre_mix_g, w_in, b_forget, sinks, grp_swa_g, grp_fox_g, w_out, post_mix_g,
           pre_ffn_g, w_up, conv_w, conv_b, w_down, post_ffn_g):
    b, s, d = x.shape
    depth = w_in.shape[0]
    bf16 = jnp.bfloat16
    xf = x.reshape(b * s, d)
    for l in range(depth):
        w_in_bf = _cast_pad_cols(w_in[l], D_IN_PAD)
        b_f = jnp.pad(b_forget[l], (0, LANES - N_HEADS_FOX)).reshape(1, LANES)

        proj, logf = _in_proj(xf, pre_mix_g[l].reshape(1, d), w_in_bf, b_f)
        proj3 = proj.reshape(b, s, D_QKV)
        aug = _cumsum(logf.reshape(b, s, LANES))

        o_a = _swa(proj3, sinks[l])
        o_b = _fox(proj3, aug)

        x1 = _out_proj(o_a.reshape(b * s, D_SWA), o_b.reshape(b * s, D_FOX), xf,
                       grp_swa_g[l].reshape(1, D_SWA), grp_fox_g[l].reshape(1, D_FOX),
                       w_out[l].astype(bf16), post_mix_g[l].reshape(1, d))
        xf = _ffn(x1, pre_ffn_g[l].reshape(1, d), w_up[l].astype(bf16), conv_w[l],
                  conv_b[l].reshape(1, 2 * D_FF), w_down[l].astype(bf16), post_ffn_g[l].reshape(1, d), s)
    return xf.reshape(b, s, d)
```

```python
import functools

import numpy as np
import jax
import jax.numpy as jnp
from jax import lax
from jax.experimental import pallas as pl
from jax.experimental.pallas import tpu as pltpu

D_MODEL = 2048
HEAD_DIM = 64
N_HEADS_SWA = 16
N_KV_SWA = 2
N_HEADS_FOX = 16
WINDOW = 128
BLOCK = 128
D_FF = 5632
CONV_WIDTH = 3
EPS = 1e-6
D_SWA = N_HEADS_SWA * HEAD_DIM
D_KV_SWA = N_KV_SWA * HEAD_DIM
D_FOX = N_HEADS_FOX * HEAD_DIM
D_MIX = D_SWA + D_FOX
D_QKV = D_SWA + 2 * D_KV_SWA + 3 * D_FOX

LANES = 128
COL_QA = 0
COL_KA = COL_QA + D_SWA // LANES
COL_VA = COL_KA + 1
COL_QB = COL_VA + 1
COL_KB = COL_QB + D_FOX // LANES
COL_VB = COL_KB + D_FOX // LANES
D_IN_PAD = D_QKV + LANES

NEG = -1e30
VMEM_LIMIT = 56 * 1024 * 1024

ALIBI_SLOPES = [float(v) for v in np.asarray(2.0 ** (-8.0 * np.arange(1, N_HEADS_SWA + 1) / N_HEADS_SWA),
                                            dtype=np.float32)]

_NT = (((1,), (1,)), ((), ()))


def _rms(xf, g):
    return xf * lax.rsqrt(jnp.mean(xf * xf, axis=-1, keepdims=True) + EPS) * g


def _params(*sem):
    return pltpu.CompilerParams(dimension_semantics=sem, vmem_limit_bytes=VMEM_LIMIT)


CAST_TM = 5 * LANES


def _cast_pad_kernel(w_ref, o_ref, *, valid_rows):
    row = pl.program_id(0) * CAST_TM + lax.broadcasted_iota(jnp.int32, w_ref.shape, 0)
    o_ref[...] = jnp.where(row < valid_rows, w_ref[...], 0.0).astype(jnp.bfloat16)


def _cast_pad_rows(w, padded_rows):
    rows, cols = w.shape
    assert padded_rows % CAST_TM == 0 and padded_rows - rows < CAST_TM
    return pl.pallas_call(
        functools.partial(_cast_pad_kernel, valid_rows=rows),
        grid=(padded_rows // CAST_TM,),
        in_specs=[pl.BlockSpec((CAST_TM, cols), lambda j: (j, 0))],
        out_specs=pl.BlockSpec((CAST_TM, cols), lambda j: (j, 0)),
        out_shape=jax.ShapeDtypeStruct((padded_rows, cols), jnp.bfloat16),
        compiler_params=_params("arbitrary"),
        name="cast_w_in",
    )(w)


IN_TM = 512
IN_CHUNK = 1024
LOG2E = float(np.log2(np.e))
Q_MULT = LOG2E / float(np.sqrt(HEAD_DIM))


Q_RANGES = ((COL_QA * LANES, COL_QA * LANES + D_SWA), (COL_QB * LANES, COL_QB * LANES + D_FOX))


def _in_proj_kernel(x_ref, g_ref, w_ref, wf_ref, bf_ref, proj_ref, logf_ref):
    h = _rms(x_ref[...], g_ref[...]).astype(jnp.bfloat16)
    for n0 in range(0, D_QKV, IN_CHUNK):
        n1 = min(n0 + IN_CHUNK, D_QKV)
        acc = lax.dot_general(h, w_ref[n0:n1, :], _NT, preferred_element_type=jnp.float32)
        overlaps = [(max(lo, n0), min(hi, n1)) for lo, hi in Q_RANGES if max(lo, n0) < min(hi, n1)]
        if overlaps:
            col = n0 + lax.broadcasted_iota(jnp.int32, (1, n1 - n0), 1)
            mult = jnp.ones((1, n1 - n0), jnp.float32)
            for lo, hi in overlaps:
                mult = jnp.where((col >= lo) & (col < hi), Q_MULT, mult)
            acc = acc * mult
        proj_ref[:, n0:n1] = acc.astype(jnp.bfloat16)
    f = lax.dot_general(h, wf_ref[...], _NT, preferred_element_type=jnp.float32) + bf_ref[...]
    logf_ref[...] = jnp.minimum(f, 0.0) - jnp.log1p(jnp.exp(-jnp.abs(f)))


def _in_proj(x2d, g, w_in_bf, b_f):
    m = x2d.shape[0]
    return pl.pallas_call(
        _in_proj_kernel,
        grid=(m // IN_TM,),
        in_specs=[
            pl.BlockSpec((IN_TM, D_MODEL), lambda i: (i, 0)),
            pl.BlockSpec((1, D_MODEL), lambda i: (0, 0)),
            pl.BlockSpec((D_QKV, D_MODEL), lambda i: (0, 0), pipeline_mode=pl.Buffered(1)),
            pl.BlockSpec((LANES, D_MODEL), lambda i: (D_QKV // LANES, 0), pipeline_mode=pl.Buffered(1)),
            pl.BlockSpec((1, LANES), lambda i: (0, 0)),
        ],
        out_specs=[
            pl.BlockSpec((IN_TM, D_QKV), lambda i: (i, 0)),
            pl.BlockSpec((IN_TM, LANES), lambda i: (i, 0)),
        ],
        out_shape=[
            jax.ShapeDtypeStruct((m, D_QKV), jnp.bfloat16),
            jax.ShapeDtypeStruct((m, LANES), jnp.float32),
        ],
        compiler_params=_params("arbitrary"),
        name="in_proj",
    )(x2d, g, w_in_bf, w_in_bf, b_f)


CS_BLK = 128
N_SPLIT = 3


def _bias_placement():
    place = np.zeros((N_SPLIT, LANES, D_FOX), np.float32)
    for h in range(N_HEADS_FOX):
        base = (h // 2) * LANES + (HEAD_DIM if h % 2 == 0 else 0)
        for i in range(N_SPLIT):
            place[i, h, base + i] = 1.0
    return place


def _cumsum_kernel(logf_ref, place_ref, a_ref, w_scr):
    s = logf_ref.shape[1]
    nblk = s // CS_BLK
    exact = dict(precision=lax.Precision.HIGHEST, preferred_element_type=jnp.float32)
    r = lax.broadcasted_iota(jnp.int32, (CS_BLK, CS_BLK), 0)
    c = lax.broadcasted_iota(jnp.int32, (CS_BLK, CS_BLK), 1)
    lower = (r >= c).astype(jnp.float32)
    blocks = [slice(b * CS_BLK, (b + 1) * CS_BLK) for b in range(nblk)]
    for rows in blocks:
        w_scr[rows, :] = jnp.dot(lower, logf_ref[0, rows, :], **exact)
    totals = w_scr[pl.ds(CS_BLK - 1, nblk, stride=CS_BLK), :]
    rb = lax.broadcasted_iota(jnp.int32, (nblk, nblk), 0)
    cb = lax.broadcasted_iota(jnp.int32, (nblk, nblk), 1)
    carry = jnp.dot((rb > cb).astype(jnp.float32), totals, **exact)
    for b, rows in enumerate(blocks):
        rest = (w_scr[rows, :] + carry[b:b + 1, :]) * (-LOG2E)
        parts = []
        for _ in range(N_SPLIT):
            part = rest.astype(jnp.bfloat16)
            rest = rest - part.astype(jnp.float32)
            parts.append(part)
        a_ref[0, rows, :] = jnp.dot(jnp.concatenate(parts, axis=1), place_ref[...],
                                    preferred_element_type=jnp.float32).astype(jnp.bfloat16)


def _cumsum(logf3):
    b, s, _ = logf3.shape
    place = jnp.asarray(_bias_placement().reshape(N_SPLIT * LANES, D_FOX), jnp.bfloat16)
    return pl.pallas_call(
        _cumsum_kernel,
        grid=(b,),
        in_specs=[pl.BlockSpec((1, s, LANES), lambda i: (i, 0, 0)),
                  pl.BlockSpec((N_SPLIT * LANES, D_FOX), lambda i: (0, 0))],
        out_specs=pl.BlockSpec((1, s, D_FOX), lambda i: (i, 0, 0)),
        out_shape=jax.ShapeDtypeStruct((b, s, D_FOX), jnp.bfloat16),
        scratch_shapes=[pltpu.VMEM((s, LANES), jnp.float32)],
        compiler_params=_params("arbitrary"),
        name="cumsum_logf",
    )(logf3, place)


SWA_TQ = 512
PAIRS_PER_KV = (N_HEADS_SWA // N_KV_SWA) // 2


SWA_R = SWA_TQ // BLOCK
SWA_COLS = PAIRS_PER_KV * BLOCK


def _swa_kernel(sink_ref, q_ref, k_ref, v_ref, o_ref, k_scr, vt_scr, bias_scr, st_scr):
    bi = pl.program_id(0)
    qi = pl.program_id(1)
    s_len = k_ref.shape[1]
    log2_block = int(np.log2(BLOCK))

    @pl.when((bi == 0) & (qi == 0))
    def _():
        key_loc = lax.broadcasted_iota(jnp.int32, (2 * BLOCK, SWA_COLS), 0)
        col = lax.broadcasted_iota(jnp.int32, (2 * BLOCK, SWA_COLS), 1)
        dist = BLOCK + (col & (BLOCK - 1)) - key_loc
        pair = lax.shift_right_logical(col, log2_block)
        distf = dist.astype(jnp.float32)
        in_win = (dist >= 0) & (dist < WINDOW)
        for kk in range(N_KV_SWA):
            for par in range(2):
                slope = jnp.zeros((2 * BLOCK, SWA_COLS), jnp.float32)
                for pi in range(PAIRS_PER_KV):
                    h = kk * 2 * PAIRS_PER_KV + 2 * pi + par
                    slope = jnp.where(pair == pi, ALIBI_SLOPES[h] * LOG2E, slope)
                bias_scr[kk, par] = jnp.where(in_win, -slope * distf, NEG)

    @pl.when(qi == 0)
    def _():
        low = lax.broadcasted_iota(jnp.int32, (SWA_TQ, LANES), 1) < HEAD_DIM
        halves = (low, jnp.logical_not(low))
        zero_k = jnp.zeros((BLOCK, LANES), jnp.bfloat16)
        zero_v = jnp.zeros((LANES, BLOCK), jnp.bfloat16)
        for kk in range(N_KV_SWA):
            for par in range(2):
                k_scr[kk, par, 0:BLOCK, :] = zero_k
                vt_scr[kk, par, :, 0:BLOCK] = zero_v
        for c0 in range(0, s_len, SWA_TQ):
            kc = k_ref[0, c0:c0 + SWA_TQ, :].astype(jnp.float32)
            vc = v_ref[0, c0:c0 + SWA_TQ, :].astype(jnp.float32)
            ksw = pltpu.roll(kc, HEAD_DIM, 1)
            vsw = pltpu.roll(vc, HEAD_DIM, 1)
            rows = slice(BLOCK + c0, BLOCK + c0 + SWA_TQ)
            for kk in range(N_KV_SWA):
                for par in range(2):
                    ksrc, vsrc = (kc, vc) if kk == par else (ksw, vsw)
                    k_scr[kk, par, rows, :] = jnp.where(halves[par], ksrc, 0.0).astype(jnp.bfloat16)
                    vt_scr[kk, par, :, rows] = jnp.where(halves[par], vsrc, 1.0).T.astype(jnp.bfloat16)

    pair1 = lax.shift_right_logical(lax.broadcasted_iota(jnp.int32, (1, SWA_COLS), 1), log2_block)
    pad_rows = lax.broadcasted_iota(jnp.int32, (2 * BLOCK, SWA_COLS), 0) < BLOCK
    first_tile = qi == 0
    sinks2 = {}
    for kk in range(N_KV_SWA):
        for par in range(2):
            sink = jnp.zeros((1, SWA_COLS), jnp.float32)
            for pi in range(PAIRS_PER_KV):
                sink = jnp.where(pair1 == pi, sink_ref[kk * 2 * PAIRS_PER_KV + 2 * pi + par] * LOG2E, sink)
            sinks2[kk, par] = sink

    def band_start(r):
        return pl.multiple_of(qi * SWA_TQ + r * BLOCK, BLOCK)

    def scores(r):
        rows = slice(r * BLOCK, (r + 1) * BLOCK)
        for kk in range(N_KV_SWA):
            qs = jnp.concatenate([q_ref[0, rows, (kk * PAIRS_PER_KV + pi) * LANES:(kk * PAIRS_PER_KV + pi + 1) * LANES]
                                  for pi in range(PAIRS_PER_KV)], axis=0)
            for par in range(2):
                st_scr[r, kk, par] = lax.dot_general(k_scr[kk, par, pl.ds(band_start(r), 2 * BLOCK), :], qs, _NT,
                                                     preferred_element_type=jnp.float32)

    def finish(r):
        rows = slice(r * BLOCK, (r + 1) * BLOCK)
        for kk in range(N_KV_SWA):
            outs = []
            for par in range(2):
                sink = sinks2[kk, par]
                st = st_scr[r, kk, par] + bias_scr[kk, par]
                if r == 0:
                    st = jnp.where(pad_rows & first_tile, NEG, st)
                m = jnp.maximum(jnp.max(st, axis=0, keepdims=True), sink)
                p = jnp.exp2(st - m).astype(jnp.bfloat16)
                o = jnp.dot(vt_scr[kk, par, :, pl.ds(band_start(r), 2 * BLOCK)], p,
                            preferred_element_type=jnp.float32)
                extra = jnp.exp2(sink - m)
                if par == 0:
                    outs.append(o[0:HEAD_DIM] / (o[HEAD_DIM:HEAD_DIM + 1] + extra))
                else:
                    outs.append(o[HEAD_DIM:LANES] / (o[0:1] + extra))
            ot = jnp.concatenate(outs, axis=0).T
            for pi in range(PAIRS_PER_KV):
                c0 = (kk * PAIRS_PER_KV + pi) * LANES
                o_ref[0, rows, c0:c0 + LANES] = ot[pi * BLOCK:(pi + 1) * BLOCK, :]

    scores(0)
    for r in range(SWA_R):
        if r + 1 < SWA_R:
            scores(r + 1)
        finish(r)


def _swa(proj3, sinks):
    b, s, _ = proj3.shape
    return pl.pallas_call(
        _swa_kernel,
        grid=(b, s // SWA_TQ),
        in_specs=[
            pl.BlockSpec(memory_space=pltpu.SMEM),
            pl.BlockSpec((1, SWA_TQ, D_SWA), lambda bi, qi: (bi, qi, COL_QA * LANES // D_SWA)),
            pl.BlockSpec((1, s, LANES), lambda bi, qi: (bi, 0, COL_KA)),
            pl.BlockSpec((1, s, LANES), lambda bi, qi: (bi, 0, COL_VA)),
        ],
        out_specs=pl.BlockSpec((1, SWA_TQ, D_SWA), lambda bi, qi: (bi, qi, 0)),
        scratch_shapes=[
            pltpu.VMEM((N_KV_SWA, 2, s + BLOCK, LANES), jnp.bfloat16),
            pltpu.VMEM((N_KV_SWA, 2, LANES, s + BLOCK), jnp.bfloat16),
            pltpu.VMEM((N_KV_SWA, 2, 2 * BLOCK, SWA_COLS), jnp.float32),
            pltpu.VMEM((SWA_R, N_KV_SWA, 2, 2 * BLOCK, SWA_COLS), jnp.float32),
        ],
        out_shape=jax.ShapeDtypeStruct((b, s, D_SWA), jnp.float32),
        compiler_params=_params("arbitrary", "arbitrary"),
        name="swa_attention",
    )(sinks, proj3, proj3, proj3)


FOX_TQ = 512
FOX_TK = FOX_TQ
FOX_PAIRS = 2
assert all(col % FOX_PAIRS == 0 for col in (COL_QB, COL_KB, COL_VB))


def _fox_kernel(q_ref, k_ref, v_ref, a_ref, o_ref, ka_scr, vat_scr, m_scr, acc_scr, st_scr, cmax_scr):
    qi = pl.program_id(2)
    s_len = k_ref.shape[1]
    low_k = lax.broadcasted_iota(jnp.int32, (FOX_TK, LANES), 1) < HEAD_DIM
    keeps = (low_k, jnp.logical_not(low_k))

    heads = range(2 * FOX_PAIRS)

    @pl.when(qi == 0)
    def _():
        for c0 in range(0, s_len, FOX_TK):
            for pair in range(FOX_PAIRS):
                cols = slice(pair * LANES, (pair + 1) * LANES)
                kc = k_ref[0, c0:c0 + FOX_TK, cols]
                vc = v_ref[0, c0:c0 + FOX_TK, cols].astype(jnp.float32)
                ac = a_ref[0, c0:c0 + FOX_TK, cols]
                for par in range(2):
                    h = 2 * pair + par
                    ka_scr[h, c0:c0 + FOX_TK, :] = jnp.where(keeps[par], kc, ac)
                    vat_scr[h, :, c0:c0 + FOX_TK] = jnp.where(keeps[par], vc, 1.0).T.astype(jnp.bfloat16)

    lane_q = lax.broadcasted_iota(jnp.int32, (FOX_TQ, LANES), 1)
    ones_hi = jnp.where((lane_q >= HEAD_DIM) & (lane_q < HEAD_DIM + N_SPLIT), 1.0, 0.0).astype(jnp.bfloat16)
    ones_lo = jnp.where(lane_q < N_SPLIT, 1.0, 0.0).astype(jnp.bfloat16)
    qa = []
    for pair in range(FOX_PAIRS):
        q = q_ref[0, :, pair * LANES:(pair + 1) * LANES]
        qa += [jnp.where(lane_q < HEAD_DIM, q, ones_hi), jnp.where(lane_q < HEAD_DIM, ones_lo, q)]
    key_row = lax.broadcasted_iota(jnp.int32, (FOX_TK, FOX_TQ), 0)
    qry_col = lax.broadcasted_iota(jnp.int32, (FOX_TK, FOX_TQ), 1)

    m_scr[...] = jnp.full(m_scr.shape, NEG, jnp.float32)
    acc_scr[...] = jnp.zeros(acc_scr.shape, jnp.float32)

    def scores(j, slot):
        k0 = pl.multiple_of(j * FOX_TK, FOX_TK)
        for h in heads:
            st = lax.dot_general(ka_scr[h, pl.ds(k0, FOX_TK), :], qa[h], _NT,
                                 preferred_element_type=jnp.float32)
            st_scr[slot, h] = st
            cmax_scr[slot, h] = jnp.max(st, axis=0, keepdims=True)

    def update(j, slot, diagonal):
        k0 = pl.multiple_of(j * FOX_TK, FOX_TK)
        for h in heads:
            st = st_scr[slot, h]
            if diagonal:
                st = jnp.where(key_row <= qry_col, st, NEG)
                cmax = jnp.max(st, axis=0, keepdims=True)
            else:
                cmax = cmax_scr[slot, h]
            m_old = m_scr[h]
            m_new = jnp.maximum(m_old, cmax)
            alpha = jnp.exp2(m_old - m_new)
            pt = jnp.exp2(st - m_new).astype(jnp.bfloat16)
            acc_scr[h] = alpha * acc_scr[h] + jnp.dot(vat_scr[h, :, pl.ds(k0, FOX_TK)], pt,
                                                      preferred_element_type=jnp.float32)
            m_scr[h] = m_new

    scores(0, 0)

    def body(jj, _):
        j = 2 * jj
        scores(j + 1, 1)
        update(j, 0, False)
        scores(j + 2, 0)
        update(j + 1, 1, False)
        return 0

    lax.fori_loop(0, qi // 2, body, 0)

    @pl.when(qi % 2 == 1)
    def _():
        scores(qi, 1)
        update(qi - 1, 0, False)
        update(qi, 1, True)

    @pl.when(qi % 2 == 0)
    def _():
        update(qi, 0, True)

    for pair in range(FOX_PAIRS):
        acc0 = acc_scr[2 * pair]
        acc1 = acc_scr[2 * pair + 1]
        ot = jnp.concatenate([acc0[0:HEAD_DIM] / acc0[HEAD_DIM:HEAD_DIM + 1],
                              acc1[HEAD_DIM:LANES] / acc1[0:1]], axis=0)
        o_ref[0, :, pair * LANES:(pair + 1) * LANES] = ot.T


def _fox(proj3, aug):
    b, s, _ = proj3.shape
    width = FOX_PAIRS * LANES
    ngroups = D_FOX // width
    nheads = 2 * FOX_PAIRS
    return pl.pallas_call(
        _fox_kernel,
        grid=(b, ngroups, s // FOX_TQ),
        in_specs=[
            pl.BlockSpec((1, FOX_TQ, width), lambda bi, p, qi: (bi, qi, COL_QB // FOX_PAIRS + p)),
            pl.BlockSpec((1, s, width), lambda bi, p, qi: (bi, 0, COL_KB // FOX_PAIRS + p)),
            pl.BlockSpec((1, s, width), lambda bi, p, qi: (bi, 0, COL_VB // FOX_PAIRS + p)),
            pl.BlockSpec((1, s, width), lambda bi, p, qi: (bi, 0, p)),
        ],
        out_specs=pl.BlockSpec((1, FOX_TQ, width), lambda bi, p, qi: (bi, qi, p)),
        out_shape=jax.ShapeDtypeStruct((b, s, D_FOX), jnp.float32),
        scratch_shapes=[
            pltpu.VMEM((nheads, s, LANES), jnp.bfloat16),
            pltpu.VMEM((nheads, LANES, s), jnp.bfloat16),
            pltpu.VMEM((nheads, 1, FOX_TQ), jnp.float32),
            pltpu.VMEM((nheads, LANES, FOX_TQ), jnp.float32),
            pltpu.VMEM((2, nheads, FOX_TK, FOX_TQ), jnp.float32),
            pltpu.VMEM((2, nheads, 1, FOX_TQ), jnp.float32),
        ],
        compiler_params=_params("arbitrary", "arbitrary", "arbitrary"),
        name="fox_attention",
    )(proj3, proj3, proj3, aug)


OUT_TM = 512


def _out_proj_kernel(oa_ref, ob_ref, x_ref, ga_ref, gb_ref, w_ref, gp_ref, x1_ref):
    na = _rms(oa_ref[...], ga_ref[...]).astype(jnp.bfloat16)
    nb = _rms(ob_ref[...], gb_ref[...]).astype(jnp.bfloat16)
    mix = (jnp.dot(na, w_ref[0:D_SWA, :], preferred_element_type=jnp.float32)
           + jnp.dot(nb, w_ref[D_SWA:D_MIX, :], preferred_element_type=jnp.float32))
    x1_ref[...] = x_ref[...] + _rms(mix, gp_ref[...])


def _out_proj(oa, ob, x2d, ga, gb, w_out, gp):
    m = x2d.shape[0]
    row = lambda i: (i, 0)
    fixed = lambda i: (0, 0)
    return pl.pallas_call(
        _out_proj_kernel,
        grid=(m // OUT_TM,),
        in_specs=[
            pl.BlockSpec((OUT_TM, D_SWA), row),
            pl.BlockSpec((OUT_TM, D_FOX), row),
            pl.BlockSpec((OUT_TM, D_MODEL), row),
            pl.BlockSpec((1, D_SWA), fixed),
            pl.BlockSpec((1, D_FOX), fixed),
            pl.BlockSpec((D_MIX, D_MODEL), fixed, pipeline_mode=pl.Buffered(1)),
            pl.BlockSpec((1, D_MODEL), fixed),
        ],
        out_specs=pl.BlockSpec((OUT_TM, D_MODEL), row),
        out_shape=jax.ShapeDtypeStruct((m, D_MODEL), jnp.float32),
        compiler_params=_params("arbitrary"),
        name="out_proj",
    )(oa, ob, x2d, ga, gb, w_out, gp)


FFN_TM = 512
FFN_TF = 512
HALO = 16
N_FT = D_FF // FFN_TF


def _gelu_tanh(x):
    return 0.5 * x * (1.0 + jnp.tanh(np.sqrt(2.0 / np.pi) * (x + 0.044715 * (x * x * x))))


def _ffn_kernel(x_ref, halo_ref, g_ref, wg_ref, wv_ref, cwg_ref, cwv_ref, cbg_ref, cbv_ref, wd_ref, gp_ref,
                o_ref, h_scr, ug_scr, uv_scr, *, tiles_per_seq):
    i = pl.program_id(0)
    j = pl.program_id(1)

    @pl.when(j == 0)
    def _():
        g = g_ref[...]
        hh = _rms(halo_ref[...], g)
        hh = jnp.where(i % tiles_per_seq == 0, 0.0, hh)
        top = jnp.concatenate([jnp.zeros_like(hh), hh], axis=0)
        h_scr[0:HALO, :] = top.astype(jnp.bfloat16)
        h_scr[HALO:, :] = _rms(x_ref[...], g).astype(jnp.bfloat16)
        o_ref[...] = jnp.zeros_like(o_ref)

    h = h_scr[...]
    ug_scr[...] = jnp.dot(h, wg_ref[...], preferred_element_type=jnp.float32)
    uv_scr[...] = jnp.dot(h, wv_ref[...], preferred_element_type=jnp.float32)

    def conv(u_scr, cw_ref, cb_ref):
        y = cb_ref[...]
        for kk in range(CONV_WIDTH):
            y = y + u_scr[pl.ds(HALO - (CONV_WIDTH - 1) + kk, FFN_TM), :] * cw_ref[kk:kk + 1, :]
        return y

    gate = conv(ug_scr, cwg_ref, cbg_ref)
    val = conv(uv_scr, cwv_ref, cbv_ref)
    a = (_gelu_tanh(gate) * val).astype(jnp.bfloat16)
    o_ref[...] += jnp.dot(a, wd_ref[...], preferred_element_type=jnp.float32)

    @pl.when(j == N_FT - 1)
    def _():
        o_ref[...] = x_ref[...] + _rms(o_ref[...], gp_ref[...])


def _ffn(x1, g, w_up, conv_w, conv_b, w_down, gp, seq_len):
    m = x1.shape[0]
    halo_blocks = FFN_TM // 8
    kern = functools.partial(_ffn_kernel, tiles_per_seq=seq_len // FFN_TM)
    return pl.pallas_call(
        kern,
        grid=(m // FFN_TM, N_FT),
        in_specs=[
            pl.BlockSpec((FFN_TM, D_MODEL), lambda i, j: (i, 0)),
            pl.BlockSpec((8, D_MODEL), lambda i, j: (jnp.maximum(i * halo_blocks - 1, 0), 0)),
            pl.BlockSpec((1, D_MODEL), lambda i, j: (0, 0)),
            pl.BlockSpec((D_MODEL, FFN_TF), lambda i, j: (0, j)),
            pl.BlockSpec((D_MODEL, FFN_TF), lambda i, j: (0, j + N_FT)),
            pl.BlockSpec((CONV_WIDTH, FFN_TF), lambda i, j: (0, j)),
            pl.BlockSpec((CONV_WIDTH, FFN_TF), lambda i, j: (0, j + N_FT)),
            pl.BlockSpec((1, FFN_TF), lambda i, j: (0, j)),
            pl.BlockSpec((1, FFN_TF), lambda i, j: (0, j + N_FT)),
            pl.BlockSpec((FFN_TF, D_MODEL), lambda i, j: (j, 0)),
            pl.BlockSpec((1, D_MODEL), lambda i, j: (0, 0)),
        ],
        out_specs=pl.BlockSpec((FFN_TM, D_MODEL), lambda i, j: (i, 0)),
        out_shape=jax.ShapeDtypeStruct((m, D_MODEL), jnp.float32),
        scratch_shapes=[
            pltpu.VMEM((FFN_TM + HALO, D_MODEL), jnp.bfloat16),
            pltpu.VMEM((FFN_TM + HALO, FFN_TF), jnp.float32),
            pltpu.VMEM((FFN_TM + HALO, FFN_TF), jnp.float32),
        ],
        compiler_params=_params("arbitrary", "arbitrary"),
        name="conv_geglu_ffn",
    )(x1, x1, g, w_up, w_up, conv_w, conv_w, conv_b, conv_b, w_down, gp)


def kernel(x, pre_mix_g, w_in, b_forget, sinks, grp_swa_g, grp_fox_g, w_out, post_mix_g,
           pre_ffn_g, w_up, conv_w, conv_b, w_down, post_ffn_g):
    b, s, d = x.shape
    depth = w_in.shape[0]
    bf16 = jnp.bfloat16
    xf = x.reshape(b * s, d)
    for l in range(depth):
        w_in_bf = _cast_pad_rows(jnp.swapaxes(w_in[l], 0, 1), D_IN_PAD)
        b_f = jnp.pad(b_forget[l], (0, LANES - N_HEADS_FOX)).reshape(1, LANES)

        proj, logf = _in_proj(xf, pre_mix_g[l].reshape(1, d), w_in_bf, b_f)
        proj3 = proj.reshape(b, s, D_QKV)
        aug = _cumsum(logf.reshape(b, s, LANES))

        o_a = _swa(proj3, sinks[l])
        o_b = _fox(proj3, aug)

        x1 = _out_proj(o_a.reshape(b * s, D_SWA), o_b.reshape(b * s, D_FOX), xf,
                       grp_swa_g[l].reshape(1, D_SWA), grp_fox_g[l].reshape(1, D_FOX),
                       w_out[l].astype(bf16), post_mix_g[l].reshape(1, d))
        xf = _ffn(x1, pre_ffn_g[l].reshape(1, d), w_up[l].astype(bf16), conv_w[l],
                  conv_b[l].reshape(1, 2 * D_FF), w_down[l].astype(bf16), post_ffn_g[l].reshape(1, d), s)
    return xf.reshape(b, s, d)
```

```python
import functools

import numpy as np
import jax
import jax.numpy as jnp
from jax import lax
from jax.experimental import pallas as pl
from jax.experimental.pallas import tpu as pltpu

D_MODEL = 2048
HEAD_DIM = 64
N_HEADS_SWA = 16
N_KV_SWA = 2
N_HEADS_FOX = 16
WINDOW = 128
BLOCK = 128
D_FF = 5632
CONV_WIDTH = 3
EPS = 1e-6
D_SWA = N_HEADS_SWA * HEAD_DIM
D_KV_SWA = N_KV_SWA * HEAD_DIM
D_FOX = N_HEADS_FOX * HEAD_DIM
D_MIX = D_SWA + D_FOX
D_QKV = D_SWA + 2 * D_KV_SWA + 3 * D_FOX

LANES = 128
COL_QA = 0
COL_KA = COL_QA + D_SWA // LANES
COL_VA = COL_KA + 1
COL_QB = COL_VA + 1
COL_KB = COL_QB + D_FOX // LANES
COL_VB = COL_KB + D_FOX // LANES
D_IN_PAD = D_QKV + LANES

NEG = -1e30
VMEM_LIMIT = 56 * 1024 * 1024

ALIBI_SLOPES = [float(v) for v in np.asarray(2.0 ** (-8.0 * np.arange(1, N_HEADS_SWA + 1) / N_HEADS_SWA),
                                            dtype=np.float32)]

_NT = (((1,), (1,)), ((), ()))


def _rms(xf, g):
    return xf * lax.rsqrt(jnp.mean(xf * xf, axis=-1, keepdims=True) + EPS) * g


def _params(*sem):
    return pltpu.CompilerParams(dimension_semantics=sem, vmem_limit_bytes=VMEM_LIMIT)


CAST_TM = 5 * LANES


def _cast_pad_kernel(w_ref, o_ref, *, valid_rows):
    row = pl.program_id(0) * CAST_TM + lax.broadcasted_iota(jnp.int32, w_ref.shape, 0)
    o_ref[...] = jnp.where(row < valid_rows, w_ref[...], 0.0).astype(jnp.bfloat16)


def _cast_pad_rows(w, padded_rows):
    rows, cols = w.shape
    assert padded_rows % CAST_TM == 0 and padded_rows - rows < CAST_TM
    return pl.pallas_call(
        functools.partial(_cast_pad_kernel, valid_rows=rows),
        grid=(padded_rows // CAST_TM,),
        in_specs=[pl.BlockSpec((CAST_TM, cols), lambda j: (j, 0))],
        out_specs=pl.BlockSpec((CAST_TM, cols), lambda j: (j, 0)),
        out_shape=jax.ShapeDtypeStruct((padded_rows, cols), jnp.bfloat16),
        compiler_params=_params("arbitrary"),
        name="cast_w_in",
    )(w)


IN_TM = 512
IN_CHUNK = 1024
LOG2E = float(np.log2(np.e))
Q_MULT = LOG2E / float(np.sqrt(HEAD_DIM))


Q_RANGES = ((COL_QA * LANES, COL_QA * LANES + D_SWA), (COL_QB * LANES, COL_QB * LANES + D_FOX))


def _in_proj_kernel(x_ref, g_ref, w_ref, wf_ref, bf_ref, *refs):
    n_side = (len(refs) - 2) // 2
    side_in, (proj_ref, logf_ref), side_out = refs[:n_side], refs[n_side:n_side + 2], refs[n_side + 2:]
    for src, dst in zip(side_in, side_out):
        dst[...] = src[...].astype(jnp.bfloat16)
    h = _rms(x_ref[...], g_ref[...]).astype(jnp.bfloat16)
    for n0 in range(0, D_QKV, IN_CHUNK):
        n1 = min(n0 + IN_CHUNK, D_QKV)
        acc = lax.dot_general(h, w_ref[n0:n1, :], _NT, preferred_element_type=jnp.float32)
        overlaps = [(max(lo, n0), min(hi, n1)) for lo, hi in Q_RANGES if max(lo, n0) < min(hi, n1)]
        if overlaps:
            col = n0 + lax.broadcasted_iota(jnp.int32, (1, n1 - n0), 1)
            mult = jnp.ones((1, n1 - n0), jnp.float32)
            for lo, hi in overlaps:
                mult = jnp.where((col >= lo) & (col < hi), Q_MULT, mult)
            acc = acc * mult
        proj_ref[:, n0:n1] = acc.astype(jnp.bfloat16)
    f = lax.dot_general(h, wf_ref[...], _NT, preferred_element_type=jnp.float32) + bf_ref[...]
    logf_ref[...] = jnp.minimum(f, 0.0) - jnp.log1p(jnp.exp(-jnp.abs(f)))


def _in_proj(x2d, g, w_in_bf, b_f, side_weights):
    m = x2d.shape[0]
    steps = m // IN_TM
    side_specs = []
    for w in side_weights:
        rows, cols = w.shape
        assert rows % (steps * 16) == 0
        side_specs.append(pl.BlockSpec((rows // steps, cols), lambda i: (i, 0)))
    return pl.pallas_call(
        _in_proj_kernel,
        grid=(steps,),
        in_specs=[
            pl.BlockSpec((IN_TM, D_MODEL), lambda i: (i, 0)),
            pl.BlockSpec((1, D_MODEL), lambda i: (0, 0)),
            pl.BlockSpec((D_QKV, D_MODEL), lambda i: (0, 0), pipeline_mode=pl.Buffered(1)),
            pl.BlockSpec((LANES, D_MODEL), lambda i: (D_QKV // LANES, 0), pipeline_mode=pl.Buffered(1)),
            pl.BlockSpec((1, LANES), lambda i: (0, 0)),
        ] + side_specs,
        out_specs=[
            pl.BlockSpec((IN_TM, D_QKV), lambda i: (i, 0)),
            pl.BlockSpec((IN_TM, LANES), lambda i: (i, 0)),
        ] + side_specs,
        out_shape=[
            jax.ShapeDtypeStruct((m, D_QKV), jnp.bfloat16),
            jax.ShapeDtypeStruct((m, LANES), jnp.float32),
        ] + [jax.ShapeDtypeStruct(w.shape, jnp.bfloat16) for w in side_weights],
        compiler_params=_params("arbitrary"),
        name="in_proj",
    )(x2d, g, w_in_bf, w_in_bf, b_f, *side_weights)


CS_BLK = 128
N_SPLIT = 3


def _bias_placement():
    place = np.zeros((N_SPLIT, LANES, D_FOX), np.float32)
    for h in range(N_HEADS_FOX):
        base = (h // 2) * LANES + (HEAD_DIM if h % 2 == 0 else 0)
        for i in range(N_SPLIT):
            place[i, h, base + i] = 1.0
    return place


def _cumsum_kernel(logf_ref, place_ref, a_ref, w_scr):
    s = logf_ref.shape[1]
    nblk = s // CS_BLK
    exact = dict(precision=lax.Precision.HIGHEST, preferred_element_type=jnp.float32)
    r = lax.broadcasted_iota(jnp.int32, (CS_BLK, CS_BLK), 0)
    c = lax.broadcasted_iota(jnp.int32, (CS_BLK, CS_BLK), 1)
    lower = (r >= c).astype(jnp.float32)
    blocks = [slice(b * CS_BLK, (b + 1) * CS_BLK) for b in range(nblk)]
    for rows in blocks:
        w_scr[rows, :] = jnp.dot(lower, logf_ref[0, rows, :], **exact)
    totals = w_scr[pl.ds(CS_BLK - 1, nblk, stride=CS_BLK), :]
    rb = lax.broadcasted_iota(jnp.int32, (nblk, nblk), 0)
    cb = lax.broadcasted_iota(jnp.int32, (nblk, nblk), 1)
    carry = jnp.dot((rb > cb).astype(jnp.float32), totals, **exact)
    for b, rows in enumerate(blocks):
        rest = (w_scr[rows, :] + carry[b:b + 1, :]) * (-LOG2E)
        parts = []
        for _ in range(N_SPLIT):
            part = rest.astype(jnp.bfloat16)
            rest = rest - part.astype(jnp.float32)
            parts.append(part)
        a_ref[0, rows, :] = jnp.dot(jnp.concatenate(parts, axis=1), place_ref[...],
                                    preferred_element_type=jnp.float32).astype(jnp.bfloat16)


def _cumsum(logf3):
    b, s, _ = logf3.shape
    place = jnp.asarray(_bias_placement().reshape(N_SPLIT * LANES, D_FOX), jnp.bfloat16)
    return pl.pallas_call(
        _cumsum_kernel,
        grid=(b,),
        in_specs=[pl.BlockSpec((1, s, LANES), lambda i: (i, 0, 0)),
                  pl.BlockSpec((N_SPLIT * LANES, D_FOX), lambda i: (0, 0))],
        out_specs=pl.BlockSpec((1, s, D_FOX), lambda i: (i, 0, 0)),
        out_shape=jax.ShapeDtypeStruct((b, s, D_FOX), jnp.bfloat16),
        scratch_shapes=[pltpu.VMEM((s, LANES), jnp.float32)],
        compiler_params=_params("arbitrary"),
        name="cumsum_logf",
    )(logf3, place)


SWA_TQ = 512
PAIRS_PER_KV = (N_HEADS_SWA // N_KV_SWA) // 2


SWA_R = SWA_TQ // BLOCK
SWA_COLS = PAIRS_PER_KV * BLOCK


def _swa_kernel(sink_ref, q_ref, k_ref, v_ref, o_ref, k_scr, vt_scr, bias_scr, st_scr):
    bi = pl.program_id(0)
    qi = pl.program_id(1)
    s_len = k_ref.shape[1]
    log2_block = int(np.log2(BLOCK))

    @pl.when((bi == 0) & (qi == 0))
    def _():
        key_loc = lax.broadcasted_iota(jnp.int32, (2 * BLOCK, SWA_COLS), 0)
        col = lax.broadcasted_iota(jnp.int32, (2 * BLOCK, SWA_COLS), 1)
        dist = BLOCK + (col & (BLOCK - 1)) - key_loc
        pair = lax.shift_right_logical(col, log2_block)
        distf = dist.astype(jnp.float32)
        in_win = (dist >= 0) & (dist < WINDOW)
        for kk in range(N_KV_SWA):
            for par in range(2):
                slope = jnp.zeros((2 * BLOCK, SWA_COLS), jnp.float32)
                for pi in range(PAIRS_PER_KV):
                    h = kk * 2 * PAIRS_PER_KV + 2 * pi + par
                    slope = jnp.where(pair == pi, ALIBI_SLOPES[h] * LOG2E, slope)
                bias_scr[kk, par] = jnp.where(in_win, -slope * distf, NEG)

    @pl.when(qi == 0)
    def _():
        low = lax.broadcasted_iota(jnp.int32, (SWA_TQ, LANES), 1) < HEAD_DIM
        halves = (low, jnp.logical_not(low))
        zero_k = jnp.zeros((BLOCK, LANES), jnp.bfloat16)
        zero_v = jnp.zeros((LANES, BLOCK), jnp.bfloat16)
        for kk in range(N_KV_SWA):
            for par in range(2):
                k_scr[kk, par, 0:BLOCK, :] = zero_k
                vt_scr[kk, par, :, 0:BLOCK] = zero_v
        for c0 in range(0, s_len, SWA_TQ):
            kc = k_ref[0, c0:c0 + SWA_TQ, :].astype(jnp.float32)
            vc = v_ref[0, c0:c0 + SWA_TQ, :].astype(jnp.float32)
            ksw = pltpu.roll(kc, HEAD_DIM, 1)
            vsw = pltpu.roll(vc, HEAD_DIM, 1)
            rows = slice(BLOCK + c0, BLOCK + c0 + SWA_TQ)
            for kk in range(N_KV_SWA):
                for par in range(2):
                    ksrc, vsrc = (kc, vc) if kk == par else (ksw, vsw)
                    k_scr[kk, par, rows, :] = jnp.where(halves[par], ksrc, 0.0).astype(jnp.bfloat16)
                    vt_scr[kk, par, :, rows] = jnp.where(halves[par], vsrc, 1.0).T.astype(jnp.bfloat16)

    pair1 = lax.shift_right_logical(lax.broadcasted_iota(jnp.int32, (1, SWA_COLS), 1), log2_block)
    pad_rows = lax.broadcasted_iota(jnp.int32, (2 * BLOCK, SWA_COLS), 0) < BLOCK
    first_tile = qi == 0
    sinks2 = {}
    for kk in range(N_KV_SWA):
        for par in range(2):
            sink = jnp.zeros((1, SWA_COLS), jnp.float32)
            for pi in range(PAIRS_PER_KV):
                sink = jnp.where(pair1 == pi, sink_ref[kk * 2 * PAIRS_PER_KV + 2 * pi + par] * LOG2E, sink)
            sinks2[kk, par] = sink

    def band_start(r):
        return pl.multiple_of(qi * SWA_TQ + r * BLOCK, BLOCK)

    def scores(r):
        rows = slice(r * BLOCK, (r + 1) * BLOCK)
        for kk in range(N_KV_SWA):
            qs = jnp.concatenate([q_ref[0, rows, (kk * PAIRS_PER_KV + pi) * LANES:(kk * PAIRS_PER_KV + pi + 1) * LANES]
                                  for pi in range(PAIRS_PER_KV)], axis=0)
            for par in range(2):
                st_scr[r, kk, par] = lax.dot_general(k_scr[kk, par, pl.ds(band_start(r), 2 * BLOCK), :], qs, _NT,
                                                     preferred_element_type=jnp.float32)

    def finish(r):
        rows = slice(r * BLOCK, (r + 1) * BLOCK)
        for kk in range(N_KV_SWA):
            outs = []
            for par in range(2):
                sink = sinks2[kk, par]
                st = st_scr[r, kk, par] + bias_scr[kk, par]
                if r == 0:
                    st = jnp.where(pad_rows & first_tile, NEG, st)
                m = jnp.maximum(jnp.max(st, axis=0, keepdims=True), sink)
                p = jnp.exp2(st - m).astype(jnp.bfloat16)
                o = jnp.dot(vt_scr[kk, par, :, pl.ds(band_start(r), 2 * BLOCK)], p,
                            preferred_element_type=jnp.float32)
                extra = jnp.exp2(sink - m)
                if par == 0:
                    outs.append(o[0:HEAD_DIM] / (o[HEAD_DIM:HEAD_DIM + 1] + extra))
                else:
                    outs.append(o[HEAD_DIM:LANES] / (o[0:1] + extra))
            ot = jnp.concatenate(outs, axis=0).T
            for pi in range(PAIRS_PER_KV):
                c0 = (kk * PAIRS_PER_KV + pi) * LANES
                o_ref[0, rows, c0:c0 + LANES] = ot[pi * BLOCK:(pi + 1) * BLOCK, :]

    scores(0)
    for r in range(SWA_R):
        if r + 1 < SWA_R:
            scores(r + 1)
        finish(r)


def _swa(proj3, sinks):
    b, s, _ = proj3.shape
    return pl.pallas_call(
        _swa_kernel,
        grid=(b, s // SWA_TQ),
        in_specs=[
            pl.BlockSpec(memory_space=pltpu.SMEM),
            pl.BlockSpec((1, SWA_TQ, D_SWA), lambda bi, qi: (bi, qi, COL_QA * LANES // D_SWA)),
            pl.BlockSpec((1, s, LANES), lambda bi, qi: (bi, 0, COL_KA)),
            pl.BlockSpec((1, s, LANES), lambda bi, qi: (bi, 0, COL_VA)),
        ],
        out_specs=pl.BlockSpec((1, SWA_TQ, D_SWA), lambda bi, qi: (bi, qi, 0)),
        scratch_shapes=[
            pltpu.VMEM((N_KV_SWA, 2, s + BLOCK, LANES), jnp.bfloat16),
            pltpu.VMEM((N_KV_SWA, 2, LANES, s + BLOCK), jnp.bfloat16),
            pltpu.VMEM((N_KV_SWA, 2, 2 * BLOCK, SWA_COLS), jnp.float32),
            pltpu.VMEM((SWA_R, N_KV_SWA, 2, 2 * BLOCK, SWA_COLS), jnp.float32),
        ],
        out_shape=jax.ShapeDtypeStruct((b, s, D_SWA), jnp.float32),
        compiler_params=_params("arbitrary", "arbitrary"),
        name="swa_attention",
    )(sinks, proj3, proj3, proj3)


FOX_TQ = 512
FOX_TK = FOX_TQ
FOX_PAIRS = 2
assert all(col % FOX_PAIRS == 0 for col in (COL_QB, COL_KB, COL_VB))


def _fox_kernel(q_ref, k_ref, v_ref, a_ref, o_ref, ka_scr, vat_scr, m_scr, acc_scr, st_scr, cmax_scr):
    qi = pl.program_id(2)
    s_len = k_ref.shape[1]
    low_k = lax.broadcasted_iota(jnp.int32, (FOX_TK, LANES), 1) < HEAD_DIM
    keeps = (low_k, jnp.logical_not(low_k))

    heads = range(2 * FOX_PAIRS)

    @pl.when(qi == 0)
    def _():
        for c0 in range(0, s_len, FOX_TK):
            for pair in range(FOX_PAIRS):
                cols = slice(pair * LANES, (pair + 1) * LANES)
                kc = k_ref[0, c0:c0 + FOX_TK, cols]
                vc = v_ref[0, c0:c0 + FOX_TK, cols].astype(jnp.float32)
                ac = a_ref[0, c0:c0 + FOX_TK, cols]
                for par in range(2):
                    h = 2 * pair + par
                    ka_scr[h, c0:c0 + FOX_TK, :] = jnp.where(keeps[par], kc, ac)
                    vat_scr[h, :, c0:c0 + FOX_TK] = jnp.where(keeps[par], vc, 1.0).T.astype(jnp.bfloat16)

    lane_q = lax.broadcasted_iota(jnp.int32, (FOX_TQ, LANES), 1)
    ones_hi = jnp.where((lane_q >= HEAD_DIM) & (lane_q < HEAD_DIM + N_SPLIT), 1.0, 0.0).astype(jnp.bfloat16)
    ones_lo = jnp.where(lane_q < N_SPLIT, 1.0, 0.0).astype(jnp.bfloat16)
    qa = []
    for pair in range(FOX_PAIRS):
        q = q_ref[0, :, pair * LANES:(pair + 1) * LANES]
        qa += [jnp.where(lane_q < HEAD_DIM, q, ones_hi), jnp.where(lane_q < HEAD_DIM, ones_lo, q)]
    key_row = lax.broadcasted_iota(jnp.int32, (FOX_TK, FOX_TQ), 0)
    qry_col = lax.broadcasted_iota(jnp.int32, (FOX_TK, FOX_TQ), 1)

    m_scr[...] = jnp.full(m_scr.shape, NEG, jnp.float32)
    acc_scr[...] = jnp.zeros(acc_scr.shape, jnp.float32)

    def scores(j, slot):
        k0 = pl.multiple_of(j * FOX_TK, FOX_TK)
        for h in heads:
            st = lax.dot_general(ka_scr[h, pl.ds(k0, FOX_TK), :], qa[h], _NT,
                                 preferred_element_type=jnp.float32)
            st_scr[slot, h] = st
            cmax_scr[slot, h] = jnp.max(st, axis=0, keepdims=True)

    def update(j, slot, diagonal):
        k0 = pl.multiple_of(j * FOX_TK, FOX_TK)
        for h in heads:
            st = st_scr[slot, h]
            if diagonal:
                st = jnp.where(key_row <= qry_col, st, NEG)
                cmax = jnp.max(st, axis=0, keepdims=True)
            else:
                cmax = cmax_scr[slot, h]
            m_old = m_scr[h]
            m_new = jnp.maximum(m_old, cmax)
            alpha = jnp.exp2(m_old - m_new)
            pt = jnp.exp2(st - m_new).astype(jnp.bfloat16)
            acc_scr[h] = alpha * acc_scr[h] + jnp.dot(vat_scr[h, :, pl.ds(k0, FOX_TK)], pt,
                                                      preferred_element_type=jnp.float32)
            m_scr[h] = m_new

    scores(0, 0)

    def body(jj, _):
        j = 2 * jj
        scores(j + 1, 1)
        update(j, 0, False)
        scores(j + 2, 0)
        update(j + 1, 1, False)
        return 0

    lax.fori_loop(0, qi // 2, body, 0)

    @pl.when(qi % 2 == 1)
    def _():
        scores(qi, 1)
        update(qi - 1, 0, False)
        update(qi, 1, True)

    @pl.when(qi % 2 == 0)
    def _():
        update(qi, 0, True)

    for pair in range(FOX_PAIRS):
        acc0 = acc_scr[2 * pair]
        acc1 = acc_scr[2 * pair + 1]
        ot = jnp.concatenate([acc0[0:HEAD_DIM] / acc0[HEAD_DIM:HEAD_DIM + 1],
                              acc1[HEAD_DIM:LANES] / acc1[0:1]], axis=0)
        o_ref[0, :, pair * LANES:(pair + 1) * LANES] = ot.T


def _fox(proj3, aug):
    b, s, _ = proj3.shape
    width = FOX_PAIRS * LANES
    ngroups = D_FOX // width
    nheads = 2 * FOX_PAIRS
    return pl.pallas_call(
        _fox_kernel,
        grid=(b, ngroups, s // FOX_TQ),
        in_specs=[
            pl.BlockSpec((1, FOX_TQ, width), lambda bi, p, qi: (bi, qi, COL_QB // FOX_PAIRS + p)),
            pl.BlockSpec((1, s, width), lambda bi, p, qi: (bi, 0, COL_KB // FOX_PAIRS + p)),
            pl.BlockSpec((1, s, width), lambda bi, p, qi: (bi, 0, COL_VB // FOX_PAIRS + p)),
            pl.BlockSpec((1, s, width), lambda bi, p, qi: (bi, 0, p)),
        ],
        out_specs=pl.BlockSpec((1, FOX_TQ, width), lambda bi, p, qi: (bi, qi, p)),
        out_shape=jax.ShapeDtypeStruct((b, s, D_FOX), jnp.float32),
        scratch_shapes=[
            pltpu.VMEM((nheads, s, LANES), jnp.bfloat16),
            pltpu.VMEM((nheads, LANES, s), jnp.bfloat16),
            pltpu.VMEM((nheads, 1, FOX_TQ), jnp.float32),
            pltpu.VMEM((nheads, LANES, FOX_TQ), jnp.float32),
            pltpu.VMEM((2, nheads, FOX_TK, FOX_TQ), jnp.float32),
            pltpu.VMEM((2, nheads, 1, FOX_TQ), jnp.float32),
        ],
        compiler_params=_params("arbitrary", "arbitrary", "arbitrary"),
        name="fox_attention",
    )(proj3, proj3, proj3, aug)


OUT_TM = 512


def _out_proj_kernel(oa_ref, ob_ref, x_ref, ga_ref, gb_ref, w_ref, gp_ref, x1_ref):
    na = _rms(oa_ref[...], ga_ref[...]).astype(jnp.bfloat16)
    nb = _rms(ob_ref[...], gb_ref[...]).astype(jnp.bfloat16)
    mix = (jnp.dot(na, w_ref[0:D_SWA, :], preferred_element_type=jnp.float32)
           + jnp.dot(nb, w_ref[D_SWA:D_MIX, :], preferred_element_type=jnp.float32))
    x1_ref[...] = x_ref[...] + _rms(mix, gp_ref[...])


def _out_proj(oa, ob, x2d, ga, gb, w_out, gp):
    m = x2d.shape[0]
    row = lambda i: (i, 0)
    fixed = lambda i: (0, 0)
    return pl.pallas_call(
        _out_proj_kernel,
        grid=(m // OUT_TM,),
        in_specs=[
            pl.BlockSpec((OUT_TM, D_SWA), row),
            pl.BlockSpec((OUT_TM, D_FOX), row),
            pl.BlockSpec((OUT_TM, D_MODEL), row),
            pl.BlockSpec((1, D_SWA), fixed),
            pl.BlockSpec((1, D_FOX), fixed),
            pl.BlockSpec((D_MIX, D_MODEL), fixed, pipeline_mode=pl.Buffered(1)),
            pl.BlockSpec((1, D_MODEL), fixed),
        ],
        out_specs=pl.BlockSpec((OUT_TM, D_MODEL), row),
        out_shape=jax.ShapeDtypeStruct((m, D_MODEL), jnp.float32),
        compiler_params=_params("arbitrary"),
        name="out_proj",
    )(oa, ob, x2d, ga, gb, w_out, gp)


FFN_TM = 512
FFN_TF = 512
HALO = 16
N_FT = D_FF // FFN_TF


def _gelu_tanh(x):
    return 0.5 * x * (1.0 + jnp.tanh(np.sqrt(2.0 / np.pi) * (x + 0.044715 * (x * x * x))))


def _ffn_kernel(x_ref, halo_ref, g_ref, wg_ref, wv_ref, cwg_ref, cwv_ref, cbg_ref, cbv_ref, wd_ref, gp_ref,
                o_ref, h_scr, ug_scr, uv_scr, *, tiles_per_seq):
    i = pl.program_id(0)
    j = pl.program_id(1)

    @pl.when(j == 0)
    def _():
        g = g_ref[...]
        hh = _rms(halo_ref[...], g)
        hh = jnp.where(i % tiles_per_seq == 0, 0.0, hh)
        top = jnp.concatenate([jnp.zeros_like(hh), hh], axis=0)
        h_scr[0:HALO, :] = top.astype(jnp.bfloat16)
        h_scr[HALO:, :] = _rms(x_ref[...], g).astype(jnp.bfloat16)
        o_ref[...] = jnp.zeros_like(o_ref)

    h = h_scr[...]
    ug_scr[...] = jnp.dot(h, wg_ref[...], preferred_element_type=jnp.float32)
    uv_scr[...] = jnp.dot(h, wv_ref[...], preferred_element_type=jnp.float32)

    def conv(u_scr, cw_ref, cb_ref):
        y = cb_ref[...]
        for kk in range(CONV_WIDTH):
            y = y + u_scr[pl.ds(HALO - (CONV_WIDTH - 1) + kk, FFN_TM), :] * cw_ref[kk:kk + 1, :]
        return y

    gate = conv(ug_scr, cwg_ref, cbg_ref)
    val = conv(uv_scr, cwv_ref, cbv_ref)
    a = (_gelu_tanh(gate) * val).astype(jnp.bfloat16)
    o_ref[...] += jnp.dot(a, wd_ref[...], preferred_element_type=jnp.float32)

    @pl.when(j == N_FT - 1)
    def _():
        o_ref[...] = x_ref[...] + _rms(o_ref[...], gp_ref[...])


def _ffn(x1, g, w_up, conv_w, conv_b, w_down, gp, seq_len):
    m = x1.shape[0]
    halo_blocks = FFN_TM // 8
    kern = functools.partial(_ffn_kernel, tiles_per_seq=seq_len // FFN_TM)
    return pl.pallas_call(
        kern,
        grid=(m // FFN_TM, N_FT),
        in_specs=[
            pl.BlockSpec((FFN_TM, D_MODEL), lambda i, j: (i, 0)),
            pl.BlockSpec((8, D_MODEL), lambda i, j: (jnp.maximum(i * halo_blocks - 1, 0), 0)),
            pl.BlockSpec((1, D_MODEL), lambda i, j: (0, 0)),
            pl.BlockSpec((D_MODEL, FFN_TF), lambda i, j: (0, j)),
            pl.BlockSpec((D_MODEL, FFN_TF), lambda i, j: (0, j + N_FT)),
            pl.BlockSpec((CONV_WIDTH, FFN_TF), lambda i, j: (0, j)),
            pl.BlockSpec((CONV_WIDTH, FFN_TF), lambda i, j: (0, j + N_FT)),
            pl.BlockSpec((1, FFN_TF), lambda i, j: (0, j)),
            pl.BlockSpec((1, FFN_TF), lambda i, j: (0, j + N_FT)),
            pl.BlockSpec((FFN_TF, D_MODEL), lambda i, j: (j, 0)),
            pl.BlockSpec((1, D_MODEL), lambda i, j: (0, 0)),
        ],
        out_specs=pl.BlockSpec((FFN_TM, D_MODEL), lambda i, j: (i, 0)),
        out_shape=jax.ShapeDtypeStruct((m, D_MODEL), jnp.float32),
        scratch_shapes=[
            pltpu.VMEM((FFN_TM + HALO, D_MODEL), jnp.bfloat16),
            pltpu.VMEM((FFN_TM + HALO, FFN_TF), jnp.float32),
            pltpu.VMEM((FFN_TM + HALO, FFN_TF), jnp.float32),
        ],
        compiler_params=_params("arbitrary", "arbitrary"),
        name="conv_geglu_ffn",
    )(x1, x1, g, w_up, w_up, conv_w, conv_w, conv_b, conv_b, w_down, gp)


def kernel(x, pre_mix_g, w_in, b_forget, sinks, grp_swa_g, grp_fox_g, w_out, post_mix_g,
           pre_ffn_g, w_up, conv_w, conv_b, w_down, post_ffn_g):
    b, s, d = x.shape
    depth = w_in.shape[0]
    xf = x.reshape(b * s, d)
    for l in range(depth):
        w_in_bf = _cast_pad_rows(jnp.swapaxes(w_in[l], 0, 1), D_IN_PAD)
        b_f = jnp.pad(b_forget[l], (0, LANES - N_HEADS_FOX)).reshape(1, LANES)

        proj, logf, w_out_bf, w_up_bf, w_down_bf = _in_proj(xf, pre_mix_g[l].reshape(1, d), w_in_bf, b_f,
                                                            (w_out[l], w_up[l], w_down[l]))
        proj3 = proj.reshape(b, s, D_QKV)
        aug = _cumsum(logf.reshape(b, s, LANES))

        o_a = _swa(proj3, sinks[l])
        o_b = _fox(proj3, aug)

        x1 = _out_proj(o_a.reshape(b * s, D_SWA), o_b.reshape(b * s, D_FOX), xf,
                       grp_swa_g[l].reshape(1, D_SWA), grp_fox_g[l].reshape(1, D_FOX),
                       w_out_bf, post_mix_g[l].reshape(1, d))
        xf = _ffn(x1, pre_ffn_g[l].reshape(1, d), w_up_bf, conv_w[l],
                  conv_b[l].reshape(1, 2 * D_FF), w_down_bf, post_ffn_g[l].reshape(1, d), s)
    return xf.reshape(b, s, d)
```

```python
import functools

import numpy as np
import jax
import jax.numpy as jnp
from jax import lax
from jax.experimental import pallas as pl
from jax.experimental.pallas import tpu as pltpu

D_MODEL = 2048
HEAD_DIM = 64
N_HEADS_SWA = 16
N_KV_SWA = 2
N_HEADS_FOX = 16
WINDOW = 128
BLOCK = 128
D_FF = 5632
CONV_WIDTH = 3
EPS = 1e-6
D_SWA = N_HEADS_SWA * HEAD_DIM
D_KV_SWA = N_KV_SWA * HEAD_DIM
D_FOX = N_HEADS_FOX * HEAD_DIM
D_MIX = D_SWA + D_FOX
D_QKV = D_SWA + 2 * D_KV_SWA + 3 * D_FOX

LANES = 128
COL_QA = 0
COL_KA = COL_QA + D_SWA // LANES
COL_VA = COL_KA + 1
COL_QB = COL_VA + 1
COL_KB = COL_QB + D_FOX // LANES
COL_VB = COL_KB + D_FOX // LANES
D_IN_PAD = D_QKV + LANES

NEG = -1e30
VMEM_LIMIT = 56 * 1024 * 1024

ALIBI_SLOPES = [float(v) for v in np.asarray(2.0 ** (-8.0 * np.arange(1, N_HEADS_SWA + 1) / N_HEADS_SWA),
                                            dtype=np.float32)]

_NT = (((1,), (1,)), ((), ()))


def _rms(xf, g):
    return xf * lax.rsqrt(jnp.mean(xf * xf, axis=-1, keepdims=True) + EPS) * g


def _params(*sem):
    return pltpu.CompilerParams(dimension_semantics=sem, vmem_limit_bytes=VMEM_LIMIT)


CAST_TM = 5 * LANES


def _cast_pad_kernel(w_ref, o_ref, *, valid_rows):
    row = pl.program_id(0) * CAST_TM + lax.broadcasted_iota(jnp.int32, w_ref.shape, 0)
    o_ref[...] = jnp.where(row < valid_rows, w_ref[...], 0.0).astype(jnp.bfloat16)


def _cast_pad_rows(w, padded_rows):
    rows, cols = w.shape
    assert padded_rows % CAST_TM == 0 and padded_rows - rows < CAST_TM
    return pl.pallas_call(
        functools.partial(_cast_pad_kernel, valid_rows=rows),
        grid=(padded_rows // CAST_TM,),
        in_specs=[pl.BlockSpec((CAST_TM, cols), lambda j: (j, 0))],
        out_specs=pl.BlockSpec((CAST_TM, cols), lambda j: (j, 0)),
        out_shape=jax.ShapeDtypeStruct((padded_rows, cols), jnp.bfloat16),
        compiler_params=_params("arbitrary"),
        name="cast_w_in",
    )(w)


IN_TM = 512
IN_CHUNK = 1024
LOG2E = float(np.log2(np.e))
Q_MULT = LOG2E / float(np.sqrt(HEAD_DIM))


Q_RANGES = ((COL_QA * LANES, COL_QA * LANES + D_SWA), (COL_QB * LANES, COL_QB * LANES + D_FOX))


def _in_proj_kernel(x_ref, g_ref, w_ref, wf_ref, bf_ref, *refs):
    n_side = (len(refs) - 2) // 2
    side_in, (proj_ref, logf_ref), side_out = refs[:n_side], refs[n_side:n_side + 2], refs[n_side + 2:]
    for src, dst in zip(side_in, side_out):
        dst[...] = src[...].astype(jnp.bfloat16)
    h = _rms(x_ref[...], g_ref[...]).astype(jnp.bfloat16)
    for n0 in range(0, D_QKV, IN_CHUNK):
        n1 = min(n0 + IN_CHUNK, D_QKV)
        acc = lax.dot_general(h, w_ref[n0:n1, :], _NT, preferred_element_type=jnp.float32)
        overlaps = [(max(lo, n0), min(hi, n1)) for lo, hi in Q_RANGES if max(lo, n0) < min(hi, n1)]
        if overlaps:
            col = n0 + lax.broadcasted_iota(jnp.int32, (1, n1 - n0), 1)
            mult = jnp.ones((1, n1 - n0), jnp.float32)
            for lo, hi in overlaps:
                mult = jnp.where((col >= lo) & (col < hi), Q_MULT, mult)
            acc = acc * mult
        proj_ref[:, n0:n1] = acc.astype(jnp.bfloat16)
    f = lax.dot_general(h, wf_ref[...], _NT, preferred_element_type=jnp.float32) + bf_ref[...]
    logf_ref[...] = jnp.minimum(f, 0.0) - jnp.log1p(jnp.exp(-jnp.abs(f)))


def _in_proj(x2d, g, w_in_bf, b_f, side_weights):
    m = x2d.shape[0]
    steps = m // IN_TM
    side_specs = []
    for w in side_weights:
        rows, cols = w.shape
        assert rows % (steps * 16) == 0
        side_specs.append(pl.BlockSpec((rows // steps, cols), lambda i: (i, 0)))
    return pl.pallas_call(
        _in_proj_kernel,
        grid=(steps,),
        in_specs=[
            pl.BlockSpec((IN_TM, D_MODEL), lambda i: (i, 0)),
            pl.BlockSpec((1, D_MODEL), lambda i: (0, 0)),
            pl.BlockSpec((D_QKV, D_MODEL), lambda i: (0, 0), pipeline_mode=pl.Buffered(1)),
            pl.BlockSpec((LANES, D_MODEL), lambda i: (D_QKV // LANES, 0), pipeline_mode=pl.Buffered(1)),
            pl.BlockSpec((1, LANES), lambda i: (0, 0)),
        ] + side_specs,
        out_specs=[
            pl.BlockSpec((IN_TM, D_QKV), lambda i: (i, 0)),
            pl.BlockSpec((IN_TM, LANES), lambda i: (i, 0)),
        ] + side_specs,
        out_shape=[
            jax.ShapeDtypeStruct((m, D_QKV), jnp.bfloat16),
            jax.ShapeDtypeStruct((m, LANES), jnp.float32),
        ] + [jax.ShapeDtypeStruct(w.shape, jnp.bfloat16) for w in side_weights],
        compiler_params=_params("arbitrary"),
        name="in_proj",
    )(x2d, g, w_in_bf, w_in_bf, b_f, *side_weights)


CS_BLK = 128
N_SPLIT = 3


def _bias_placement():
    place = np.zeros((N_SPLIT, LANES, D_FOX), np.float32)
    for h in range(N_HEADS_FOX):
        base = (h // 2) * LANES + (HEAD_DIM if h % 2 == 0 else 0)
        for i in range(N_SPLIT):
            place[i, h, base + i] = 1.0
    return place


def _cumsum_kernel(logf_ref, place_ref, a_ref, w_scr):
    s = logf_ref.shape[1]
    nblk = s // CS_BLK
    exact = dict(precision=lax.Precision.HIGHEST, preferred_element_type=jnp.float32)
    r = lax.broadcasted_iota(jnp.int32, (CS_BLK, CS_BLK), 0)
    c = lax.broadcasted_iota(jnp.int32, (CS_BLK, CS_BLK), 1)
    lower = (r >= c).astype(jnp.float32)
    blocks = [slice(b * CS_BLK, (b + 1) * CS_BLK) for b in range(nblk)]
    for rows in blocks:
        w_scr[rows, :] = jnp.dot(lower, logf_ref[0, rows, :], **exact)
    totals = w_scr[pl.ds(CS_BLK - 1, nblk, stride=CS_BLK), :]
    rb = lax.broadcasted_iota(jnp.int32, (nblk, nblk), 0)
    cb = lax.broadcasted_iota(jnp.int32, (nblk, nblk), 1)
    carry = jnp.dot((rb > cb).astype(jnp.float32), totals, **exact)
    for b, rows in enumerate(blocks):
        rest = (w_scr[rows, :] + carry[b:b + 1, :]) * (-LOG2E)
        parts = []
        for _ in range(N_SPLIT):
            part = rest.astype(jnp.bfloat16)
            rest = rest - part.astype(jnp.float32)
            parts.append(part)
        a_ref[0, rows, :] = jnp.dot(jnp.concatenate(parts, axis=1), place_ref[...],
                                    preferred_element_type=jnp.float32).astype(jnp.bfloat16)


def _cumsum(logf3):
    b, s, _ = logf3.shape
    place = jnp.asarray(_bias_placement().reshape(N_SPLIT * LANES, D_FOX), jnp.bfloat16)
    return pl.pallas_call(
        _cumsum_kernel,
        grid=(b,),
        in_specs=[pl.BlockSpec((1, s, LANES), lambda i: (i, 0, 0)),
                  pl.BlockSpec((N_SPLIT * LANES, D_FOX), lambda i: (0, 0))],
        out_specs=pl.BlockSpec((1, s, D_FOX), lambda i: (i, 0, 0)),
        out_shape=jax.ShapeDtypeStruct((b, s, D_FOX), jnp.bfloat16),
        scratch_shapes=[pltpu.VMEM((s, LANES), jnp.float32)],
        compiler_params=_params("arbitrary"),
        name="cumsum_logf",
    )(logf3, place)


SWA_TQ = 512
PAIRS_PER_KV = (N_HEADS_SWA // N_KV_SWA) // 2


SWA_R = SWA_TQ // BLOCK
SWA_COLS = PAIRS_PER_KV * BLOCK


def _swa_kernel(sink_ref, q_ref, k_ref, v_ref, o_ref, k_scr, vt_scr, bias_scr, st_scr):
    bi = pl.program_id(0)
    qi = pl.program_id(1)
    s_len = k_ref.shape[1]
    log2_block = int(np.log2(BLOCK))

    @pl.when((bi == 0) & (qi == 0))
    def _():
        key_loc = lax.broadcasted_iota(jnp.int32, (2 * BLOCK, SWA_COLS), 0)
        col = lax.broadcasted_iota(jnp.int32, (2 * BLOCK, SWA_COLS), 1)
        dist = BLOCK + (col & (BLOCK - 1)) - key_loc
        pair = lax.shift_right_logical(col, log2_block)
        distf = dist.astype(jnp.float32)
        in_win = (dist >= 0) & (dist < WINDOW)
        for kk in range(N_KV_SWA):
            for par in range(2):
                slope = jnp.zeros((2 * BLOCK, SWA_COLS), jnp.float32)
                for pi in range(PAIRS_PER_KV):
                    h = kk * 2 * PAIRS_PER_KV + 2 * pi + par
                    slope = jnp.where(pair == pi, ALIBI_SLOPES[h] * LOG2E, slope)
                bias_scr[kk, par] = jnp.where(in_win, -slope * distf, NEG)

    @pl.when(qi == 0)
    def _():
        low = lax.broadcasted_iota(jnp.int32, (SWA_TQ, LANES), 1) < HEAD_DIM
        halves = (low, jnp.logical_not(low))
        zero_k = jnp.zeros((BLOCK, LANES), jnp.bfloat16)
        zero_v = jnp.zeros((LANES, BLOCK), jnp.bfloat16)
        for kk in range(N_KV_SWA):
            for par in range(2):
                k_scr[kk, par, 0:BLOCK, :] = zero_k
                vt_scr[kk, par, :, 0:BLOCK] = zero_v
        for c0 in range(0, s_len, SWA_TQ):
            kc = k_ref[0, c0:c0 + SWA_TQ, :].astype(jnp.float32)
            vc = v_ref[0, c0:c0 + SWA_TQ, :].astype(jnp.float32)
            ksw = pltpu.roll(kc, HEAD_DIM, 1)
            vsw = pltpu.roll(vc, HEAD_DIM, 1)
            rows = slice(BLOCK + c0, BLOCK + c0 + SWA_TQ)
            for kk in range(N_KV_SWA):
                for par in range(2):
                    ksrc, vsrc = (kc, vc) if kk == par else (ksw, vsw)
                    k_scr[kk, par, rows, :] = jnp.where(halves[par], ksrc, 0.0).astype(jnp.bfloat16)
                    vt_scr[kk, par, :, rows] = jnp.where(halves[par], vsrc, 1.0).T.astype(jnp.bfloat16)

    pair1 = lax.shift_right_logical(lax.broadcasted_iota(jnp.int32, (1, SWA_COLS), 1), log2_block)
    pad_rows = lax.broadcasted_iota(jnp.int32, (2 * BLOCK, SWA_COLS), 0) < BLOCK
    first_tile = qi == 0
    sinks2 = {}
    for kk in range(N_KV_SWA):
        for par in range(2):
            sink = jnp.zeros((1, SWA_COLS), jnp.float32)
            for pi in range(PAIRS_PER_KV):
                sink = jnp.where(pair1 == pi, sink_ref[kk * 2 * PAIRS_PER_KV + 2 * pi + par] * LOG2E, sink)
            sinks2[kk, par] = sink

    def band_start(r):
        return pl.multiple_of(qi * SWA_TQ + r * BLOCK, BLOCK)

    def scores(r):
        rows = slice(r * BLOCK, (r + 1) * BLOCK)
        for kk in range(N_KV_SWA):
            qs = jnp.concatenate([q_ref[0, rows, (kk * PAIRS_PER_KV + pi) * LANES:(kk * PAIRS_PER_KV + pi + 1) * LANES]
                                  for pi in range(PAIRS_PER_KV)], axis=0)
            for par in range(2):
                st_scr[r, kk, par] = lax.dot_general(k_scr[kk, par, pl.ds(band_start(r), 2 * BLOCK), :], qs, _NT,
                                                     preferred_element_type=jnp.float32)

    def finish(r):
        rows = slice(r * BLOCK, (r + 1) * BLOCK)
        for kk in range(N_KV_SWA):
            outs = []
            for par in range(2):
                sink = sinks2[kk, par]
                st = st_scr[r, kk, par] + bias_scr[kk, par]
                if r == 0:
                    st = jnp.where(pad_rows & first_tile, NEG, st)
                m = jnp.maximum(jnp.max(st, axis=0, keepdims=True), sink)
                p = jnp.exp2(st - m).astype(jnp.bfloat16)
                o = jnp.dot(vt_scr[kk, par, :, pl.ds(band_start(r), 2 * BLOCK)], p,
                            preferred_element_type=jnp.float32)
                extra = jnp.exp2(sink - m)
                if par == 0:
                    outs.append(o[0:HEAD_DIM] / (o[HEAD_DIM:HEAD_DIM + 1] + extra))
                else:
                    outs.append(o[HEAD_DIM:LANES] / (o[0:1] + extra))
            ot = jnp.concatenate(outs, axis=0).T
            for pi in range(PAIRS_PER_KV):
                c0 = (kk * PAIRS_PER_KV + pi) * LANES
                o_ref[0, rows, c0:c0 + LANES] = ot[pi * BLOCK:(pi + 1) * BLOCK, :]

    scores(0)
    for r in range(SWA_R):
        if r + 1 < SWA_R:
            scores(r + 1)
        finish(r)


def _swa(proj3, sinks):
    b, s, _ = proj3.shape
    return pl.pallas_call(
        _swa_kernel,
        grid=(b, s // SWA_TQ),
        in_specs=[
            pl.BlockSpec(memory_space=pltpu.SMEM),
            pl.BlockSpec((1, SWA_TQ, D_SWA), lambda bi, qi: (bi, qi, COL_QA * LANES // D_SWA)),
            pl.BlockSpec((1, s, LANES), lambda bi, qi: (bi, 0, COL_KA)),
            pl.BlockSpec((1, s, LANES), lambda bi, qi: (bi, 0, COL_VA)),
        ],
        out_specs=pl.BlockSpec((1, SWA_TQ, D_SWA), lambda bi, qi: (bi, qi, 0)),
        scratch_shapes=[
            pltpu.VMEM((N_KV_SWA, 2, s + BLOCK, LANES), jnp.bfloat16),
            pltpu.VMEM((N_KV_SWA, 2, LANES, s + BLOCK), jnp.bfloat16),
            pltpu.VMEM((N_KV_SWA, 2, 2 * BLOCK, SWA_COLS), jnp.float32),
            pltpu.VMEM((SWA_R, N_KV_SWA, 2, 2 * BLOCK, SWA_COLS), jnp.float32),
        ],
        out_shape=jax.ShapeDtypeStruct((b, s, D_SWA), jnp.float32),
        compiler_params=_params("arbitrary", "arbitrary"),
        name="swa_attention",
    )(sinks, proj3, proj3, proj3)


FOX_TQ = 512
FOX_TK = FOX_TQ
FOX_PAIRS = 2
FOX_VROWS = HEAD_DIM + 16
assert all(col % FOX_PAIRS == 0 for col in (COL_QB, COL_KB, COL_VB))


def _fox_kernel(q_ref, k_ref, v_ref, a_ref, o_ref, ka_scr, vat_scr, m_scr, acc_scr, st_scr, cmax_scr):
    qi = pl.program_id(2)
    s_len = k_ref.shape[1]
    low_k = lax.broadcasted_iota(jnp.int32, (FOX_TK, LANES), 1) < HEAD_DIM
    keeps = (low_k, jnp.logical_not(low_k))

    heads = range(2 * FOX_PAIRS)

    @pl.when(qi == 0)
    def _():
        for c0 in range(0, s_len, FOX_TK):
            for pair in range(FOX_PAIRS):
                cols = slice(pair * LANES, (pair + 1) * LANES)
                kc = k_ref[0, c0:c0 + FOX_TK, cols]
                vc = v_ref[0, c0:c0 + FOX_TK, cols].astype(jnp.float32)
                ac = a_ref[0, c0:c0 + FOX_TK, cols]
                for par in range(2):
                    h = 2 * pair + par
                    ka_scr[h, c0:c0 + FOX_TK, :] = jnp.where(keeps[par], kc, ac)
                    vt = jnp.where(keeps[par], vc, 1.0).T.astype(jnp.bfloat16)
                    vat_scr[h, :, c0:c0 + FOX_TK] = vt[0:FOX_VROWS] if par == 0 else vt[LANES - FOX_VROWS:LANES]

    lane_q = lax.broadcasted_iota(jnp.int32, (FOX_TQ, LANES), 1)
    ones_hi = jnp.where((lane_q >= HEAD_DIM) & (lane_q < HEAD_DIM + N_SPLIT), 1.0, 0.0).astype(jnp.bfloat16)
    ones_lo = jnp.where(lane_q < N_SPLIT, 1.0, 0.0).astype(jnp.bfloat16)
    qa = []
    for pair in range(FOX_PAIRS):
        q = q_ref[0, :, pair * LANES:(pair + 1) * LANES]
        qa += [jnp.where(lane_q < HEAD_DIM, q, ones_hi), jnp.where(lane_q < HEAD_DIM, ones_lo, q)]
    key_row = lax.broadcasted_iota(jnp.int32, (FOX_TK, FOX_TQ), 0)
    qry_col = lax.broadcasted_iota(jnp.int32, (FOX_TK, FOX_TQ), 1)

    m_scr[...] = jnp.full(m_scr.shape, NEG, jnp.float32)
    acc_scr[...] = jnp.zeros(acc_scr.shape, jnp.float32)

    def scores(j, slot):
        k0 = pl.multiple_of(j * FOX_TK, FOX_TK)
        for h in heads:
            st = lax.dot_general(ka_scr[h, pl.ds(k0, FOX_TK), :], qa[h], _NT,
                                 preferred_element_type=jnp.float32)
            st_scr[slot, h] = st
            cmax_scr[slot, h] = jnp.max(st, axis=0, keepdims=True)

    def update(j, slot, diagonal):
        k0 = pl.multiple_of(j * FOX_TK, FOX_TK)
        for h in heads:
            st = st_scr[slot, h]
            if diagonal:
                st = jnp.where(key_row <= qry_col, st, NEG)
                cmax = jnp.max(st, axis=0, keepdims=True)
            else:
                cmax = cmax_scr[slot, h]
            m_old = m_scr[h]
            m_new = jnp.maximum(m_old, cmax)
            alpha = jnp.exp2(m_old - m_new)
            pt = jnp.exp2(st - m_new).astype(jnp.bfloat16)
            acc_scr[h] = alpha * acc_scr[h] + jnp.dot(vat_scr[h, :, pl.ds(k0, FOX_TK)], pt,
                                                      preferred_element_type=jnp.float32)
            m_scr[h] = m_new

    scores(0, 0)

    def body(jj, _):
        j = 2 * jj
        scores(j + 1, 1)
        update(j, 0, False)
        scores(j + 2, 0)
        update(j + 1, 1, False)
        return 0

    lax.fori_loop(0, qi // 2, body, 0)

    @pl.when(qi % 2 == 1)
    def _():
        scores(qi, 1)
        update(qi - 1, 0, False)
        update(qi, 1, True)

    @pl.when(qi % 2 == 0)
    def _():
        update(qi, 0, True)

    for pair in range(FOX_PAIRS):
        acc0 = acc_scr[2 * pair]
        acc1 = acc_scr[2 * pair + 1]
        ot = jnp.concatenate([acc0[0:HEAD_DIM] / acc0[HEAD_DIM:HEAD_DIM + 1],
                              acc1[FOX_VROWS - HEAD_DIM:FOX_VROWS] / acc1[0:1]], axis=0)
        o_ref[0, :, pair * LANES:(pair + 1) * LANES] = ot.T


def _fox(proj3, aug):
    b, s, _ = proj3.shape
    width = FOX_PAIRS * LANES
    ngroups = D_FOX // width
    nheads = 2 * FOX_PAIRS
    return pl.pallas_call(
        _fox_kernel,
        grid=(b, ngroups, s // FOX_TQ),
        in_specs=[
            pl.BlockSpec((1, FOX_TQ, width), lambda bi, p, qi: (bi, qi, COL_QB // FOX_PAIRS + p)),
            pl.BlockSpec((1, s, width), lambda bi, p, qi: (bi, 0, COL_KB // FOX_PAIRS + p)),
            pl.BlockSpec((1, s, width), lambda bi, p, qi: (bi, 0, COL_VB // FOX_PAIRS + p)),
            pl.BlockSpec((1, s, width), lambda bi, p, qi: (bi, 0, p)),
        ],
        out_specs=pl.BlockSpec((1, FOX_TQ, width), lambda bi, p, qi: (bi, qi, p)),
        out_shape=jax.ShapeDtypeStruct((b, s, D_FOX), jnp.float32),
        scratch_shapes=[
            pltpu.VMEM((nheads, s, LANES), jnp.bfloat16),
            pltpu.VMEM((nheads, FOX_VROWS, s), jnp.bfloat16),
            pltpu.VMEM((nheads, 1, FOX_TQ), jnp.float32),
            pltpu.VMEM((nheads, FOX_VROWS, FOX_TQ), jnp.float32),
            pltpu.VMEM((2, nheads, FOX_TK, FOX_TQ), jnp.float32),
            pltpu.VMEM((2, nheads, 1, FOX_TQ), jnp.float32),
        ],
        compiler_params=_params("arbitrary", "arbitrary", "arbitrary"),
        name="fox_attention",
    )(proj3, proj3, proj3, aug)


OUT_TM = 512


def _out_proj_kernel(oa_ref, ob_ref, x_ref, ga_ref, gb_ref, w_ref, gp_ref, x1_ref):
    na = _rms(oa_ref[...], ga_ref[...]).astype(jnp.bfloat16)
    nb = _rms(ob_ref[...], gb_ref[...]).astype(jnp.bfloat16)
    mix = (jnp.dot(na, w_ref[0:D_SWA, :], preferred_element_type=jnp.float32)
           + jnp.dot(nb, w_ref[D_SWA:D_MIX, :], preferred_element_type=jnp.float32))
    x1_ref[...] = x_ref[...] + _rms(mix, gp_ref[...])


def _out_proj(oa, ob, x2d, ga, gb, w_out, gp):
    m = x2d.shape[0]
    row = lambda i: (i, 0)
    fixed = lambda i: (0, 0)
    return pl.pallas_call(
        _out_proj_kernel,
        grid=(m // OUT_TM,),
        in_specs=[
            pl.BlockSpec((OUT_TM, D_SWA), row),
            pl.BlockSpec((OUT_TM, D_FOX), row),
            pl.BlockSpec((OUT_TM, D_MODEL), row),
            pl.BlockSpec((1, D_SWA), fixed),
            pl.BlockSpec((1, D_FOX), fixed),
            pl.BlockSpec((D_MIX, D_MODEL), fixed, pipeline_mode=pl.Buffered(1)),
            pl.BlockSpec((1, D_MODEL), fixed),
        ],
        out_specs=pl.BlockSpec((OUT_TM, D_MODEL), row),
        out_shape=jax.ShapeDtypeStruct((m, D_MODEL), jnp.float32),
        compiler_params=_params("arbitrary"),
        name="out_proj",
    )(oa, ob, x2d, ga, gb, w_out, gp)


FFN_TM = 512
FFN_TF = 512
CARRY = 8
ROW_CHUNK = 32
N_FT = D_FF // FFN_TF


def _gelu_tanh(x):
    return 0.5 * x * (1.0 + jnp.tanh(np.sqrt(2.0 / np.pi) * (x + 0.044715 * (x * x * x))))


def _ffn_kernel(x_ref, g_ref, wg_ref, wv_ref, cwg_ref, cwv_ref, cbg_ref, cbv_ref, wd_ref, gp_ref,
                o_ref, h_scr, ug_scr, uv_scr, tail_scr, *, tiles_per_seq):
    i = pl.program_id(0)
    j = pl.program_id(1)

    @pl.when(j == 0)
    def _():
        h_scr[...] = _rms(x_ref[...], g_ref[...]).astype(jnp.bfloat16)
        o_ref[...] = jnp.zeros_like(o_ref)

    @pl.when((i == 0) & (j == 0))
    def _():
        tail_scr[...] = jnp.zeros_like(tail_scr)

    h = h_scr[...]
    seq_start = i % tiles_per_seq == 0
    for k, (u_scr, w_ref) in enumerate(((ug_scr, wg_ref), (uv_scr, wv_ref))):
        u_scr[0:CARRY, :] = jnp.where(seq_start, 0.0, tail_scr[j, k])
        u_scr[CARRY:, :] = jnp.dot(h, w_ref[...], preferred_element_type=jnp.float32)
        tail_scr[j, k] = u_scr[FFN_TM:FFN_TM + CARRY, :]

    def conv(u_scr, cw_ref, cb_ref):
        y = cb_ref[...]
        for kk in range(CONV_WIDTH):
            y = y + u_scr[pl.ds(CARRY - (CONV_WIDTH - 1) + kk, FFN_TM), :] * cw_ref[kk:kk + 1, :]
        return y

    gate = conv(ug_scr, cwg_ref, cbg_ref)
    val = conv(uv_scr, cwv_ref, cbv_ref)
    a = (_gelu_tanh(gate) * val).astype(jnp.bfloat16)
    o_ref[...] += jnp.dot(a, wd_ref[...], preferred_element_type=jnp.float32)

    @pl.when(j == N_FT - 1)
    def _():
        gp = gp_ref[...]
        for r0 in range(0, FFN_TM, ROW_CHUNK):
            rows = slice(r0, r0 + ROW_CHUNK)
            o_ref[rows, :] = x_ref[rows, :] + _rms(o_ref[rows, :], gp)


def _ffn(x1, g, w_up, conv_w, conv_b, w_down, gp, seq_len):
    m = x1.shape[0]
    kern = functools.partial(_ffn_kernel, tiles_per_seq=seq_len // FFN_TM)
    return pl.pallas_call(
        kern,
        grid=(m // FFN_TM, N_FT),
        in_specs=[
            pl.BlockSpec((FFN_TM, D_MODEL), lambda i, j: (i, 0)),
            pl.BlockSpec((1, D_MODEL), lambda i, j: (0, 0)),
            pl.BlockSpec((D_MODEL, FFN_TF), lambda i, j: (0, j)),
            pl.BlockSpec((D_MODEL, FFN_TF), lambda i, j: (0, j + N_FT)),
            pl.BlockSpec((CONV_WIDTH, FFN_TF), lambda i, j: (0, j)),
            pl.BlockSpec((CONV_WIDTH, FFN_TF), lambda i, j: (0, j + N_FT)),
            pl.BlockSpec((1, FFN_TF), lambda i, j: (0, j)),
            pl.BlockSpec((1, FFN_TF), lambda i, j: (0, j + N_FT)),
            pl.BlockSpec((FFN_TF, D_MODEL), lambda i, j: (j, 0)),
            pl.BlockSpec((1, D_MODEL), lambda i, j: (0, 0)),
        ],
        out_specs=pl.BlockSpec((FFN_TM, D_MODEL), lambda i, j: (i, 0)),
        out_shape=jax.ShapeDtypeStruct((m, D_MODEL), jnp.float32),
        scratch_shapes=[
            pltpu.VMEM((FFN_TM, D_MODEL), jnp.bfloat16),
            pltpu.VMEM((FFN_TM + CARRY, FFN_TF), jnp.float32),
            pltpu.VMEM((FFN_TM + CARRY, FFN_TF), jnp.float32),
            pltpu.VMEM((N_FT, 2, CARRY, FFN_TF), jnp.float32),
        ],
        compiler_params=_params("arbitrary", "arbitrary"),
        name="conv_geglu_ffn",
    )(x1, g, w_up, w_up, conv_w, conv_w, conv_b, conv_b, w_down, gp)


def kernel(x, pre_mix_g, w_in, b_forget, sinks, grp_swa_g, grp_fox_g, w_out, post_mix_g,
           pre_ffn_g, w_up, conv_w, conv_b, w_down, post_ffn_g):
    b, s, d = x.shape
    depth = w_in.shape[0]
    xf = x.reshape(b * s, d)
    for l in range(depth):
        w_in_bf = _cast_pad_rows(jnp.swapaxes(w_in[l], 0, 1), D_IN_PAD)
        b_f = jnp.pad(b_forget[l], (0, LANES - N_HEADS_FOX)).reshape(1, LANES)

        proj, logf, w_out_bf, w_up_bf, w_down_bf = _in_proj(xf, pre_mix_g[l].reshape(1, d), w_in_bf, b_f,
                                                            (w_out[l], w_up[l], w_down[l]))
        proj3 = proj.reshape(b, s, D_QKV)
        aug = _cumsum(logf.reshape(b, s, LANES))

        o_a = _swa(proj3, sinks[l])
        o_b = _fox(proj3, aug)

        x1 = _out_proj(o_a.reshape(b * s, D_SWA), o_b.reshape(b * s, D_FOX), xf,
                       grp_swa_g[l].reshape(1, D_SWA), grp_fox_g[l].reshape(1, D_FOX),
                       w_out_bf, post_mix_g[l].reshape(1, d))
        xf = _ffn(x1, pre_ffn_g[l].reshape(1, d), w_up_bf, conv_w[l],
                  conv_b[l].reshape(1, 2 * D_FF), w_down_bf, post_ffn_g[l].reshape(1, d), s)
    return xf.reshape(b, s, d)
```

```python
import functools

import numpy as np
import jax
import jax.numpy as jnp
from jax import lax
from jax.experimental import pallas as pl
from jax.experimental.pallas import tpu as pltpu

D_MODEL = 2048
HEAD_DIM = 64
N_HEADS_SWA = 16
N_KV_SWA = 2
N_HEADS_FOX = 16
WINDOW = 128
BLOCK = 128
D_FF = 5632
CONV_WIDTH = 3
EPS = 1e-6
D_SWA = N_HEADS_SWA * HEAD_DIM
D_KV_SWA = N_KV_SWA * HEAD_DIM
D_FOX = N_HEADS_FOX * HEAD_DIM
D_MIX = D_SWA + D_FOX
D_QKV = D_SWA + 2 * D_KV_SWA + 3 * D_FOX

LANES = 128
COL_QA = 0
COL_KA = COL_QA + D_SWA // LANES
COL_VA = COL_KA + 1
COL_QB = COL_VA + 1
COL_KB = COL_QB + D_FOX // LANES
COL_VB = COL_KB + D_FOX // LANES
D_IN_PAD = D_QKV + LANES

NEG = -1e30
VMEM_LIMIT = 56 * 1024 * 1024

ALIBI_SLOPES = [float(v) for v in np.asarray(2.0 ** (-8.0 * np.arange(1, N_HEADS_SWA + 1) / N_HEADS_SWA),
                                            dtype=np.float32)]

_NT = (((1,), (1,)), ((), ()))


def _rms(xf, g):
    return xf * lax.rsqrt(jnp.mean(xf * xf, axis=-1, keepdims=True) + EPS) * g


def _params(*sem):
    return pltpu.CompilerParams(dimension_semantics=sem, vmem_limit_bytes=VMEM_LIMIT)


CAST_TM = 5 * LANES


def _cast_pad_kernel(w_ref, o_ref, *, valid_rows):
    row = pl.program_id(0) * CAST_TM + lax.broadcasted_iota(jnp.int32, w_ref.shape, 0)
    o_ref[...] = jnp.where(row < valid_rows, w_ref[...], 0.0).astype(jnp.bfloat16)


def _cast_pad_rows(w, padded_rows):
    rows, cols = w.shape
    assert padded_rows % CAST_TM == 0 and padded_rows - rows < CAST_TM
    return pl.pallas_call(
        functools.partial(_cast_pad_kernel, valid_rows=rows),
        grid=(padded_rows // CAST_TM,),
        in_specs=[pl.BlockSpec((CAST_TM, cols), lambda j: (j, 0))],
        out_specs=pl.BlockSpec((CAST_TM, cols), lambda j: (j, 0)),
        out_shape=jax.ShapeDtypeStruct((padded_rows, cols), jnp.bfloat16),
        compiler_params=_params("arbitrary"),
        name="cast_w_in",
    )(w)


IN_TM = 512
IN_CHUNK = 1024
LOG2E = float(np.log2(np.e))
Q_MULT = LOG2E / float(np.sqrt(HEAD_DIM))


Q_RANGES = ((COL_QA * LANES, COL_QA * LANES + D_SWA), (COL_QB * LANES, COL_QB * LANES + D_FOX))


def _in_proj_kernel(x_ref, g_ref, w_ref, wf_ref, bf_ref, *refs):
    n_side = (len(refs) - 2) // 2
    side_in, (proj_ref, logf_ref), side_out = refs[:n_side], refs[n_side:n_side + 2], refs[n_side + 2:]
    for src, dst in zip(side_in, side_out):
        dst[...] = src[...].astype(jnp.bfloat16)
    h = _rms(x_ref[...], g_ref[...]).astype(jnp.bfloat16)
    for n0 in range(0, D_QKV, IN_CHUNK):
        n1 = min(n0 + IN_CHUNK, D_QKV)
        acc = lax.dot_general(h, w_ref[n0:n1, :], _NT, preferred_element_type=jnp.float32)
        overlaps = [(max(lo, n0), min(hi, n1)) for lo, hi in Q_RANGES if max(lo, n0) < min(hi, n1)]
        if overlaps:
            col = n0 + lax.broadcasted_iota(jnp.int32, (1, n1 - n0), 1)
            mult = jnp.ones((1, n1 - n0), jnp.float32)
            for lo, hi in overlaps:
                mult = jnp.where((col >= lo) & (col < hi), Q_MULT, mult)
            acc = acc * mult
        proj_ref[:, n0:n1] = acc.astype(jnp.bfloat16)
    f = lax.dot_general(h, wf_ref[...], _NT, preferred_element_type=jnp.float32) + bf_ref[...]
    logf_ref[...] = jnp.minimum(f, 0.0) - jnp.log1p(jnp.exp(-jnp.abs(f)))


def _in_proj(x2d, g, w_in_bf, b_f, side_weights):
    m = x2d.shape[0]
    steps = m // IN_TM
    side_specs = []
    for w in side_weights:
        rows, cols = w.shape
        assert rows % (steps * 16) == 0
        side_specs.append(pl.BlockSpec((rows // steps, cols), lambda i: (i, 0)))
    return pl.pallas_call(
        _in_proj_kernel,
        grid=(steps,),
        in_specs=[
            pl.BlockSpec((IN_TM, D_MODEL), lambda i: (i, 0)),
            pl.BlockSpec((1, D_MODEL), lambda i: (0, 0)),
            pl.BlockSpec((D_QKV, D_MODEL), lambda i: (0, 0), pipeline_mode=pl.Buffered(1)),
            pl.BlockSpec((LANES, D_MODEL), lambda i: (D_QKV // LANES, 0), pipeline_mode=pl.Buffered(1)),
            pl.BlockSpec((1, LANES), lambda i: (0, 0)),
        ] + side_specs,
        out_specs=[
            pl.BlockSpec((IN_TM, D_QKV), lambda i: (i, 0)),
            pl.BlockSpec((IN_TM, LANES), lambda i: (i, 0)),
        ] + side_specs,
        out_shape=[
            jax.ShapeDtypeStruct((m, D_QKV), jnp.bfloat16),
            jax.ShapeDtypeStruct((m, LANES), jnp.float32),
        ] + [jax.ShapeDtypeStruct(w.shape, jnp.bfloat16) for w in side_weights],
        compiler_params=_params("arbitrary"),
        name="in_proj",
    )(x2d, g, w_in_bf, w_in_bf, b_f, *side_weights)


CS_BLK = 128
N_SPLIT = 3


def _bias_placement():
    place = np.zeros((N_SPLIT, LANES, D_FOX), np.float32)
    for h in range(N_HEADS_FOX):
        base = (h // 2) * LANES + (HEAD_DIM if h % 2 == 0 else 0)
        for i in range(N_SPLIT):
            place[i, h, base + i] = 1.0
    return place


def _cumsum_kernel(logf_ref, place_ref, a_ref, w_scr):
    s = logf_ref.shape[1]
    nblk = s // CS_BLK
    exact = dict(precision=lax.Precision.HIGHEST, preferred_element_type=jnp.float32)
    r = lax.broadcasted_iota(jnp.int32, (CS_BLK, CS_BLK), 0)
    c = lax.broadcasted_iota(jnp.int32, (CS_BLK, CS_BLK), 1)
    lower = (r >= c).astype(jnp.float32)
    blocks = [slice(b * CS_BLK, (b + 1) * CS_BLK) for b in range(nblk)]
    for rows in blocks:
        w_scr[rows, :] = jnp.dot(lower, logf_ref[0, rows, :], **exact)
    totals = w_scr[pl.ds(CS_BLK - 1, nblk, stride=CS_BLK), :]
    rb = lax.broadcasted_iota(jnp.int32, (nblk, nblk), 0)
    cb = lax.broadcasted_iota(jnp.int32, (nblk, nblk), 1)
    carry = jnp.dot((rb > cb).astype(jnp.float32), totals, **exact)
    for b, rows in enumerate(blocks):
        rest = (w_scr[rows, :] + carry[b:b + 1, :]) * (-LOG2E)
        parts = []
        for _ in range(N_SPLIT):
            part = rest.astype(jnp.bfloat16)
            rest = rest - part.astype(jnp.float32)
            parts.append(part)
        a_ref[0, rows, :] = jnp.dot(jnp.concatenate(parts, axis=1), place_ref[...],
                                    preferred_element_type=jnp.float32).astype(jnp.bfloat16)


def _cumsum(logf3):
    b, s, _ = logf3.shape
    place = jnp.asarray(_bias_placement().reshape(N_SPLIT * LANES, D_FOX), jnp.bfloat16)
    return pl.pallas_call(
        _cumsum_kernel,
        grid=(b,),
        in_specs=[pl.BlockSpec((1, s, LANES), lambda i: (i, 0, 0)),
                  pl.BlockSpec((N_SPLIT * LANES, D_FOX), lambda i: (0, 0))],
        out_specs=pl.BlockSpec((1, s, D_FOX), lambda i: (i, 0, 0)),
        out_shape=jax.ShapeDtypeStruct((b, s, D_FOX), jnp.bfloat16),
        scratch_shapes=[pltpu.VMEM((s, LANES), jnp.float32)],
        compiler_params=_params("arbitrary"),
        name="cumsum_logf",
    )(logf3, place)


SWA_TQ = 512
PAIRS_PER_KV = (N_HEADS_SWA // N_KV_SWA) // 2


SWA_R = SWA_TQ // BLOCK
SWA_COLS = PAIRS_PER_KV * BLOCK


def _swa_kernel(sink_ref, q_ref, k_ref, v_ref, o_ref, k_scr, vt_scr, bias_scr, st_scr):
    bi = pl.program_id(0)
    qi = pl.program_id(1)
    s_len = k_ref.shape[1]
    log2_block = int(np.log2(BLOCK))

    @pl.when((bi == 0) & (qi == 0))
    def _():
        key_loc = lax.broadcasted_iota(jnp.int32, (2 * BLOCK, SWA_COLS), 0)
        col = lax.broadcasted_iota(jnp.int32, (2 * BLOCK, SWA_COLS), 1)
        dist = BLOCK + (col & (BLOCK - 1)) - key_loc
        pair = lax.shift_right_logical(col, log2_block)
        distf = dist.astype(jnp.float32)
        in_win = (dist >= 0) & (dist < WINDOW)
        for kk in range(N_KV_SWA):
            for par in range(2):
                slope = jnp.zeros((2 * BLOCK, SWA_COLS), jnp.float32)
                for pi in range(PAIRS_PER_KV):
                    h = kk * 2 * PAIRS_PER_KV + 2 * pi + par
                    slope = jnp.where(pair == pi, ALIBI_SLOPES[h] * LOG2E, slope)
                bias_scr[kk, par] = jnp.where(in_win, -slope * distf, NEG)

    @pl.when(qi == 0)
    def _():
        low = lax.broadcasted_iota(jnp.int32, (SWA_TQ, LANES), 1) < HEAD_DIM
        halves = (low, jnp.logical_not(low))
        zero_k = jnp.zeros((BLOCK, LANES), jnp.bfloat16)
        zero_v = jnp.zeros((LANES, BLOCK), jnp.bfloat16)
        for kk in range(N_KV_SWA):
            for par in range(2):
                k_scr[kk, par, 0:BLOCK, :] = zero_k
                vt_scr[kk, par, :, 0:BLOCK] = zero_v
        for c0 in range(0, s_len, SWA_TQ):
            kc = k_ref[0, c0:c0 + SWA_TQ, :].astype(jnp.float32)
            vc = v_ref[0, c0:c0 + SWA_TQ, :].astype(jnp.float32)
            ksw = pltpu.roll(kc, HEAD_DIM, 1)
            vsw = pltpu.roll(vc, HEAD_DIM, 1)
            rows = slice(BLOCK + c0, BLOCK + c0 + SWA_TQ)
            for kk in range(N_KV_SWA):
                for par in range(2):
                    ksrc, vsrc = (kc, vc) if kk == par else (ksw, vsw)
                    k_scr[kk, par, rows, :] = jnp.where(halves[par], ksrc, 0.0).astype(jnp.bfloat16)
                    vt_scr[kk, par, :, rows] = jnp.where(halves[par], vsrc, 1.0).T.astype(jnp.bfloat16)

    pair1 = lax.shift_right_logical(lax.broadcasted_iota(jnp.int32, (1, SWA_COLS), 1), log2_block)
    pad_rows = lax.broadcasted_iota(jnp.int32, (2 * BLOCK, SWA_COLS), 0) < BLOCK
    first_tile = qi == 0
    sinks2 = {}
    for kk in range(N_KV_SWA):
        for par in range(2):
            sink = jnp.zeros((1, SWA_COLS), jnp.float32)
            for pi in range(PAIRS_PER_KV):
                sink = jnp.where(pair1 == pi, sink_ref[kk * 2 * PAIRS_PER_KV + 2 * pi + par] * LOG2E, sink)
            sinks2[kk, par] = sink

    def band_start(r):
        return pl.multiple_of(qi * SWA_TQ + r * BLOCK, BLOCK)

    def scores(r):
        rows = slice(r * BLOCK, (r + 1) * BLOCK)
        for kk in range(N_KV_SWA):
            qs = jnp.concatenate([q_ref[0, rows, (kk * PAIRS_PER_KV + pi) * LANES:(kk * PAIRS_PER_KV + pi + 1) * LANES]
                                  for pi in range(PAIRS_PER_KV)], axis=0)
            for par in range(2):
                st_scr[r, kk, par] = lax.dot_general(k_scr[kk, par, pl.ds(band_start(r), 2 * BLOCK), :], qs, _NT,
                                                     preferred_element_type=jnp.float32)

    def finish(r):
        rows = slice(r * BLOCK, (r + 1) * BLOCK)
        for kk in range(N_KV_SWA):
            outs = []
            for par in range(2):
                sink = sinks2[kk, par]
                st = st_scr[r, kk, par] + bias_scr[kk, par]
                if r == 0:
                    st = jnp.where(pad_rows & first_tile, NEG, st)
                m = jnp.maximum(jnp.max(st, axis=0, keepdims=True), sink)
                p = jnp.exp2(st - m).astype(jnp.bfloat16)
                o = jnp.dot(vt_scr[kk, par, :, pl.ds(band_start(r), 2 * BLOCK)], p,
                            preferred_element_type=jnp.float32)
                extra = jnp.exp2(sink - m)
                if par == 0:
                    outs.append(o[0:HEAD_DIM] / (o[HEAD_DIM:HEAD_DIM + 1] + extra))
                else:
                    outs.append(o[HEAD_DIM:LANES] / (o[0:1] + extra))
            ot = jnp.concatenate(outs, axis=0).T
            for pi in range(PAIRS_PER_KV):
                c0 = (kk * PAIRS_PER_KV + pi) * LANES
                o_ref[0, rows, c0:c0 + LANES] = ot[pi * BLOCK:(pi + 1) * BLOCK, :]

    scores(0)
    for r in range(SWA_R):
        if r + 1 < SWA_R:
            scores(r + 1)
        finish(r)


def _swa(proj3, sinks):
    b, s, _ = proj3.shape
    return pl.pallas_call(
        _swa_kernel,
        grid=(b, s // SWA_TQ),
        in_specs=[
            pl.BlockSpec(memory_space=pltpu.SMEM),
            pl.BlockSpec((1, SWA_TQ, D_SWA), lambda bi, qi: (bi, qi, COL_QA * LANES // D_SWA)),
            pl.BlockSpec((1, s, LANES), lambda bi, qi: (bi, 0, COL_KA)),
            pl.BlockSpec((1, s, LANES), lambda bi, qi: (bi, 0, COL_VA)),
        ],
        out_specs=pl.BlockSpec((1, SWA_TQ, D_SWA), lambda bi, qi: (bi, qi, 0)),
        scratch_shapes=[
            pltpu.VMEM((N_KV_SWA, 2, s + BLOCK, LANES), jnp.bfloat16),
            pltpu.VMEM((N_KV_SWA, 2, LANES, s + BLOCK), jnp.bfloat16),
            pltpu.VMEM((N_KV_SWA, 2, 2 * BLOCK, SWA_COLS), jnp.float32),
            pltpu.VMEM((SWA_R, N_KV_SWA, 2, 2 * BLOCK, SWA_COLS), jnp.float32),
        ],
        out_shape=jax.ShapeDtypeStruct((b, s, D_SWA), jnp.float32),
        compiler_params=_params("arbitrary", "arbitrary"),
        name="swa_attention",
    )(sinks, proj3, proj3, proj3)


FOX_TQ = 512
FOX_TK = FOX_TQ
FOX_PAIRS = 2
FOX_VROWS = HEAD_DIM + 16
assert all(col % FOX_PAIRS == 0 for col in (COL_QB, COL_KB, COL_VB))


def _fox_kernel(q_ref, qn_ref, k_ref, v_ref, a_ref, o_ref, ka_scr, vat_scr, qa_scr, m_scr, acc_scr, st_scr, cmax_scr):
    qi = pl.program_id(2)
    s_len = k_ref.shape[1]
    n_tiles = s_len // FOX_TQ
    half = FOX_TQ // 2
    low_k = lax.broadcasted_iota(jnp.int32, (FOX_TK, LANES), 1) < HEAD_DIM
    keeps = (low_k, jnp.logical_not(low_k))
    heads = range(2 * FOX_PAIRS)

    lane_q = lax.broadcasted_iota(jnp.int32, (FOX_TQ, LANES), 1)
    ones_hi = jnp.where((lane_q >= HEAD_DIM) & (lane_q < HEAD_DIM + N_SPLIT), 1.0, 0.0).astype(jnp.bfloat16)
    ones_lo = jnp.where(lane_q < N_SPLIT, 1.0, 0.0).astype(jnp.bfloat16)

    def build_queries(ref):
        for pair in range(FOX_PAIRS):
            q = ref[0, :, pair * LANES:(pair + 1) * LANES]
            qa_scr[2 * pair] = jnp.where(lane_q < HEAD_DIM, q, ones_hi)
            qa_scr[2 * pair + 1] = jnp.where(lane_q < HEAD_DIM, ones_lo, q)

    @pl.when(qi == 0)
    def _():
        for c0 in range(0, s_len, FOX_TK):
            for pair in range(FOX_PAIRS):
                cols = slice(pair * LANES, (pair + 1) * LANES)
                kc = k_ref[0, c0:c0 + FOX_TK, cols]
                vc = v_ref[0, c0:c0 + FOX_TK, cols].astype(jnp.float32)
                ac = a_ref[0, c0:c0 + FOX_TK, cols]
                for par in range(2):
                    h = 2 * pair + par
                    ka_scr[h, c0:c0 + FOX_TK, :] = jnp.where(keeps[par], kc, ac)
                    vt = jnp.where(keeps[par], vc, 1.0).T.astype(jnp.bfloat16)
                    vat_scr[h, :, c0:c0 + FOX_TK] = vt[0:FOX_VROWS] if par == 0 else vt[LANES - FOX_VROWS:LANES]
        build_queries(q_ref)

    tri = (lax.broadcasted_iota(jnp.int32, (half, half), 0)
           <= lax.broadcasted_iota(jnp.int32, (half, half), 1))

    m_scr[...] = jnp.full(m_scr.shape, NEG, jnp.float32)
    acc_scr[...] = jnp.zeros(acc_scr.shape, jnp.float32)

    def scores(c, slot, diagonal):
        keys = slice(c * FOX_TK, (c + 1) * FOX_TK)
        for h in heads:
            if diagonal:
                lo_keys = slice(c * FOX_TK, c * FOX_TK + half)
                st_scr[slot, h, 0:half, 0:half] = lax.dot_general(ka_scr[h, lo_keys, :], qa_scr[h, 0:half, :], _NT,
                                                                  preferred_element_type=jnp.float32)
                st_scr[slot, h, :, half:FOX_TQ] = lax.dot_general(ka_scr[h, keys, :], qa_scr[h, half:FOX_TQ, :], _NT,
                                                                  preferred_element_type=jnp.float32)
            else:
                st = lax.dot_general(ka_scr[h, keys, :], qa_scr[h], _NT,
                                     preferred_element_type=jnp.float32)
                st_scr[slot, h] = st
                cmax_scr[slot, h] = jnp.max(st, axis=0, keepdims=True)

    def accumulate(h, st, cmax, keys, cols):
        m_old = m_scr[h, :, cols]
        m_new = jnp.maximum(m_old, cmax)
        alpha = jnp.exp2(m_old - m_new)
        pt = jnp.exp2(st - m_new).astype(jnp.bfloat16)
        acc_scr[h, :, cols] = alpha * acc_scr[h, :, cols] + jnp.dot(vat_scr[h, :, keys], pt,
                                                                  preferred_element_type=jnp.float32)
        m_scr[h, :, cols] = m_new

    def update(c, slot, diagonal):
        k0 = c * FOX_TK
        for h in heads:
            if diagonal:
                st_lo = jnp.where(tri, st_scr[slot, h, 0:half, 0:half], NEG)
                accumulate(h, st_lo, jnp.max(st_lo, axis=0, keepdims=True), slice(k0, k0 + half), slice(0, half))
                st_hi = jnp.concatenate([st_scr[slot, h, 0:half, half:FOX_TQ],
                                         jnp.where(tri, st_scr[slot, h, half:FOX_TK, half:FOX_TQ], NEG)], axis=0)
                accumulate(h, st_hi, jnp.max(st_hi, axis=0, keepdims=True), slice(k0, k0 + FOX_TK),
                           slice(half, FOX_TQ))
            else:
                accumulate(h, st_scr[slot, h], cmax_scr[slot, h], slice(k0, k0 + FOX_TK), slice(0, FOX_TQ))

    def tile(t):
        flip = (t * (t + 1) // 2) % 2
        slot = lambda c: (c + flip) % 2
        if t == 0:
            scores(0, slot(0), True)
        for c in range(t + 1):
            if c < t:
                scores(c + 1, slot(c + 1), c + 1 == t)
            elif t + 1 < n_tiles:
                build_queries(qn_ref)
                scores(0, slot(t + 1), False)
            update(c, slot(c), c == t)

    for t in range(n_tiles):
        pl.when(qi == t)(functools.partial(tile, t))

    for pair in range(FOX_PAIRS):
        acc0 = acc_scr[2 * pair]
        acc1 = acc_scr[2 * pair + 1]
        ot = jnp.concatenate([acc0[0:HEAD_DIM] / acc0[HEAD_DIM:HEAD_DIM + 1],
                              acc1[FOX_VROWS - HEAD_DIM:FOX_VROWS] / acc1[0:1]], axis=0)
        o_ref[0, :, pair * LANES:(pair + 1) * LANES] = ot.T


def _fox(proj3, aug):
    b, s, _ = proj3.shape
    width = FOX_PAIRS * LANES
    ngroups = D_FOX // width
    nheads = 2 * FOX_PAIRS
    n_tiles = s // FOX_TQ
    return pl.pallas_call(
        _fox_kernel,
        grid=(b, ngroups, n_tiles),
        in_specs=[
            pl.BlockSpec((1, FOX_TQ, width), lambda bi, p, qi: (bi, qi, COL_QB // FOX_PAIRS + p)),
            pl.BlockSpec((1, FOX_TQ, width),
                         lambda bi, p, qi: (bi, jnp.minimum(qi + 1, n_tiles - 1), COL_QB // FOX_PAIRS + p)),
            pl.BlockSpec((1, s, width), lambda bi, p, qi: (bi, 0, COL_KB // FOX_PAIRS + p)),
            pl.BlockSpec((1, s, width), lambda bi, p, qi: (bi, 0, COL_VB // FOX_PAIRS + p)),
            pl.BlockSpec((1, s, width), lambda bi, p, qi: (bi, 0, p)),
        ],
        out_specs=pl.BlockSpec((1, FOX_TQ, width), lambda bi, p, qi: (bi, qi, p)),
        out_shape=jax.ShapeDtypeStruct((b, s, D_FOX), jnp.float32),
        scratch_shapes=[
            pltpu.VMEM((nheads, s, LANES), jnp.bfloat16),
            pltpu.VMEM((nheads, FOX_VROWS, s), jnp.bfloat16),
            pltpu.VMEM((nheads, FOX_TQ, LANES), jnp.bfloat16),
            pltpu.VMEM((nheads, 1, FOX_TQ), jnp.float32),
            pltpu.VMEM((nheads, FOX_VROWS, FOX_TQ), jnp.float32),
            pltpu.VMEM((2, nheads, FOX_TK, FOX_TQ), jnp.float32),
            pltpu.VMEM((2, nheads, 1, FOX_TQ), jnp.float32),
        ],
        compiler_params=_params("arbitrary", "arbitrary", "arbitrary"),
        name="fox_attention",
    )(proj3, proj3, proj3, proj3, aug)


OUT_TM = 512


def _out_proj_kernel(oa_ref, ob_ref, x_ref, ga_ref, gb_ref, w_ref, gp_ref, x1_ref):
    na = _rms(oa_ref[...], ga_ref[...]).astype(jnp.bfloat16)
    nb = _rms(ob_ref[...], gb_ref[...]).astype(jnp.bfloat16)
    mix = (jnp.dot(na, w_ref[0:D_SWA, :], preferred_element_type=jnp.float32)
           + jnp.dot(nb, w_ref[D_SWA:D_MIX, :], preferred_element_type=jnp.float32))
    x1_ref[...] = x_ref[...] + _rms(mix, gp_ref[...])


def _out_proj(oa, ob, x2d, ga, gb, w_out, gp):
    m = x2d.shape[0]
    row = lambda i: (i, 0)
    fixed = lambda i: (0, 0)
    return pl.pallas_call(
        _out_proj_kernel,
        grid=(m // OUT_TM,),
        in_specs=[
            pl.BlockSpec((OUT_TM, D_SWA), row),
            pl.BlockSpec((OUT_TM, D_FOX), row),
            pl.BlockSpec((OUT_TM, D_MODEL), row),
            pl.BlockSpec((1, D_SWA), fixed),
            pl.BlockSpec((1, D_FOX), fixed),
            pl.BlockSpec((D_MIX, D_MODEL), fixed, pipeline_mode=pl.Buffered(1)),
            pl.BlockSpec((1, D_MODEL), fixed),
        ],
        out_specs=pl.BlockSpec((OUT_TM, D_MODEL), row),
        out_shape=jax.ShapeDtypeStruct((m, D_MODEL), jnp.float32),
        compiler_params=_params("arbitrary"),
        name="out_proj",
    )(oa, ob, x2d, ga, gb, w_out, gp)


FFN_TM = 512
FFN_TF = 512
CARRY = 8
ROW_CHUNK = 32
N_FT = D_FF // FFN_TF


def _gelu_tanh(x):
    return 0.5 * x * (1.0 + jnp.tanh(np.sqrt(2.0 / np.pi) * (x + 0.044715 * (x * x * x))))


def _ffn_kernel(x_ref, g_ref, wg_ref, wv_ref, cwg_ref, cwv_ref, cbg_ref, cbv_ref, wd_ref, gp_ref,
                o_ref, h_scr, ug_scr, uv_scr, tail_scr, *, tiles_per_seq):
    i = pl.program_id(0)
    j = pl.program_id(1)

    @pl.when(j == 0)
    def _():
        h_scr[...] = _rms(x_ref[...], g_ref[...]).astype(jnp.bfloat16)
        o_ref[...] = jnp.zeros_like(o_ref)

    @pl.when((i == 0) & (j == 0))
    def _():
        tail_scr[...] = jnp.zeros_like(tail_scr)

    h = h_scr[...]
    seq_start = i % tiles_per_seq == 0
    for k, (u_scr, w_ref) in enumerate(((ug_scr, wg_ref), (uv_scr, wv_ref))):
        u_scr[0:CARRY, :] = jnp.where(seq_start, 0.0, tail_scr[j, k])
        u_scr[CARRY:, :] = jnp.dot(h, w_ref[...], preferred_element_type=jnp.float32)
        tail_scr[j, k] = u_scr[FFN_TM:FFN_TM + CARRY, :]

    def conv(u_scr, cw_ref, cb_ref):
        y = cb_ref[...]
        for kk in range(CONV_WIDTH):
            y = y + u_scr[pl.ds(CARRY - (CONV_WIDTH - 1) + kk, FFN_TM), :] * cw_ref[kk:kk + 1, :]
        return y

    gate = conv(ug_scr, cwg_ref, cbg_ref)
    val = conv(uv_scr, cwv_ref, cbv_ref)
    a = (_gelu_tanh(gate) * val).astype(jnp.bfloat16)
    o_ref[...] += jnp.dot(a, wd_ref[...], preferred_element_type=jnp.float32)

    @pl.when(j == N_FT - 1)
    def _():
        gp = gp_ref[...]
        for r0 in range(0, FFN_TM, ROW_CHUNK):
            rows = slice(r0, r0 + ROW_CHUNK)
            o_ref[rows, :] = x_ref[rows, :] + _rms(o_ref[rows, :], gp)


def _ffn(x1, g, w_up, conv_w, conv_b, w_down, gp, seq_len):
    m = x1.shape[0]
    kern = functools.partial(_ffn_kernel, tiles_per_seq=seq_len // FFN_TM)
    return pl.pallas_call(
        kern,
        grid=(m // FFN_TM, N_FT),
        in_specs=[
            pl.BlockSpec((FFN_TM, D_MODEL), lambda i, j: (i, 0)),
            pl.BlockSpec((1, D_MODEL), lambda i, j: (0, 0)),
            pl.BlockSpec((D_MODEL, FFN_TF), lambda i, j: (0, j)),
            pl.BlockSpec((D_MODEL, FFN_TF), lambda i, j: (0, j + N_FT)),
            pl.BlockSpec((CONV_WIDTH, FFN_TF), lambda i, j: (0, j)),
            pl.BlockSpec((CONV_WIDTH, FFN_TF), lambda i, j: (0, j + N_FT)),
            pl.BlockSpec((1, FFN_TF), lambda i, j: (0, j)),
            pl.BlockSpec((1, FFN_TF), lambda i, j: (0, j + N_FT)),
            pl.BlockSpec((FFN_TF, D_MODEL), lambda i, j: (j, 0)),
            pl.BlockSpec((1, D_MODEL), lambda i, j: (0, 0)),
        ],
        out_specs=pl.BlockSpec((FFN_TM, D_MODEL), lambda i, j: (i, 0)),
        out_shape=jax.ShapeDtypeStruct((m, D_MODEL), jnp.float32),
        scratch_shapes=[
            pltpu.VMEM((FFN_TM, D_MODEL), jnp.bfloat16),
            pltpu.VMEM((FFN_TM + CARRY, FFN_TF), jnp.float32),
            pltpu.VMEM((FFN_TM + CARRY, FFN_TF), jnp.float32),
            pltpu.VMEM((N_FT, 2, CARRY, FFN_TF), jnp.float32),
        ],
        compiler_params=_params("arbitrary", "arbitrary"),
        name="conv_geglu_ffn",
    )(x1, g, w_up, w_up, conv_w, conv_w, conv_b, conv_b, w_down, gp)


def kernel(x, pre_mix_g, w_in, b_forget, sinks, grp_swa_g, grp_fox_g, w_out, post_mix_g,
           pre_ffn_g, w_up, conv_w, conv_b, w_down, post_ffn_g):
    b, s, d = x.shape
    depth = w_in.shape[0]
    xf = x.reshape(b * s, d)
    for l in range(depth):
        w_in_bf = _cast_pad_rows(jnp.swapaxes(w_in[l], 0, 1), D_IN_PAD)
        b_f = jnp.pad(b_forget[l], (0, LANES - N_HEADS_FOX)).reshape(1, LANES)

        proj, logf, w_out_bf, w_up_bf, w_down_bf = _in_proj(xf, pre_mix_g[l].reshape(1, d), w_in_bf, b_f,
                                                            (w_out[l], w_up[l], w_down[l]))
        proj3 = proj.reshape(b, s, D_QKV)
        aug = _cumsum(logf.reshape(b, s, LANES))

        o_a = _swa(proj3, sinks[l])
        o_b = _fox(proj3, aug)

        x1 = _out_proj(o_a.reshape(b * s, D_SWA), o_b.reshape(b * s, D_FOX), xf,
                       grp_swa_g[l].reshape(1, D_SWA), grp_fox_g[l].reshape(1, D_FOX),
                       w_out_bf, post_mix_g[l].reshape(1, d))
        xf = _ffn(x1, pre_ffn_g[l].reshape(1, d), w_up_bf, conv_w[l],
                  conv_b[l].reshape(1, 2 * D_FF), w_down_bf, post_ffn_g[l].reshape(1, d), s)
    return xf.reshape(b, s, d)
```

```python
import functools

import numpy as np
import jax
import jax.numpy as jnp
from jax import lax
from jax.experimental import pallas as pl
from jax.experimental.pallas import tpu as pltpu

D_MODEL = 2048
HEAD_DIM = 64
N_HEADS_SWA = 16
N_KV_SWA = 2
N_HEADS_FOX = 16
WINDOW = 128
BLOCK = 128
D_FF = 5632
CONV_WIDTH = 3
EPS = 1e-6
D_SWA = N_HEADS_SWA * HEAD_DIM
D_KV_SWA = N_KV_SWA * HEAD_DIM
D_FOX = N_HEADS_FOX * HEAD_DIM
D_MIX = D_SWA + D_FOX
D_QKV = D_SWA + 2 * D_KV_SWA + 3 * D_FOX

LANES = 128
COL_QA = 0
COL_KA = COL_QA + D_SWA // LANES
COL_VA = COL_KA + 1
COL_QB = COL_VA + 1
COL_KB = COL_QB + D_FOX // LANES
COL_VB = COL_KB + D_FOX // LANES
D_IN_PAD = D_QKV + LANES

NEG = -1e30
VMEM_LIMIT = 56 * 1024 * 1024

ALIBI_SLOPES = [float(v) for v in np.asarray(2.0 ** (-8.0 * np.arange(1, N_HEADS_SWA + 1) / N_HEADS_SWA),
                                            dtype=np.float32)]

_NT = (((1,), (1,)), ((), ()))


def _rms(xf, g):
    return xf * lax.rsqrt(jnp.mean(xf * xf, axis=-1, keepdims=True) + EPS) * g


def _params(*sem):
    return pltpu.CompilerParams(dimension_semantics=sem, vmem_limit_bytes=VMEM_LIMIT)


CAST_TM = 5 * LANES


def _cast_pad_kernel(w_ref, o_ref, *, valid_rows):
    row = pl.program_id(0) * CAST_TM + lax.broadcasted_iota(jnp.int32, w_ref.shape, 0)
    o_ref[...] = jnp.where(row < valid_rows, w_ref[...], 0.0).astype(jnp.bfloat16)


def _cast_pad_rows(w, padded_rows):
    rows, cols = w.shape
    assert padded_rows % CAST_TM == 0 and padded_rows - rows < CAST_TM
    return pl.pallas_call(
        functools.partial(_cast_pad_kernel, valid_rows=rows),
        grid=(padded_rows // CAST_TM,),
        in_specs=[pl.BlockSpec((CAST_TM, cols), lambda j: (j, 0))],
        out_specs=pl.BlockSpec((CAST_TM, cols), lambda j: (j, 0)),
        out_shape=jax.ShapeDtypeStruct((padded_rows, cols), jnp.bfloat16),
        compiler_params=_params("arbitrary"),
        name="cast_w_in",
    )(w)


IN_TM = 512
IN_CHUNK = 1024
LOG2E = float(np.log2(np.e))
Q_MULT = LOG2E / float(np.sqrt(HEAD_DIM))


Q_RANGES = ((COL_QA * LANES, COL_QA * LANES + D_SWA), (COL_QB * LANES, COL_QB * LANES + D_FOX))


def _in_proj_kernel(x_ref, g_ref, w_ref, wf_ref, bf_ref, *refs):
    n_side = (len(refs) - 2) // 2
    side_in, (proj_ref, logf_ref), side_out = refs[:n_side], refs[n_side:n_side + 2], refs[n_side + 2:]
    for src, dst in zip(side_in, side_out):
        dst[...] = src[...].astype(jnp.bfloat16)
    h = _rms(x_ref[...], g_ref[...]).astype(jnp.bfloat16)
    for n0 in range(0, D_QKV, IN_CHUNK):
        n1 = min(n0 + IN_CHUNK, D_QKV)
        acc = lax.dot_general(h, w_ref[n0:n1, :], _NT, preferred_element_type=jnp.float32)
        overlaps = [(max(lo, n0), min(hi, n1)) for lo, hi in Q_RANGES if max(lo, n0) < min(hi, n1)]
        if overlaps:
            col = n0 + lax.broadcasted_iota(jnp.int32, (1, n1 - n0), 1)
            mult = jnp.ones((1, n1 - n0), jnp.float32)
            for lo, hi in overlaps:
                mult = jnp.where((col >= lo) & (col < hi), Q_MULT, mult)
            acc = acc * mult
        proj_ref[:, n0:n1] = acc.astype(jnp.bfloat16)
    f = lax.dot_general(h, wf_ref[...], _NT, preferred_element_type=jnp.float32) + bf_ref[...]
    logf_ref[...] = jnp.minimum(f, 0.0) - jnp.log1p(jnp.exp(-jnp.abs(f)))


def _in_proj(x2d, g, w_in_bf, b_f, side_weights):
    m = x2d.shape[0]
    steps = m // IN_TM
    side_specs = []
    for w in side_weights:
        rows, cols = w.shape
        assert rows % (steps * 16) == 0
        side_specs.append(pl.BlockSpec((rows // steps, cols), lambda i: (i, 0)))
    return pl.pallas_call(
        _in_proj_kernel,
        grid=(steps,),
        in_specs=[
            pl.BlockSpec((IN_TM, D_MODEL), lambda i: (i, 0)),
            pl.BlockSpec((1, D_MODEL), lambda i: (0, 0)),
            pl.BlockSpec((D_QKV, D_MODEL), lambda i: (0, 0), pipeline_mode=pl.Buffered(1)),
            pl.BlockSpec((LANES, D_MODEL), lambda i: (D_QKV // LANES, 0), pipeline_mode=pl.Buffered(1)),
            pl.BlockSpec((1, LANES), lambda i: (0, 0)),
        ] + side_specs,
        out_specs=[
            pl.BlockSpec((IN_TM, D_QKV), lambda i: (i, 0)),
            pl.BlockSpec((IN_TM, LANES), lambda i: (i, 0)),
        ] + side_specs,
        out_shape=[
            jax.ShapeDtypeStruct((m, D_QKV), jnp.bfloat16),
            jax.ShapeDtypeStruct((m, LANES), jnp.float32),
        ] + [jax.ShapeDtypeStruct(w.shape, jnp.bfloat16) for w in side_weights],
        compiler_params=_params("arbitrary"),
        name="in_proj",
    )(x2d, g, w_in_bf, w_in_bf, b_f, *side_weights)


CS_BLK = 128
N_SPLIT = 3


def _bias_placement():
    place = np.zeros((N_SPLIT, LANES, D_FOX), np.float32)
    for h in range(N_HEADS_FOX):
        base = (h // 2) * LANES + (HEAD_DIM if h % 2 == 0 else 0)
        for i in range(N_SPLIT):
            place[i, h, base + i] = 1.0
    return place


def _cumsum_kernel(logf_ref, place_ref, a_ref, w_scr):
    s = logf_ref.shape[1]
    nblk = s // CS_BLK
    exact = dict(precision=lax.Precision.HIGHEST, preferred_element_type=jnp.float32)
    r = lax.broadcasted_iota(jnp.int32, (CS_BLK, CS_BLK), 0)
    c = lax.broadcasted_iota(jnp.int32, (CS_BLK, CS_BLK), 1)
    lower = (r >= c).astype(jnp.float32)
    blocks = [slice(b * CS_BLK, (b + 1) * CS_BLK) for b in range(nblk)]
    for rows in blocks:
        w_scr[rows, :] = jnp.dot(lower, logf_ref[0, rows, :], **exact)
    totals = w_scr[pl.ds(CS_BLK - 1, nblk, stride=CS_BLK), :]
    rb = lax.broadcasted_iota(jnp.int32, (nblk, nblk), 0)
    cb = lax.broadcasted_iota(jnp.int32, (nblk, nblk), 1)
    carry = jnp.dot((rb > cb).astype(jnp.float32), totals, **exact)
    for b, rows in enumerate(blocks):
        rest = (w_scr[rows, :] + carry[b:b + 1, :]) * (-LOG2E)
        parts = []
        for _ in range(N_SPLIT):
            part = rest.astype(jnp.bfloat16)
            rest = rest - part.astype(jnp.float32)
            parts.append(part)
        a_ref[0, rows, :] = jnp.dot(jnp.concatenate(parts, axis=1), place_ref[...],
                                    preferred_element_type=jnp.float32).astype(jnp.bfloat16)


def _cumsum(logf3):
    b, s, _ = logf3.shape
    place = jnp.asarray(_bias_placement().reshape(N_SPLIT * LANES, D_FOX), jnp.bfloat16)
    return pl.pallas_call(
        _cumsum_kernel,
        grid=(b,),
        in_specs=[pl.BlockSpec((1, s, LANES), lambda i: (i, 0, 0)),
                  pl.BlockSpec((N_SPLIT * LANES, D_FOX), lambda i: (0, 0))],
        out_specs=pl.BlockSpec((1, s, D_FOX), lambda i: (i, 0, 0)),
        out_shape=jax.ShapeDtypeStruct((b, s, D_FOX), jnp.bfloat16),
        scratch_shapes=[pltpu.VMEM((s, LANES), jnp.float32)],
        compiler_params=_params("arbitrary"),
        name="cumsum_logf",
    )(logf3, place)


SWA_TQ = 512
PAIRS_PER_KV = (N_HEADS_SWA // N_KV_SWA) // 2


SWA_R = SWA_TQ // BLOCK
SWA_COLS = PAIRS_PER_KV * BLOCK


def _swa_kernel(sink_ref, q_ref, k_ref, v_ref, o_ref, k_scr, vt_scr, bias_scr, st_scr):
    bi = pl.program_id(0)
    qi = pl.program_id(1)
    s_len = k_ref.shape[1]
    log2_block = int(np.log2(BLOCK))

    @pl.when((bi == 0) & (qi == 0))
    def _():
        key_loc = lax.broadcasted_iota(jnp.int32, (2 * BLOCK, SWA_COLS), 0)
        col = lax.broadcasted_iota(jnp.int32, (2 * BLOCK, SWA_COLS), 1)
        dist = BLOCK + (col & (BLOCK - 1)) - key_loc
        pair = lax.shift_right_logical(col, log2_block)
        distf = dist.astype(jnp.float32)
        in_win = (dist >= 0) & (dist < WINDOW)
        for kk in range(N_KV_SWA):
            for par in range(2):
                slope = jnp.zeros((2 * BLOCK, SWA_COLS), jnp.float32)
                for pi in range(PAIRS_PER_KV):
                    h = kk * 2 * PAIRS_PER_KV + 2 * pi + par
                    slope = jnp.where(pair == pi, ALIBI_SLOPES[h] * LOG2E, slope)
                bias_scr[kk, par] = jnp.where(in_win, -slope * distf, NEG)

    @pl.when(qi == 0)
    def _():
        low = lax.broadcasted_iota(jnp.int32, (SWA_TQ, LANES), 1) < HEAD_DIM
        halves = (low, jnp.logical_not(low))
        zero_k = jnp.zeros((BLOCK, LANES), jnp.bfloat16)
        zero_v = jnp.zeros((LANES, BLOCK), jnp.bfloat16)
        for kk in range(N_KV_SWA):
            for par in range(2):
                k_scr[kk, par, 0:BLOCK, :] = zero_k
                vt_scr[kk, par, :, 0:BLOCK] = zero_v
        for c0 in range(0, s_len, SWA_TQ):
            kc = k_ref[0, c0:c0 + SWA_TQ, :].astype(jnp.float32)
            vc = v_ref[0, c0:c0 + SWA_TQ, :].astype(jnp.float32)
            ksw = pltpu.roll(kc, HEAD_DIM, 1)
            vsw = pltpu.roll(vc, HEAD_DIM, 1)
            rows = slice(BLOCK + c0, BLOCK + c0 + SWA_TQ)
            for kk in range(N_KV_SWA):
                for par in range(2):
                    ksrc, vsrc = (kc, vc) if kk == par else (ksw, vsw)
                    k_scr[kk, par, rows, :] = jnp.where(halves[par], ksrc, 0.0).astype(jnp.bfloat16)
                    vt_scr[kk, par, :, rows] = jnp.where(halves[par], vsrc, 1.0).T.astype(jnp.bfloat16)

    pair1 = lax.shift_right_logical(lax.broadcasted_iota(jnp.int32, (1, SWA_COLS), 1), log2_block)
    pad_rows = lax.broadcasted_iota(jnp.int32, (2 * BLOCK, SWA_COLS), 0) < BLOCK
    first_tile = qi == 0
    sinks2 = {}
    for kk in range(N_KV_SWA):
        for par in range(2):
            sink = jnp.zeros((1, SWA_COLS), jnp.float32)
            for pi in range(PAIRS_PER_KV):
                sink = jnp.where(pair1 == pi, sink_ref[kk * 2 * PAIRS_PER_KV + 2 * pi + par] * LOG2E, sink)
            sinks2[kk, par] = sink

    def band_start(r):
        return pl.multiple_of(qi * SWA_TQ + r * BLOCK, BLOCK)

    def scores(r):
        rows = slice(r * BLOCK, (r + 1) * BLOCK)
        for kk in range(N_KV_SWA):
            qs = jnp.concatenate([q_ref[0, rows, (kk * PAIRS_PER_KV + pi) * LANES:(kk * PAIRS_PER_KV + pi + 1) * LANES]
                                  for pi in range(PAIRS_PER_KV)], axis=0)
            for par in range(2):
                st_scr[r, kk, par] = lax.dot_general(k_scr[kk, par, pl.ds(band_start(r), 2 * BLOCK), :], qs, _NT,
                                                     preferred_element_type=jnp.float32)

    def finish(r):
        rows = slice(r * BLOCK, (r + 1) * BLOCK)
        for kk in range(N_KV_SWA):
            outs = []
            for par in range(2):
                sink = sinks2[kk, par]
                st = st_scr[r, kk, par] + bias_scr[kk, par]
                if r == 0:
                    st = jnp.where(pad_rows & first_tile, NEG, st)
                m = jnp.maximum(jnp.max(st, axis=0, keepdims=True), sink)
                p = jnp.exp2(st - m).astype(jnp.bfloat16)
                o = jnp.dot(vt_scr[kk, par, :, pl.ds(band_start(r), 2 * BLOCK)], p,
                            preferred_element_type=jnp.float32)
                extra = jnp.exp2(sink - m)
                if par == 0:
                    outs.append(o[0:HEAD_DIM] / (o[HEAD_DIM:HEAD_DIM + 1] + extra))
                else:
                    outs.append(o[HEAD_DIM:LANES] / (o[0:1] + extra))
            ot = jnp.concatenate(outs, axis=0).T
            for pi in range(PAIRS_PER_KV):
                c0 = (kk * PAIRS_PER_KV + pi) * LANES
                o_ref[0, rows, c0:c0 + LANES] = ot[pi * BLOCK:(pi + 1) * BLOCK, :]

    scores(0)
    for r in range(SWA_R):
        if r + 1 < SWA_R:
            scores(r + 1)
        finish(r)


def _swa(proj3, sinks):
    b, s, _ = proj3.shape
    return pl.pallas_call(
        _swa_kernel,
        grid=(b, s // SWA_TQ),
        in_specs=[
            pl.BlockSpec(memory_space=pltpu.SMEM),
            pl.BlockSpec((1, SWA_TQ, D_SWA), lambda bi, qi: (bi, qi, COL_QA * LANES // D_SWA)),
            pl.BlockSpec((1, s, LANES), lambda bi, qi: (bi, 0, COL_KA)),
            pl.BlockSpec((1, s, LANES), lambda bi, qi: (bi, 0, COL_VA)),
        ],
        out_specs=pl.BlockSpec((1, SWA_TQ, D_SWA), lambda bi, qi: (bi, qi, 0)),
        scratch_shapes=[
            pltpu.VMEM((N_KV_SWA, 2, s + BLOCK, LANES), jnp.bfloat16),
            pltpu.VMEM((N_KV_SWA, 2, LANES, s + BLOCK), jnp.bfloat16),
            pltpu.VMEM((N_KV_SWA, 2, 2 * BLOCK, SWA_COLS), jnp.float32),
            pltpu.VMEM((SWA_R, N_KV_SWA, 2, 2 * BLOCK, SWA_COLS), jnp.float32),
        ],
        out_shape=jax.ShapeDtypeStruct((b, s, D_SWA), jnp.float32),
        compiler_params=_params("arbitrary", "arbitrary"),
        name="swa_attention",
    )(sinks, proj3, proj3, proj3)


FOX_TQ = 512
FOX_TK = FOX_TQ
FOX_PAIRS = 2
FOX_VROWS = HEAD_DIM + 16
assert all(col % FOX_PAIRS == 0 for col in (COL_QB, COL_KB, COL_VB))


def _fox_kernel(q_ref, qn_ref, k_ref, v_ref, a_ref, o_ref, ka_scr, vat_scr, qa_scr, m_scr, acc_scr, st_scr, cmax_scr):
    qi = pl.program_id(2)
    s_len = k_ref.shape[1]
    n_tiles = s_len // FOX_TQ
    half = FOX_TQ // 2
    low_k = lax.broadcasted_iota(jnp.int32, (FOX_TK, LANES), 1) < HEAD_DIM
    keeps = (low_k, jnp.logical_not(low_k))
    heads = range(2 * FOX_PAIRS)

    lane_q = lax.broadcasted_iota(jnp.int32, (FOX_TQ, LANES), 1)
    ones_hi = jnp.where((lane_q >= HEAD_DIM) & (lane_q < HEAD_DIM + N_SPLIT), 1.0, 0.0).astype(jnp.bfloat16)
    ones_lo = jnp.where(lane_q < N_SPLIT, 1.0, 0.0).astype(jnp.bfloat16)

    def build_queries(ref):
        for pair in range(FOX_PAIRS):
            q = ref[0, :, pair * LANES:(pair + 1) * LANES]
            qa_scr[2 * pair] = jnp.where(lane_q < HEAD_DIM, q, ones_hi)
            qa_scr[2 * pair + 1] = jnp.where(lane_q < HEAD_DIM, ones_lo, q)

    @pl.when(qi == 0)
    def _():
        for c0 in range(0, s_len, FOX_TK):
            for pair in range(FOX_PAIRS):
                cols = slice(pair * LANES, (pair + 1) * LANES)
                kc = k_ref[0, c0:c0 + FOX_TK, cols]
                vc = v_ref[0, c0:c0 + FOX_TK, cols].astype(jnp.float32)
                ac = a_ref[0, c0:c0 + FOX_TK, cols]
                for par in range(2):
                    h = 2 * pair + par
                    ka_scr[h, c0:c0 + FOX_TK, :] = jnp.where(keeps[par], kc, ac)
                    vt = jnp.where(keeps[par], vc, 1.0).T.astype(jnp.bfloat16)
                    vat_scr[h, :, c0:c0 + FOX_TK] = vt[0:FOX_VROWS] if par == 0 else vt[LANES - FOX_VROWS:LANES]
        build_queries(q_ref)

    tri = (lax.broadcasted_iota(jnp.int32, (half, half), 0)
           <= lax.broadcasted_iota(jnp.int32, (half, half), 1))

    m_scr[...] = jnp.full(m_scr.shape, NEG, jnp.float32)
    acc_scr[...] = jnp.zeros(acc_scr.shape, jnp.float32)

    def scores(c, slot, diagonal):
        keys = slice(c * FOX_TK, (c + 1) * FOX_TK)
        for h in heads:
            if diagonal:
                lo_keys = slice(c * FOX_TK, c * FOX_TK + half)
                st_scr[slot, h, 0:half, 0:half] = lax.dot_general(ka_scr[h, lo_keys, :], qa_scr[h, 0:half, :], _NT,
                                                                  preferred_element_type=jnp.float32)
                st_scr[slot, h, :, half:FOX_TQ] = lax.dot_general(ka_scr[h, keys, :], qa_scr[h, half:FOX_TQ, :], _NT,
                                                                  preferred_element_type=jnp.float32)
            else:
                st = lax.dot_general(ka_scr[h, keys, :], qa_scr[h], _NT,
                                     preferred_element_type=jnp.float32)
                st_scr[slot, h] = st
                cmax_scr[slot, h] = jnp.max(st, axis=0, keepdims=True)

    def accumulate(h, st, cmax, keys, cols):
        m_old = m_scr[h, :, cols]
        m_new = jnp.maximum(m_old, cmax)
        alpha = jnp.exp2(m_old - m_new)
        pt = jnp.exp2(st - m_new).astype(jnp.bfloat16)
        acc_scr[h, :, cols] = alpha * acc_scr[h, :, cols] + jnp.dot(vat_scr[h, :, keys], pt,
                                                                  preferred_element_type=jnp.float32)
        m_scr[h, :, cols] = m_new

    def update(c, slot, diagonal):
        k0 = c * FOX_TK
        for h in heads:
            if diagonal:
                st_lo = jnp.where(tri, st_scr[slot, h, 0:half, 0:half], NEG)
                accumulate(h, st_lo, jnp.max(st_lo, axis=0, keepdims=True), slice(k0, k0 + half), slice(0, half))
                st_hi = jnp.concatenate([st_scr[slot, h, 0:half, half:FOX_TQ],
                                         jnp.where(tri, st_scr[slot, h, half:FOX_TK, half:FOX_TQ], NEG)], axis=0)
                accumulate(h, st_hi, jnp.max(st_hi, axis=0, keepdims=True), slice(k0, k0 + FOX_TK),
                           slice(half, FOX_TQ))
            else:
                accumulate(h, st_scr[slot, h], cmax_scr[slot, h], slice(k0, k0 + FOX_TK), slice(0, FOX_TQ))

    def tile(t):
        flip = (t * (t + 1) // 2) % 2
        slot = lambda c: (c + flip) % 2
        if t == 0:
            scores(0, slot(0), True)
        for c in range(t + 1):
            if c < t:
                scores(c + 1, slot(c + 1), c + 1 == t)
            elif t + 1 < n_tiles:
                build_queries(qn_ref)
                scores(0, slot(t + 1), False)
            update(c, slot(c), c == t)

    for t in range(n_tiles):
        pl.when(qi == t)(functools.partial(tile, t))

    for pair in range(FOX_PAIRS):
        acc0 = acc_scr[2 * pair]
        acc1 = acc_scr[2 * pair + 1]
        ot = jnp.concatenate([acc0[0:HEAD_DIM] / acc0[HEAD_DIM:HEAD_DIM + 1],
                              acc1[FOX_VROWS - HEAD_DIM:FOX_VROWS] / acc1[0:1]], axis=0)
        o_ref[0, :, pair * LANES:(pair + 1) * LANES] = ot.T


def _fox(proj3, aug):
    b, s, _ = proj3.shape
    width = FOX_PAIRS * LANES
    ngroups = D_FOX // width
    nheads = 2 * FOX_PAIRS
    n_tiles = s // FOX_TQ
    return pl.pallas_call(
        _fox_kernel,
        grid=(b, ngroups, n_tiles),
        in_specs=[
            pl.BlockSpec((1, FOX_TQ, width), lambda bi, p, qi: (bi, qi, COL_QB // FOX_PAIRS + p)),
            pl.BlockSpec((1, FOX_TQ, width),
                         lambda bi, p, qi: (bi, jnp.minimum(qi + 1, n_tiles - 1), COL_QB // FOX_PAIRS + p)),
            pl.BlockSpec((1, s, width), lambda bi, p, qi: (bi, 0, COL_KB // FOX_PAIRS + p)),
            pl.BlockSpec((1, s, width), lambda bi, p, qi: (bi, 0, COL_VB // FOX_PAIRS + p)),
            pl.BlockSpec((1, s, width), lambda bi, p, qi: (bi, 0, p)),
        ],
        out_specs=pl.BlockSpec((1, FOX_TQ, width), lambda bi, p, qi: (bi, qi, p)),
        out_shape=jax.ShapeDtypeStruct((b, s, D_FOX), jnp.float32),
        scratch_shapes=[
            pltpu.VMEM((nheads, s, LANES), jnp.bfloat16),
            pltpu.VMEM((nheads, FOX_VROWS, s), jnp.bfloat16),
            pltpu.VMEM((nheads, FOX_TQ, LANES), jnp.bfloat16),
            pltpu.VMEM((nheads, 1, FOX_TQ), jnp.float32),
            pltpu.VMEM((nheads, FOX_VROWS, FOX_TQ), jnp.float32),
            pltpu.VMEM((2, nheads, FOX_TK, FOX_TQ), jnp.float32),
            pltpu.VMEM((2, nheads, 1, FOX_TQ), jnp.float32),
        ],
        compiler_params=_params("arbitrary", "arbitrary", "arbitrary"),
        name="fox_attention",
    )(proj3, proj3, proj3, proj3, aug)


OUT_TM = 512


def _out_proj_kernel(oa_ref, ob_ref, x_ref, ga_ref, gb_ref, w_ref, gp_ref, x1_ref):
    na = _rms(oa_ref[...], ga_ref[...]).astype(jnp.bfloat16)
    nb = _rms(ob_ref[...], gb_ref[...]).astype(jnp.bfloat16)
    mix = (jnp.dot(na, w_ref[0:D_SWA, :], preferred_element_type=jnp.float32)
           + jnp.dot(nb, w_ref[D_SWA:D_MIX, :], preferred_element_type=jnp.float32))
    x1_ref[...] = x_ref[...] + _rms(mix, gp_ref[...])


def _out_proj(oa, ob, x2d, ga, gb, w_out, gp):
    m = x2d.shape[0]
    row = lambda i: (i, 0)
    fixed = lambda i: (0, 0)
    return pl.pallas_call(
        _out_proj_kernel,
        grid=(m // OUT_TM,),
        in_specs=[
            pl.BlockSpec((OUT_TM, D_SWA), row),
            pl.BlockSpec((OUT_TM, D_FOX), row),
            pl.BlockSpec((OUT_TM, D_MODEL), row),
            pl.BlockSpec((1, D_SWA), fixed),
            pl.BlockSpec((1, D_FOX), fixed),
            pl.BlockSpec((D_MIX, D_MODEL), fixed, pipeline_mode=pl.Buffered(1)),
            pl.BlockSpec((1, D_MODEL), fixed),
        ],
        out_specs=pl.BlockSpec((OUT_TM, D_MODEL), row),
        out_shape=jax.ShapeDtypeStruct((m, D_MODEL), jnp.float32),
        compiler_params=_params("arbitrary"),
        name="out_proj",
    )(oa, ob, x2d, ga, gb, w_out, gp)


FFN_TM = 512
FFN_TF = 512
CARRY = 8
ROW_CHUNK = 32
N_FT = D_FF // FFN_TF


def _gelu_tanh(x):
    return 0.5 * x * (1.0 + jnp.tanh(np.sqrt(2.0 / np.pi) * (x + 0.044715 * (x * x * x))))


def _ffn_kernel(x_ref, xn_ref, g_ref, wg_ref, wv_ref, cwg_ref, cwv_ref, cbg_ref, cbv_ref, wd_ref, gp_ref,
                o_ref, h_scr, ug_scr, uv_scr, tail_scr, *, tiles_per_seq):
    i = pl.program_id(0)
    j = pl.program_id(1)
    g = g_ref[...]

    @pl.when((i == 0) & (j == 0))
    def _():
        tail_scr[...] = jnp.zeros_like(tail_scr)
        h_scr[0] = _rms(x_ref[...], g).astype(jnp.bfloat16)

    seq_start = i % tiles_per_seq == 0

    def conv(u_scr, cw_ref, cb_ref):
        y = cb_ref[...]
        for kk in range(CONV_WIDTH):
            y = y + u_scr[pl.ds(CARRY - (CONV_WIDTH - 1) + kk, FFN_TM), :] * cw_ref[kk:kk + 1, :]
        return y

    def step(cur, first, last):
        h = h_scr[cur]
        for k, (u_scr, w_ref) in enumerate(((ug_scr, wg_ref), (uv_scr, wv_ref))):
            u_scr[0:CARRY, :] = jnp.where(seq_start, 0.0, tail_scr[j, k])
            u_scr[CARRY:, :] = jnp.dot(h, w_ref[...], preferred_element_type=jnp.float32)
            tail_scr[j, k] = u_scr[FFN_TM:FFN_TM + CARRY, :]
        a = (_gelu_tanh(conv(ug_scr, cwg_ref, cbg_ref)) * conv(uv_scr, cwv_ref, cbv_ref)).astype(jnp.bfloat16)
        part = jnp.dot(a, wd_ref[...], preferred_element_type=jnp.float32)
        if first:
            o_ref[...] = part
        else:
            o_ref[...] += part
        if last:
            h_scr[1 - cur] = _rms(xn_ref[...], g).astype(jnp.bfloat16)
            gp = gp_ref[...]
            for r0 in range(0, FFN_TM, ROW_CHUNK):
                rows = slice(r0, r0 + ROW_CHUNK)
                o_ref[rows, :] = x_ref[rows, :] + _rms(o_ref[rows, :], gp)

    for cur in range(2):
        parity = i % 2 == cur
        pl.when(parity & (j == 0))(functools.partial(step, cur, True, False))
        pl.when(parity & (j > 0) & (j < N_FT - 1))(functools.partial(step, cur, False, False))
        pl.when(parity & (j == N_FT - 1))(functools.partial(step, cur, False, True))


def _ffn(x1, g, w_up, conv_w, conv_b, w_down, gp, seq_len):
    m = x1.shape[0]
    n_tiles = m // FFN_TM
    kern = functools.partial(_ffn_kernel, tiles_per_seq=seq_len // FFN_TM)
    return pl.pallas_call(
        kern,
        grid=(n_tiles, N_FT),
        in_specs=[
            pl.BlockSpec((FFN_TM, D_MODEL), lambda i, j: (i, 0)),
            pl.BlockSpec((FFN_TM, D_MODEL), lambda i, j: (jnp.minimum(i + 1, n_tiles - 1), 0)),
            pl.BlockSpec((1, D_MODEL), lambda i, j: (0, 0)),
            pl.BlockSpec((D_MODEL, FFN_TF), lambda i, j: (0, j)),
            pl.BlockSpec((D_MODEL, FFN_TF), lambda i, j: (0, j + N_FT)),
            pl.BlockSpec((CONV_WIDTH, FFN_TF), lambda i, j: (0, j)),
            pl.BlockSpec((CONV_WIDTH, FFN_TF), lambda i, j: (0, j + N_FT)),
            pl.BlockSpec((1, FFN_TF), lambda i, j: (0, j)),
            pl.BlockSpec((1, FFN_TF), lambda i, j: (0, j + N_FT)),
            pl.BlockSpec((FFN_TF, D_MODEL), lambda i, j: (j, 0)),
            pl.BlockSpec((1, D_MODEL), lambda i, j: (0, 0)),
        ],
        out_specs=pl.BlockSpec((FFN_TM, D_MODEL), lambda i, j: (i, 0)),
        out_shape=jax.ShapeDtypeStruct((m, D_MODEL), jnp.float32),
        scratch_shapes=[
            pltpu.VMEM((2, FFN_TM, D_MODEL), jnp.bfloat16),
            pltpu.VMEM((FFN_TM + CARRY, FFN_TF), jnp.float32),
            pltpu.VMEM((FFN_TM + CARRY, FFN_TF), jnp.float32),
            pltpu.VMEM((N_FT, 2, CARRY, FFN_TF), jnp.float32),
        ],
        compiler_params=_params("arbitrary", "arbitrary"),
        name="conv_geglu_ffn",
    )(x1, x1, g, w_up, w_up, conv_w, conv_w, conv_b, conv_b, w_down, gp)


def kernel(x, pre_mix_g, w_in, b_forget, sinks, grp_swa_g, grp_fox_g, w_out, post_mix_g,
           pre_ffn_g, w_up, conv_w, conv_b, w_down, post_ffn_g):
    b, s, d = x.shape
    depth = w_in.shape[0]
    xf = x.reshape(b * s, d)
    for l in range(depth):
        w_in_bf = _cast_pad_rows(jnp.swapaxes(w_in[l], 0, 1), D_IN_PAD)
        b_f = jnp.pad(b_forget[l], (0, LANES - N_HEADS_FOX)).reshape(1, LANES)

        proj, logf, w_out_bf, w_up_bf, w_down_bf = _in_proj(xf, pre_mix_g[l].reshape(1, d), w_in_bf, b_f,
                                                            (w_out[l], w_up[l], w_down[l]))
        proj3 = proj.reshape(b, s, D_QKV)
        aug = _cumsum(logf.reshape(b, s, LANES))

        o_a = _swa(proj3, sinks[l])
        o_b = _fox(proj3, aug)

        x1 = _out_proj(o_a.reshape(b * s, D_SWA), o_b.reshape(b * s, D_FOX), xf,
                       grp_swa_g[l].reshape(1, D_SWA), grp_fox_g[l].reshape(1, D_FOX),
                       w_out_bf, post_mix_g[l].reshape(1, d))
        xf = _ffn(x1, pre_ffn_g[l].reshape(1, d), w_up_bf, conv_w[l],
                  conv_b[l].reshape(1, 2 * D_FF), w_down_bf, post_ffn_g[l].reshape(1, d), s)
    return xf.reshape(b, s, d)
```

```python
import functools

import numpy as np
import jax
import jax.numpy as jnp
from jax import lax
from jax.experimental import pallas as pl
from jax.experimental.pallas import tpu as pltpu

D_MODEL = 2048
HEAD_DIM = 64
N_HEADS_SWA = 16
N_KV_SWA = 2
N_HEADS_FOX = 16
WINDOW = 128
BLOCK = 128
D_FF = 5632
CONV_WIDTH = 3
EPS = 1e-6
D_SWA = N_HEADS_SWA * HEAD_DIM
D_KV_SWA = N_KV_SWA * HEAD_DIM
D_FOX = N_HEADS_FOX * HEAD_DIM
D_MIX = D_SWA + D_FOX
D_QKV = D_SWA + 2 * D_KV_SWA + 3 * D_FOX

LANES = 128
COL_QA = 0
COL_KA = COL_QA + D_SWA // LANES
COL_VA = COL_KA + 1
COL_QB = COL_VA + 1
COL_KB = COL_QB + D_FOX // LANES
COL_VB = COL_KB + D_FOX // LANES
D_IN_PAD = D_QKV + LANES

NEG = -1e30
VMEM_LIMIT = 56 * 1024 * 1024

ALIBI_SLOPES = [float(v) for v in np.asarray(2.0 ** (-8.0 * np.arange(1, N_HEADS_SWA + 1) / N_HEADS_SWA),
                                            dtype=np.float32)]

_NT = (((1,), (1,)), ((), ()))


def _rms(xf, g):
    return xf * lax.rsqrt(jnp.mean(xf * xf, axis=-1, keepdims=True) + EPS) * g


def _params(*sem):
    return pltpu.CompilerParams(dimension_semantics=sem, vmem_limit_bytes=VMEM_LIMIT)


CAST_TM = 5 * LANES


def _cast_pad_kernel(w_ref, o_ref, *, valid_rows):
    row = pl.program_id(0) * CAST_TM + lax.broadcasted_iota(jnp.int32, w_ref.shape, 0)
    o_ref[...] = jnp.where(row < valid_rows, w_ref[...], 0.0).astype(jnp.bfloat16)


def _cast_pad_rows(w, padded_rows):
    rows, cols = w.shape
    assert padded_rows % CAST_TM == 0 and padded_rows - rows < CAST_TM
    return pl.pallas_call(
        functools.partial(_cast_pad_kernel, valid_rows=rows),
        grid=(padded_rows // CAST_TM,),
        in_specs=[pl.BlockSpec((CAST_TM, cols), lambda j: (j, 0))],
        out_specs=pl.BlockSpec((CAST_TM, cols), lambda j: (j, 0)),
        out_shape=jax.ShapeDtypeStruct((padded_rows, cols), jnp.bfloat16),
        compiler_params=_params("arbitrary"),
        name="cast_w_in",
    )(w)


IN_TM = 512
IN_CHUNK = 1024
LOG2E = float(np.log2(np.e))
Q_MULT = LOG2E / float(np.sqrt(HEAD_DIM))


Q_RANGES = ((COL_QA * LANES, COL_QA * LANES + D_SWA), (COL_QB * LANES, COL_QB * LANES + D_FOX))


def _in_proj_kernel(x_ref, g_ref, w_ref, wf_ref, bf_ref, *refs):
    n_side = (len(refs) - 2) // 2
    side_in, (proj_ref, logf_ref), side_out = refs[:n_side], refs[n_side:n_side + 2], refs[n_side + 2:]
    for src, dst in zip(side_in, side_out):
        dst[...] = src[...].astype(jnp.bfloat16)
    h = _rms(x_ref[...], g_ref[...]).astype(jnp.bfloat16)
    for n0 in range(0, D_QKV, IN_CHUNK):
        n1 = min(n0 + IN_CHUNK, D_QKV)
        acc = lax.dot_general(h, w_ref[n0:n1, :], _NT, preferred_element_type=jnp.float32)
        overlaps = [(max(lo, n0), min(hi, n1)) for lo, hi in Q_RANGES if max(lo, n0) < min(hi, n1)]
        if overlaps:
            col = n0 + lax.broadcasted_iota(jnp.int32, (1, n1 - n0), 1)
            mult = jnp.ones((1, n1 - n0), jnp.float32)
            for lo, hi in overlaps:
                mult = jnp.where((col >= lo) & (col < hi), Q_MULT, mult)
            acc = acc * mult
        proj_ref[:, n0:n1] = acc.astype(jnp.bfloat16)
    f = lax.dot_general(h, wf_ref[...], _NT, preferred_element_type=jnp.float32) + bf_ref[...]
    logf_ref[...] = jnp.minimum(f, 0.0) - jnp.log1p(jnp.exp(-jnp.abs(f)))


def _in_proj(x2d, g, w_in_bf, b_f, side_weights):
    m = x2d.shape[0]
    steps = m // IN_TM
    side_specs = []
    for w in side_weights:
        rows, cols = w.shape
        assert rows % (steps * 16) == 0
        side_specs.append(pl.BlockSpec((rows // steps, cols), lambda i: (i, 0)))
    return pl.pallas_call(
        _in_proj_kernel,
        grid=(steps,),
        in_specs=[
            pl.BlockSpec((IN_TM, D_MODEL), lambda i: (i, 0)),
            pl.BlockSpec((1, D_MODEL), lambda i: (0, 0)),
            pl.BlockSpec((D_QKV, D_MODEL), lambda i: (0, 0), pipeline_mode=pl.Buffered(1)),
            pl.BlockSpec((LANES, D_MODEL), lambda i: (D_QKV // LANES, 0), pipeline_mode=pl.Buffered(1)),
            pl.BlockSpec((1, LANES), lambda i: (0, 0)),
        ] + side_specs,
        out_specs=[
            pl.BlockSpec((IN_TM, D_QKV), lambda i: (i, 0)),
            pl.BlockSpec((IN_TM, LANES), lambda i: (i, 0)),
        ] + side_specs,
        out_shape=[
            jax.ShapeDtypeStruct((m, D_QKV), jnp.bfloat16),
            jax.ShapeDtypeStruct((m, LANES), jnp.float32),
        ] + [jax.ShapeDtypeStruct(w.shape, jnp.bfloat16) for w in side_weights],
        compiler_params=_params("arbitrary"),
        name="in_proj",
    )(x2d, g, w_in_bf, w_in_bf, b_f, *side_weights)


CS_BLK = 128
N_SPLIT = 3


def _bias_placement():
    place = np.zeros((N_SPLIT, LANES, D_FOX), np.float32)
    for h in range(N_HEADS_FOX):
        base = (h // 2) * LANES + (HEAD_DIM if h % 2 == 0 else 0)
        for i in range(N_SPLIT):
            place[i, h, base + i] = 1.0
    return place


def _cumsum_kernel(logf_ref, place_ref, a_ref, w_scr):
    s = logf_ref.shape[1]
    nblk = s // CS_BLK
    exact = dict(precision=lax.Precision.HIGHEST, preferred_element_type=jnp.float32)
    r = lax.broadcasted_iota(jnp.int32, (CS_BLK, CS_BLK), 0)
    c = lax.broadcasted_iota(jnp.int32, (CS_BLK, CS_BLK), 1)
    lower = (r >= c).astype(jnp.float32)
    blocks = [slice(b * CS_BLK, (b + 1) * CS_BLK) for b in range(nblk)]
    for rows in blocks:
        w_scr[rows, :] = jnp.dot(lower, logf_ref[0, rows, :], **exact)
    totals = w_scr[pl.ds(CS_BLK - 1, nblk, stride=CS_BLK), :]
    rb = lax.broadcasted_iota(jnp.int32, (nblk, nblk), 0)
    cb = lax.broadcasted_iota(jnp.int32, (nblk, nblk), 1)
    carry = jnp.dot((rb > cb).astype(jnp.float32), totals, **exact)
    for b, rows in enumerate(blocks):
        rest = (w_scr[rows, :] + carry[b:b + 1, :]) * (-LOG2E)
        parts = []
        for _ in range(N_SPLIT):
            part = rest.astype(jnp.bfloat16)
            rest = rest - part.astype(jnp.float32)
            parts.append(part)
        a_ref[0, rows, :] = jnp.dot(jnp.concatenate(parts, axis=1), place_ref[...],
                                    preferred_element_type=jnp.float32).astype(jnp.bfloat16)


def _cumsum(logf3):
    b, s, _ = logf3.shape
    place = jnp.asarray(_bias_placement().reshape(N_SPLIT * LANES, D_FOX), jnp.bfloat16)
    return pl.pallas_call(
        _cumsum_kernel,
        grid=(b,),
        in_specs=[pl.BlockSpec((1, s, LANES), lambda i: (i, 0, 0)),
                  pl.BlockSpec((N_SPLIT * LANES, D_FOX), lambda i: (0, 0))],
        out_specs=pl.BlockSpec((1, s, D_FOX), lambda i: (i, 0, 0)),
        out_shape=jax.ShapeDtypeStruct((b, s, D_FOX), jnp.bfloat16),
        scratch_shapes=[pltpu.VMEM((s, LANES), jnp.float32)],
        compiler_params=_params("arbitrary"),
        name="cumsum_logf",
    )(logf3, place)


SWA_TQ = 512
PAIRS_PER_KV = (N_HEADS_SWA // N_KV_SWA) // 2


SWA_R = SWA_TQ // BLOCK
SWA_COLS = PAIRS_PER_KV * BLOCK


def _swa_kernel(sink_ref, q_ref, k_ref, v_ref, o_ref, k_scr, vt_scr, bias_scr, st_scr):
    bi = pl.program_id(0)
    qi = pl.program_id(1)
    s_len = k_ref.shape[1]
    log2_block = int(np.log2(BLOCK))

    @pl.when((bi == 0) & (qi == 0))
    def _():
        key_loc = lax.broadcasted_iota(jnp.int32, (2 * BLOCK, SWA_COLS), 0)
        col = lax.broadcasted_iota(jnp.int32, (2 * BLOCK, SWA_COLS), 1)
        dist = BLOCK + (col & (BLOCK - 1)) - key_loc
        pair = lax.shift_right_logical(col, log2_block)
        distf = dist.astype(jnp.float32)
        in_win = (dist >= 0) & (dist < WINDOW)
        for kk in range(N_KV_SWA):
            for par in range(2):
                slope = jnp.zeros((2 * BLOCK, SWA_COLS), jnp.float32)
                for pi in range(PAIRS_PER_KV):
                    h = kk * 2 * PAIRS_PER_KV + 2 * pi + par
                    slope = jnp.where(pair == pi, ALIBI_SLOPES[h] * LOG2E, slope)
                bias_scr[kk, par] = jnp.where(in_win, -slope * distf, NEG)

    @pl.when(qi == 0)
    def _():
        low = lax.broadcasted_iota(jnp.int32, (SWA_TQ, LANES), 1) < HEAD_DIM
        halves = (low, jnp.logical_not(low))
        zero_k = jnp.zeros((BLOCK, LANES), jnp.bfloat16)
        zero_v = jnp.zeros((LANES, BLOCK), jnp.bfloat16)
        for kk in range(N_KV_SWA):
            for par in range(2):
                k_scr[kk, par, 0:BLOCK, :] = zero_k
                vt_scr[kk, par, :, 0:BLOCK] = zero_v
        for c0 in range(0, s_len, SWA_TQ):
            kc = k_ref[0, c0:c0 + SWA_TQ, :].astype(jnp.float32)
            vc = v_ref[0, c0:c0 + SWA_TQ, :].astype(jnp.float32)
            ksw = pltpu.roll(kc, HEAD_DIM, 1)
            vsw = pltpu.roll(vc, HEAD_DIM, 1)
            rows = slice(BLOCK + c0, BLOCK + c0 + SWA_TQ)
            for kk in range(N_KV_SWA):
                for par in range(2):
                    ksrc, vsrc = (kc, vc) if kk == par else (ksw, vsw)
                    k_scr[kk, par, rows, :] = jnp.where(halves[par], ksrc, 0.0).astype(jnp.bfloat16)
                    vt_scr[kk, par, :, rows] = jnp.where(halves[par], vsrc, 1.0).T.astype(jnp.bfloat16)

    pair1 = lax.shift_right_logical(lax.broadcasted_iota(jnp.int32, (1, SWA_COLS), 1), log2_block)
    pad_rows = lax.broadcasted_iota(jnp.int32, (2 * BLOCK, SWA_COLS), 0) < BLOCK
    first_tile = qi == 0
    sinks2 = {}
    for kk in range(N_KV_SWA):
        for par in range(2):
            sink = jnp.zeros((1, SWA_COLS), jnp.float32)
            for pi in range(PAIRS_PER_KV):
                sink = jnp.where(pair1 == pi, sink_ref[kk * 2 * PAIRS_PER_KV + 2 * pi + par] * LOG2E, sink)
            sinks2[kk, par] = sink

    def band_start(r):
        return pl.multiple_of(qi * SWA_TQ + r * BLOCK, BLOCK)

    def scores(r):
        rows = slice(r * BLOCK, (r + 1) * BLOCK)
        for kk in range(N_KV_SWA):
            qs = jnp.concatenate([q_ref[0, rows, (kk * PAIRS_PER_KV + pi) * LANES:(kk * PAIRS_PER_KV + pi + 1) * LANES]
                                  for pi in range(PAIRS_PER_KV)], axis=0)
            for par in range(2):
                st_scr[r, kk, par] = lax.dot_general(k_scr[kk, par, pl.ds(band_start(r), 2 * BLOCK), :], qs, _NT,
                                                     preferred_element_type=jnp.float32)

    def finish(r):
        rows = slice(r * BLOCK, (r + 1) * BLOCK)
        for kk in range(N_KV_SWA):
            outs = []
            for par in range(2):
                sink = sinks2[kk, par]
                st = st_scr[r, kk, par] + bias_scr[kk, par]
                if r == 0:
                    st = jnp.where(pad_rows & first_tile, NEG, st)
                m = jnp.maximum(jnp.max(st, axis=0, keepdims=True), sink)
                p = jnp.exp2(st - m).astype(jnp.bfloat16)
                o = jnp.dot(vt_scr[kk, par, :, pl.ds(band_start(r), 2 * BLOCK)], p,
                            preferred_element_type=jnp.float32)
                extra = jnp.exp2(sink - m)
                if par == 0:
                    outs.append(o[0:HEAD_DIM] / (o[HEAD_DIM:HEAD_DIM + 1] + extra))
                else:
                    outs.append(o[HEAD_DIM:LANES] / (o[0:1] + extra))
            ot = jnp.concatenate(outs, axis=0).T
            for pi in range(PAIRS_PER_KV):
                c0 = (kk * PAIRS_PER_KV + pi) * LANES
                o_ref[0, rows, c0:c0 + LANES] = ot[pi * BLOCK:(pi + 1) * BLOCK, :]

    scores(0)
    for r in range(SWA_R):
        if r + 1 < SWA_R:
            scores(r + 1)
        finish(r)


def _swa(proj3, sinks):
    b, s, _ = proj3.shape
    return pl.pallas_call(
        _swa_kernel,
        grid=(b, s // SWA_TQ),
        in_specs=[
            pl.BlockSpec(memory_space=pltpu.SMEM),
            pl.BlockSpec((1, SWA_TQ, D_SWA), lambda bi, qi: (bi, qi, COL_QA * LANES // D_SWA)),
            pl.BlockSpec((1, s, LANES), lambda bi, qi: (bi, 0, COL_KA)),
            pl.BlockSpec((1, s, LANES), lambda bi, qi: (bi, 0, COL_VA)),
        ],
        out_specs=pl.BlockSpec((1, SWA_TQ, D_SWA), lambda bi, qi: (bi, qi, 0)),
        scratch_shapes=[
            pltpu.VMEM((N_KV_SWA, 2, s + BLOCK, LANES), jnp.bfloat16),
            pltpu.VMEM((N_KV_SWA, 2, LANES, s + BLOCK), jnp.bfloat16),
            pltpu.VMEM((N_KV_SWA, 2, 2 * BLOCK, SWA_COLS), jnp.float32),
            pltpu.VMEM((SWA_R, N_KV_SWA, 2, 2 * BLOCK, SWA_COLS), jnp.float32),
        ],
        out_shape=jax.ShapeDtypeStruct((b, s, D_SWA), jnp.float32),
        compiler_params=_params("arbitrary", "arbitrary"),
        name="swa_attention",
    )(sinks, proj3, proj3, proj3)


FOX_TQ = 512
FOX_TK = FOX_TQ
FOX_PAIRS = 2
FOX_VROWS = HEAD_DIM + 16
assert all(col % FOX_PAIRS == 0 for col in (COL_QB, COL_KB, COL_VB))


def _fox_kernel(q_ref, qn_ref, k_ref, v_ref, a_ref, o_ref, ka_scr, vat_scr, qa_scr, m_scr, acc_scr, st_scr, cmax_scr):
    qi = pl.program_id(2)
    s_len = k_ref.shape[1]
    n_tiles = s_len // FOX_TQ
    half = FOX_TQ // 2
    low_k = lax.broadcasted_iota(jnp.int32, (FOX_TK, LANES), 1) < HEAD_DIM
    keeps = (low_k, jnp.logical_not(low_k))
    heads = range(2 * FOX_PAIRS)

    lane_q = lax.broadcasted_iota(jnp.int32, (FOX_TQ, LANES), 1)
    ones_hi = jnp.where((lane_q >= HEAD_DIM) & (lane_q < HEAD_DIM + N_SPLIT), 1.0, 0.0).astype(jnp.bfloat16)
    ones_lo = jnp.where(lane_q < N_SPLIT, 1.0, 0.0).astype(jnp.bfloat16)

    def build_queries(ref):
        for pair in range(FOX_PAIRS):
            q = ref[0, :, pair * LANES:(pair + 1) * LANES]
            qa_scr[2 * pair] = jnp.where(lane_q < HEAD_DIM, q, ones_hi)
            qa_scr[2 * pair + 1] = jnp.where(lane_q < HEAD_DIM, ones_lo, q)

    @pl.when(qi == 0)
    def _():
        for c0 in range(0, s_len, FOX_TK):
            for pair in range(FOX_PAIRS):
                cols = slice(pair * LANES, (pair + 1) * LANES)
                kc = k_ref[0, c0:c0 + FOX_TK, cols]
                ac = a_ref[0, c0:c0 + FOX_TK, cols]
                vt = v_ref[0, c0:c0 + FOX_TK, cols].astype(jnp.float32).T.astype(jnp.bfloat16)
                ones = jnp.ones((FOX_VROWS - HEAD_DIM, FOX_TK), jnp.bfloat16)
                for par in range(2):
                    h = 2 * pair + par
                    ka_scr[h, c0:c0 + FOX_TK, :] = jnp.where(keeps[par], kc, ac)
                vat_scr[2 * pair, :, c0:c0 + FOX_TK] = jnp.concatenate([vt[0:HEAD_DIM], ones], axis=0)
                vat_scr[2 * pair + 1, :, c0:c0 + FOX_TK] = jnp.concatenate([ones, vt[HEAD_DIM:LANES]], axis=0)
        build_queries(q_ref)

    tri = (lax.broadcasted_iota(jnp.int32, (half, half), 0)
           <= lax.broadcasted_iota(jnp.int32, (half, half), 1))

    def scores(c, slot, diagonal):
        keys = slice(c * FOX_TK, (c + 1) * FOX_TK)
        for h in heads:
            if diagonal:
                lo_keys = slice(c * FOX_TK, c * FOX_TK + half)
                st_scr[slot, h, 0:half, 0:half] = lax.dot_general(ka_scr[h, lo_keys, :], qa_scr[h, 0:half, :], _NT,
                                                                  preferred_element_type=jnp.float32)
                st_scr[slot, h, :, half:FOX_TQ] = lax.dot_general(ka_scr[h, keys, :], qa_scr[h, half:FOX_TQ, :], _NT,
                                                                  preferred_element_type=jnp.float32)
            else:
                st = lax.dot_general(ka_scr[h, keys, :], qa_scr[h], _NT,
                                     preferred_element_type=jnp.float32)
                st_scr[slot, h] = st
                cmax_scr[slot, h] = jnp.max(st, axis=0, keepdims=True)

    def accumulate(h, st, cmax, keys, cols, first):
        m_new = cmax if first else jnp.maximum(m_scr[h, :, cols], cmax)
        pt = jnp.exp2(st - m_new).astype(jnp.bfloat16)
        pv = jnp.dot(vat_scr[h, :, keys], pt, preferred_element_type=jnp.float32)
        if first:
            acc_scr[h, :, cols] = pv
        else:
            acc_scr[h, :, cols] = jnp.exp2(m_scr[h, :, cols] - m_new) * acc_scr[h, :, cols] + pv
        m_scr[h, :, cols] = m_new

    def update(c, slot, diagonal):
        k0 = c * FOX_TK
        first = c == 0
        for h in heads:
            if diagonal:
                st_lo = jnp.where(tri, st_scr[slot, h, 0:half, 0:half], NEG)
                accumulate(h, st_lo, jnp.max(st_lo, axis=0, keepdims=True), slice(k0, k0 + half), slice(0, half),
                           first)
                st_hi = jnp.concatenate([st_scr[slot, h, 0:half, half:FOX_TQ],
                                         jnp.where(tri, st_scr[slot, h, half:FOX_TK, half:FOX_TQ], NEG)], axis=0)
                accumulate(h, st_hi, jnp.max(st_hi, axis=0, keepdims=True), slice(k0, k0 + FOX_TK),
                           slice(half, FOX_TQ), first)
            else:
                accumulate(h, st_scr[slot, h], cmax_scr[slot, h], slice(k0, k0 + FOX_TK), slice(0, FOX_TQ), first)

    def tile(t):
        flip = (t * (t + 1) // 2) % 2
        slot = lambda c: (c + flip) % 2
        if t == 0:
            scores(0, slot(0), True)
        for c in range(t + 1):
            if c < t:
                scores(c + 1, slot(c + 1), c + 1 == t)
            elif t + 1 < n_tiles:
                build_queries(qn_ref)
                scores(0, slot(t + 1), False)
            update(c, slot(c), c == t)

    for t in range(n_tiles):
        pl.when(qi == t)(functools.partial(tile, t))

    for pair in range(FOX_PAIRS):
        acc0 = acc_scr[2 * pair]
        acc1 = acc_scr[2 * pair + 1]
        ot = jnp.concatenate([acc0[0:HEAD_DIM] / acc0[HEAD_DIM:HEAD_DIM + 1],
                              acc1[FOX_VROWS - HEAD_DIM:FOX_VROWS] / acc1[0:1]], axis=0)
        o_ref[0, :, pair * LANES:(pair + 1) * LANES] = ot.T


def _fox(proj3, aug):
    b, s, _ = proj3.shape
    width = FOX_PAIRS * LANES
    ngroups = D_FOX // width
    nheads = 2 * FOX_PAIRS
    n_tiles = s // FOX_TQ
    return pl.pallas_call(
        _fox_kernel,
        grid=(b, ngroups, n_tiles),
        in_specs=[
            pl.BlockSpec((1, FOX_TQ, width), lambda bi, p, qi: (bi, qi, COL_QB // FOX_PAIRS + p)),
            pl.BlockSpec((1, FOX_TQ, width),
                         lambda bi, p, qi: (bi, jnp.minimum(qi + 1, n_tiles - 1), COL_QB // FOX_PAIRS + p)),
            pl.BlockSpec((1, s, width), lambda bi, p, qi: (bi, 0, COL_KB // FOX_PAIRS + p)),
            pl.BlockSpec((1, s, width), lambda bi, p, qi: (bi, 0, COL_VB // FOX_PAIRS + p)),
            pl.BlockSpec((1, s, width), lambda bi, p, qi: (bi, 0, p)),
        ],
        out_specs=pl.BlockSpec((1, FOX_TQ, width), lambda bi, p, qi: (bi, qi, p)),
        out_shape=jax.ShapeDtypeStruct((b, s, D_FOX), jnp.float32),
        scratch_shapes=[
            pltpu.VMEM((nheads, s, LANES), jnp.bfloat16),
            pltpu.VMEM((nheads, FOX_VROWS, s), jnp.bfloat16),
            pltpu.VMEM((nheads, FOX_TQ, LANES), jnp.bfloat16),
            pltpu.VMEM((nheads, 1, FOX_TQ), jnp.float32),
            pltpu.VMEM((nheads, FOX_VROWS, FOX_TQ), jnp.float32),
            pltpu.VMEM((2, nheads, FOX_TK, FOX_TQ), jnp.float32),
            pltpu.VMEM((2, nheads, 1, FOX_TQ), jnp.float32),
        ],
        compiler_params=_params("arbitrary", "arbitrary", "arbitrary"),
        name="fox_attention",
    )(proj3, proj3, proj3, proj3, aug)


OUT_TM = 512


def _out_proj_kernel(oa_ref, ob_ref, x_ref, ga_ref, gb_ref, w_ref, gp_ref, x1_ref):
    na = _rms(oa_ref[...], ga_ref[...]).astype(jnp.bfloat16)
    nb = _rms(ob_ref[...], gb_ref[...]).astype(jnp.bfloat16)
    mix = (jnp.dot(na, w_ref[0:D_SWA, :], preferred_element_type=jnp.float32)
           + jnp.dot(nb, w_ref[D_SWA:D_MIX, :], preferred_element_type=jnp.float32))
    x1_ref[...] = x_ref[...] + _rms(mix, gp_ref[...])


def _out_proj(oa, ob, x2d, ga, gb, w_out, gp):
    m = x2d.shape[0]
    row = lambda i: (i, 0)
    fixed = lambda i: (0, 0)
    return pl.pallas_call(
        _out_proj_kernel,
        grid=(m // OUT_TM,),
        in_specs=[
            pl.BlockSpec((OUT_TM, D_SWA), row),
            pl.BlockSpec((OUT_TM, D_FOX), row),
            pl.BlockSpec((OUT_TM, D_MODEL), row),
            pl.BlockSpec((1, D_SWA), fixed),
            pl.BlockSpec((1, D_FOX), fixed),
            pl.BlockSpec((D_MIX, D_MODEL), fixed, pipeline_mode=pl.Buffered(1)),
            pl.BlockSpec((1, D_MODEL), fixed),
        ],
        out_specs=pl.BlockSpec((OUT_TM, D_MODEL), row),
        out_shape=jax.ShapeDtypeStruct((m, D_MODEL), jnp.float32),
        compiler_params=_params("arbitrary"),
        name="out_proj",
    )(oa, ob, x2d, ga, gb, w_out, gp)


FFN_TM = 512
FFN_TF = 512
CARRY = 8
ROW_CHUNK = 32
N_FT = D_FF // FFN_TF


def _gelu_tanh(x):
    return 0.5 * x * (1.0 + jnp.tanh(np.sqrt(2.0 / np.pi) * (x + 0.044715 * (x * x * x))))


def _ffn_kernel(x_ref, g_ref, wg_ref, wv_ref, cwg_ref, cwv_ref, cbg_ref, cbv_ref, wd_ref, gp_ref,
                o_ref, h_scr, ug_scr, uv_scr, tail_scr, *, tiles_per_seq):
    i = pl.program_id(0)
    j = pl.program_id(1)

    @pl.when(j == 0)
    def _():
        h_scr[...] = _rms(x_ref[...], g_ref[...]).astype(jnp.bfloat16)
        o_ref[...] = jnp.zeros_like(o_ref)

    @pl.when((i == 0) & (j == 0))
    def _():
        tail_scr[...] = jnp.zeros_like(tail_scr)

    h = h_scr[...]
    seq_start = i % tiles_per_seq == 0
    for k, (u_scr, w_ref) in enumerate(((ug_scr, wg_ref), (uv_scr, wv_ref))):
        u_scr[0:CARRY, :] = jnp.where(seq_start, 0.0, tail_scr[j, k])
        u_scr[CARRY:, :] = jnp.dot(h, w_ref[...], preferred_element_type=jnp.float32)
        tail_scr[j, k] = u_scr[FFN_TM:FFN_TM + CARRY, :]

    def conv(u_scr, cw_ref, cb_ref):
        y = cb_ref[...]
        for kk in range(CONV_WIDTH):
            y = y + u_scr[pl.ds(CARRY - (CONV_WIDTH - 1) + kk, FFN_TM), :] * cw_ref[kk:kk + 1, :]
        return y

    gate = conv(ug_scr, cwg_ref, cbg_ref)
    val = conv(uv_scr, cwv_ref, cbv_ref)
    a = (_gelu_tanh(gate) * val).astype(jnp.bfloat16)
    o_ref[...] += jnp.dot(a, wd_ref[...], preferred_element_type=jnp.float32)

    @pl.when(j == N_FT - 1)
    def _():
        gp = gp_ref[...]
        for r0 in range(0, FFN_TM, ROW_CHUNK):
            rows = slice(r0, r0 + ROW_CHUNK)
            o_ref[rows, :] = x_ref[rows, :] + _rms(o_ref[rows, :], gp)


def _ffn(x1, g, w_up, conv_w, conv_b, w_down, gp, seq_len):
    m = x1.shape[0]
    kern = functools.partial(_ffn_kernel, tiles_per_seq=seq_len // FFN_TM)
    return pl.pallas_call(
        kern,
        grid=(m // FFN_TM, N_FT),
        in_specs=[
            pl.BlockSpec((FFN_TM, D_MODEL), lambda i, j: (i, 0)),
            pl.BlockSpec((1, D_MODEL), lambda i, j: (0, 0)),
            pl.BlockSpec((D_MODEL, FFN_TF), lambda i, j: (0, j)),
            pl.BlockSpec((D_MODEL, FFN_TF), lambda i, j: (0, j + N_FT)),
            pl.BlockSpec((CONV_WIDTH, FFN_TF), lambda i, j: (0, j)),
            pl.BlockSpec((CONV_WIDTH, FFN_TF), lambda i, j: (0, j + N_FT)),
            pl.BlockSpec((1, FFN_TF), lambda i, j: (0, j)),
            pl.BlockSpec((1, FFN_TF), lambda i, j: (0, j + N_FT)),
            pl.BlockSpec((FFN_TF, D_MODEL), lambda i, j: (j, 0)),
            pl.BlockSpec((1, D_MODEL), lambda i, j: (0, 0)),
        ],
        out_specs=pl.BlockSpec((FFN_TM, D_MODEL), lambda i, j: (i, 0)),
        out_shape=jax.ShapeDtypeStruct((m, D_MODEL), jnp.float32),
        scratch_shapes=[
            pltpu.VMEM((FFN_TM, D_MODEL), jnp.bfloat16),
            pltpu.VMEM((FFN_TM + CARRY, FFN_TF), jnp.float32),
            pltpu.VMEM((FFN_TM + CARRY, FFN_TF), jnp.float32),
            pltpu.VMEM((N_FT, 2, CARRY, FFN_TF), jnp.float32),
        ],
        compiler_params=_params("arbitrary", "arbitrary"),
        name="conv_geglu_ffn",
    )(x1, g, w_up, w_up, conv_w, conv_w, conv_b, conv_b, w_down, gp)


def kernel(x, pre_mix_g, w_in, b_forget, sinks, grp_swa_g, grp_fox_g, w_out, post_mix_g,
           pre_ffn_g, w_up, conv_w, conv_b, w_down, post_ffn_g):
    b, s, d = x.shape
    depth = w_in.shape[0]
    xf = x.reshape(b * s, d)
    for l in range(depth):
        w_in_bf = _cast_pad_rows(jnp.swapaxes(w_in[l], 0, 1), D_IN_PAD)
        b_f = jnp.pad(b_forget[l], (0, LANES - N_HEADS_FOX)).reshape(1, LANES)

        proj, logf, w_out_bf, w_up_bf, w_down_bf = _in_proj(xf, pre_mix_g[l].reshape(1, d), w_in_bf, b_f,
                                                            (w_out[l], w_up[l], w_down[l]))
        proj3 = proj.reshape(b, s, D_QKV)
        aug = _cumsum(logf.reshape(b, s, LANES))

        o_a = _swa(proj3, sinks[l])
        o_b = _fox(proj3, aug)

        x1 = _out_proj(o_a.reshape(b * s, D_SWA), o_b.reshape(b * s, D_FOX), xf,
                       grp_swa_g[l].reshape(1, D_SWA), grp_fox_g[l].reshape(1, D_FOX),
                       w_out_bf, post_mix_g[l].reshape(1, d))
        xf = _ffn(x1, pre_ffn_g[l].reshape(1, d), w_up_bf, conv_w[l],
                  conv_b[l].reshape(1, 2 * D_FF), w_down_bf, post_ffn_g[l].reshape(1, d), s)
    return xf.reshape(b, s, d)
```

```python
import functools

import numpy as np
import jax
import jax.numpy as jnp
from jax import lax
from jax.experimental import pallas as pl
from jax.experimental.pallas import tpu as pltpu

D_MODEL = 2048
HEAD_DIM = 64
N_HEADS_SWA = 16
N_KV_SWA = 2
N_HEADS_FOX = 16
WINDOW = 128
BLOCK = 128
D_FF = 5632
CONV_WIDTH = 3
EPS = 1e-6
D_SWA = N_HEADS_SWA * HEAD_DIM
D_KV_SWA = N_KV_SWA * HEAD_DIM
D_FOX = N_HEADS_FOX * HEAD_DIM
D_MIX = D_SWA + D_FOX
D_QKV = D_SWA + 2 * D_KV_SWA + 3 * D_FOX

LANES = 128
COL_QA = 0
COL_KA = COL_QA + D_SWA // LANES
COL_VA = COL_KA + 1
COL_QB = COL_VA + 1
COL_KB = COL_QB + D_FOX // LANES
COL_VB = COL_KB + D_FOX // LANES
D_IN_PAD = D_QKV + LANES

NEG = -1e30
VMEM_LIMIT = 56 * 1024 * 1024

ALIBI_SLOPES = [float(v) for v in np.asarray(2.0 ** (-8.0 * np.arange(1, N_HEADS_SWA + 1) / N_HEADS_SWA),
                                            dtype=np.float32)]

_NT = (((1,), (1,)), ((), ()))


def _rms(xf, g):
    return xf * lax.rsqrt(jnp.mean(xf * xf, axis=-1, keepdims=True) + EPS) * g


def _params(*sem):
    return pltpu.CompilerParams(dimension_semantics=sem, vmem_limit_bytes=VMEM_LIMIT)


CAST_TM = 5 * LANES


def _cast_pad_kernel(w_ref, o_ref, *, valid_rows):
    row = pl.program_id(0) * CAST_TM + lax.broadcasted_iota(jnp.int32, w_ref.shape, 0)
    o_ref[...] = jnp.where(row < valid_rows, w_ref[...], 0.0).astype(jnp.bfloat16)


def _cast_pad_rows(w, padded_rows):
    rows, cols = w.shape
    assert padded_rows % CAST_TM == 0 and padded_rows - rows < CAST_TM
    return pl.pallas_call(
        functools.partial(_cast_pad_kernel, valid_rows=rows),
        grid=(padded_rows // CAST_TM,),
        in_specs=[pl.BlockSpec((CAST_TM, cols), lambda j: (j, 0))],
        out_specs=pl.BlockSpec((CAST_TM, cols), lambda j: (j, 0)),
        out_shape=jax.ShapeDtypeStruct((padded_rows, cols), jnp.bfloat16),
        compiler_params=_params("arbitrary"),
        name="cast_w_in",
    )(w)


IN_TM = 512
IN_CHUNK = 1024
LOG2E = float(np.log2(np.e))
Q_MULT = LOG2E / float(np.sqrt(HEAD_DIM))


Q_RANGES = ((COL_QA * LANES, COL_QA * LANES + D_SWA), (COL_QB * LANES, COL_QB * LANES + D_FOX))


def _in_proj_kernel(x_ref, g_ref, w_ref, wf_ref, bf_ref, *refs):
    n_side = (len(refs) - 2) // 2
    side_in, (proj_ref, logf_ref), side_out = refs[:n_side], refs[n_side:n_side + 2], refs[n_side + 2:]
    for src, dst in zip(side_in, side_out):
        dst[...] = src[...].astype(jnp.bfloat16)
    h = _rms(x_ref[...], g_ref[...]).astype(jnp.bfloat16)
    for n0 in range(0, D_QKV, IN_CHUNK):
        n1 = min(n0 + IN_CHUNK, D_QKV)
        acc = lax.dot_general(h, w_ref[n0:n1, :], _NT, preferred_element_type=jnp.float32)
        overlaps = [(max(lo, n0), min(hi, n1)) for lo, hi in Q_RANGES if max(lo, n0) < min(hi, n1)]
        if overlaps:
            col = n0 + lax.broadcasted_iota(jnp.int32, (1, n1 - n0), 1)
            mult = jnp.ones((1, n1 - n0), jnp.float32)
            for lo, hi in overlaps:
                mult = jnp.where((col >= lo) & (col < hi), Q_MULT, mult)
            acc = acc * mult
        proj_ref[:, n0:n1] = acc.astype(jnp.bfloat16)
    f = lax.dot_general(h, wf_ref[...], _NT, preferred_element_type=jnp.float32) + bf_ref[...]
    logf_ref[...] = jnp.minimum(f, 0.0) - jnp.log1p(jnp.exp(-jnp.abs(f)))


def _in_proj(x2d, g, w_in_bf, b_f, side_weights):
    m = x2d.shape[0]
    steps = m // IN_TM
    side_specs = []
    for w in side_weights:
        rows, cols = w.shape
        assert rows % (steps * 16) == 0
        side_specs.append(pl.BlockSpec((rows // steps, cols), lambda i: (i, 0)))
    return pl.pallas_call(
        _in_proj_kernel,
        grid=(steps,),
        in_specs=[
            pl.BlockSpec((IN_TM, D_MODEL), lambda i: (i, 0)),
            pl.BlockSpec((1, D_MODEL), lambda i: (0, 0)),
            pl.BlockSpec((D_QKV, D_MODEL), lambda i: (0, 0), pipeline_mode=pl.Buffered(1)),
            pl.BlockSpec((LANES, D_MODEL), lambda i: (D_QKV // LANES, 0), pipeline_mode=pl.Buffered(1)),
            pl.BlockSpec((1, LANES), lambda i: (0, 0)),
        ] + side_specs,
        out_specs=[
            pl.BlockSpec((IN_TM, D_QKV), lambda i: (i, 0)),
            pl.BlockSpec((IN_TM, LANES), lambda i: (i, 0)),
        ] + side_specs,
        out_shape=[
            jax.ShapeDtypeStruct((m, D_QKV), jnp.bfloat16),
            jax.ShapeDtypeStruct((m, LANES), jnp.float32),
        ] + [jax.ShapeDtypeStruct(w.shape, jnp.bfloat16) for w in side_weights],
        compiler_params=_params("arbitrary"),
        name="in_proj",
    )(x2d, g, w_in_bf, w_in_bf, b_f, *side_weights)


CS_BLK = 128
N_SPLIT = 3


def _bias_placement():
    place = np.zeros((N_SPLIT, LANES, D_FOX), np.float32)
    for h in range(N_HEADS_FOX):
        base = (h // 2) * LANES + (HEAD_DIM if h % 2 == 0 else 0)
        for i in range(N_SPLIT):
            place[i, h, base + i] = 1.0
    return place


def _cumsum_kernel(logf_ref, place_ref, a_ref, w_scr):
    s = logf_ref.shape[1]
    nblk = s // CS_BLK
    exact = dict(precision=lax.Precision.HIGHEST, preferred_element_type=jnp.float32)
    r = lax.broadcasted_iota(jnp.int32, (CS_BLK, CS_BLK), 0)
    c = lax.broadcasted_iota(jnp.int32, (CS_BLK, CS_BLK), 1)
    lower = (r >= c).astype(jnp.float32)
    blocks = [slice(b * CS_BLK, (b + 1) * CS_BLK) for b in range(nblk)]
    for rows in blocks:
        w_scr[rows, :] = jnp.dot(lower, logf_ref[0, rows, :], **exact)
    totals = w_scr[pl.ds(CS_BLK - 1, nblk, stride=CS_BLK), :]
    rb = lax.broadcasted_iota(jnp.int32, (nblk, nblk), 0)
    cb = lax.broadcasted_iota(jnp.int32, (nblk, nblk), 1)
    carry = jnp.dot((rb > cb).astype(jnp.float32), totals, **exact)
    for b, rows in enumerate(blocks):
        rest = (w_scr[rows, :] + carry[b:b + 1, :]) * (-LOG2E)
        parts = []
        for _ in range(N_SPLIT):
            part = rest.astype(jnp.bfloat16)
            rest = rest - part.astype(jnp.float32)
            parts.append(part)
        a_ref[0, rows, :] = jnp.dot(jnp.concatenate(parts, axis=1), place_ref[...],
                                    preferred_element_type=jnp.float32).astype(jnp.bfloat16)


def _cumsum(logf3):
    b, s, _ = logf3.shape
    place = jnp.asarray(_bias_placement().reshape(N_SPLIT * LANES, D_FOX), jnp.bfloat16)
    return pl.pallas_call(
        _cumsum_kernel,
        grid=(b,),
        in_specs=[pl.BlockSpec((1, s, LANES), lambda i: (i, 0, 0)),
                  pl.BlockSpec((N_SPLIT * LANES, D_FOX), lambda i: (0, 0))],
        out_specs=pl.BlockSpec((1, s, D_FOX), lambda i: (i, 0, 0)),
        out_shape=jax.ShapeDtypeStruct((b, s, D_FOX), jnp.bfloat16),
        scratch_shapes=[pltpu.VMEM((s, LANES), jnp.float32)],
        compiler_params=_params("arbitrary"),
        name="cumsum_logf",
    )(logf3, place)


SWA_TQ = 512
PAIRS_PER_KV = (N_HEADS_SWA // N_KV_SWA) // 2


SWA_R = SWA_TQ // BLOCK
SWA_COLS = PAIRS_PER_KV * BLOCK


def _swa_kernel(sink_ref, q_ref, qn_ref, k_ref, v_ref, o_ref, k_scr, vt_scr, bias_scr, st_scr):
    bi = pl.program_id(0)
    qi = pl.program_id(1)
    s_len = k_ref.shape[1]
    log2_block = int(np.log2(BLOCK))

    @pl.when((bi == 0) & (qi == 0))
    def _():
        key_loc = lax.broadcasted_iota(jnp.int32, (2 * BLOCK, SWA_COLS), 0)
        col = lax.broadcasted_iota(jnp.int32, (2 * BLOCK, SWA_COLS), 1)
        dist = BLOCK + (col & (BLOCK - 1)) - key_loc
        pair = lax.shift_right_logical(col, log2_block)
        distf = dist.astype(jnp.float32)
        in_win = (dist >= 0) & (dist < WINDOW)
        for kk in range(N_KV_SWA):
            for par in range(2):
                slope = jnp.zeros((2 * BLOCK, SWA_COLS), jnp.float32)
                for pi in range(PAIRS_PER_KV):
                    h = kk * 2 * PAIRS_PER_KV + 2 * pi + par
                    slope = jnp.where(pair == pi, ALIBI_SLOPES[h] * LOG2E, slope)
                bias_scr[kk, par] = jnp.where(in_win, -slope * distf, NEG)

    @pl.when(qi == 0)
    def _():
        low = lax.broadcasted_iota(jnp.int32, (SWA_TQ, LANES), 1) < HEAD_DIM
        halves = (low, jnp.logical_not(low))
        zero_k = jnp.zeros((BLOCK, LANES), jnp.bfloat16)
        zero_v = jnp.zeros((LANES, BLOCK), jnp.bfloat16)
        for kk in range(N_KV_SWA):
            for par in range(2):
                k_scr[kk, par, 0:BLOCK, :] = zero_k
                vt_scr[kk, par, :, 0:BLOCK] = zero_v
        for c0 in range(0, s_len, SWA_TQ):
            kc = k_ref[0, c0:c0 + SWA_TQ, :].astype(jnp.float32)
            vc = v_ref[0, c0:c0 + SWA_TQ, :].astype(jnp.float32)
            ksw = pltpu.roll(kc, HEAD_DIM, 1)
            vsw = pltpu.roll(vc, HEAD_DIM, 1)
            rows = slice(BLOCK + c0, BLOCK + c0 + SWA_TQ)
            for kk in range(N_KV_SWA):
                for par in range(2):
                    ksrc, vsrc = (kc, vc) if kk == par else (ksw, vsw)
                    k_scr[kk, par, rows, :] = jnp.where(halves[par], ksrc, 0.0).astype(jnp.bfloat16)
                    vt_scr[kk, par, :, rows] = jnp.where(halves[par], vsrc, 1.0).T.astype(jnp.bfloat16)

    pair1 = lax.shift_right_logical(lax.broadcasted_iota(jnp.int32, (1, SWA_COLS), 1), log2_block)
    pad_rows = lax.broadcasted_iota(jnp.int32, (2 * BLOCK, SWA_COLS), 0) < BLOCK
    first_tile = qi == 0
    sinks2 = {}
    for kk in range(N_KV_SWA):
        for par in range(2):
            sink = jnp.zeros((1, SWA_COLS), jnp.float32)
            for pi in range(PAIRS_PER_KV):
                sink = jnp.where(pair1 == pi, sink_ref[kk * 2 * PAIRS_PER_KV + 2 * pi + par] * LOG2E, sink)
            sinks2[kk, par] = sink

    def band_start(r, tile=qi):
        return pl.multiple_of(tile * SWA_TQ + r * BLOCK, BLOCK)

    def scores(r, queries=q_ref, tile=qi):
        rows = slice(r * BLOCK, (r + 1) * BLOCK)
        for kk in range(N_KV_SWA):
            qs = jnp.concatenate([queries[0, rows, (kk * PAIRS_PER_KV + pi) * LANES:(kk * PAIRS_PER_KV + pi + 1) * LANES]
                                  for pi in range(PAIRS_PER_KV)], axis=0)
            for par in range(2):
                st_scr[r, kk, par] = lax.dot_general(k_scr[kk, par, pl.ds(band_start(r, tile), 2 * BLOCK), :], qs,
                                                     _NT, preferred_element_type=jnp.float32)

    def finish(r):
        rows = slice(r * BLOCK, (r + 1) * BLOCK)
        for kk in range(N_KV_SWA):
            outs = []
            for par in range(2):
                sink = sinks2[kk, par]
                st = st_scr[r, kk, par] + bias_scr[kk, par]
                if r == 0:
                    st = jnp.where(pad_rows & first_tile, NEG, st)
                m = jnp.maximum(jnp.max(st, axis=0, keepdims=True), sink)
                p = jnp.exp2(st - m).astype(jnp.bfloat16)
                o = jnp.dot(vt_scr[kk, par, :, pl.ds(band_start(r), 2 * BLOCK)], p,
                            preferred_element_type=jnp.float32)
                extra = jnp.exp2(sink - m)
                if par == 0:
                    outs.append(o[0:HEAD_DIM] / (o[HEAD_DIM:HEAD_DIM + 1] + extra))
                else:
                    outs.append(o[HEAD_DIM:LANES] / (o[0:1] + extra))
            ot = jnp.concatenate(outs, axis=0).T
            for pi in range(PAIRS_PER_KV):
                c0 = (kk * PAIRS_PER_KV + pi) * LANES
                o_ref[0, rows, c0:c0 + LANES] = ot[pi * BLOCK:(pi + 1) * BLOCK, :]

    @pl.when(qi == 0)
    def _():
        scores(0)

    last_tile = s_len // SWA_TQ - 1
    for r in range(SWA_R):
        if r + 1 < SWA_R:
            scores(r + 1)
        else:
            scores(0, qn_ref, jnp.minimum(qi + 1, last_tile))
        finish(r)


def _swa(proj3, sinks):
    b, s, _ = proj3.shape
    return pl.pallas_call(
        _swa_kernel,
        grid=(b, s // SWA_TQ),
        in_specs=[
            pl.BlockSpec(memory_space=pltpu.SMEM),
            pl.BlockSpec((1, SWA_TQ, D_SWA), lambda bi, qi: (bi, qi, COL_QA * LANES // D_SWA)),
            pl.BlockSpec((1, SWA_TQ, D_SWA),
                         lambda bi, qi: (bi, jnp.minimum(qi + 1, s // SWA_TQ - 1), COL_QA * LANES // D_SWA)),
            pl.BlockSpec((1, s, LANES), lambda bi, qi: (bi, 0, COL_KA)),
            pl.BlockSpec((1, s, LANES), lambda bi, qi: (bi, 0, COL_VA)),
        ],
        out_specs=pl.BlockSpec((1, SWA_TQ, D_SWA), lambda bi, qi: (bi, qi, 0)),
        scratch_shapes=[
            pltpu.VMEM((N_KV_SWA, 2, s + BLOCK, LANES), jnp.bfloat16),
            pltpu.VMEM((N_KV_SWA, 2, LANES, s + BLOCK), jnp.bfloat16),
            pltpu.VMEM((N_KV_SWA, 2, 2 * BLOCK, SWA_COLS), jnp.float32),
            pltpu.VMEM((SWA_R, N_KV_SWA, 2, 2 * BLOCK, SWA_COLS), jnp.float32),
        ],
        out_shape=jax.ShapeDtypeStruct((b, s, D_SWA), jnp.float32),
        compiler_params=_params("arbitrary", "arbitrary"),
        name="swa_attention",
    )(sinks, proj3, proj3, proj3, proj3)


FOX_TQ = 512
FOX_TK = FOX_TQ
FOX_PAIRS = 2
FOX_VROWS = HEAD_DIM + 16
assert all(col % FOX_PAIRS == 0 for col in (COL_QB, COL_KB, COL_VB))


def _fox_kernel(q_ref, qn_ref, k_ref, v_ref, a_ref, o_ref, ka_scr, vat_scr, qa_scr, m_scr, acc_scr, st_scr, cmax_scr):
    qi = pl.program_id(2)
    s_len = k_ref.shape[1]
    n_tiles = s_len // FOX_TQ
    half = FOX_TQ // 2
    low_k = lax.broadcasted_iota(jnp.int32, (FOX_TK, LANES), 1) < HEAD_DIM
    keeps = (low_k, jnp.logical_not(low_k))
    heads = range(2 * FOX_PAIRS)

    lane_q = lax.broadcasted_iota(jnp.int32, (FOX_TQ, LANES), 1)
    ones_hi = jnp.where((lane_q >= HEAD_DIM) & (lane_q < HEAD_DIM + N_SPLIT), 1.0, 0.0).astype(jnp.bfloat16)
    ones_lo = jnp.where(lane_q < N_SPLIT, 1.0, 0.0).astype(jnp.bfloat16)

    def build_queries(ref):
        for pair in range(FOX_PAIRS):
            q = ref[0, :, pair * LANES:(pair + 1) * LANES]
            qa_scr[2 * pair] = jnp.where(lane_q < HEAD_DIM, q, ones_hi)
            qa_scr[2 * pair + 1] = jnp.where(lane_q < HEAD_DIM, ones_lo, q)

    @pl.when(qi == 0)
    def _():
        for c0 in range(0, s_len, FOX_TK):
            for pair in range(FOX_PAIRS):
                cols = slice(pair * LANES, (pair + 1) * LANES)
                kc = k_ref[0, c0:c0 + FOX_TK, cols]
                ac = a_ref[0, c0:c0 + FOX_TK, cols]
                vt = v_ref[0, c0:c0 + FOX_TK, cols].astype(jnp.float32).T.astype(jnp.bfloat16)
                ones = jnp.ones((FOX_VROWS - HEAD_DIM, FOX_TK), jnp.bfloat16)
                for par in range(2):
                    h = 2 * pair + par
                    ka_scr[h, c0:c0 + FOX_TK, :] = jnp.where(keeps[par], kc, ac)
                vat_scr[2 * pair, :, c0:c0 + FOX_TK] = jnp.concatenate([vt[0:HEAD_DIM], ones], axis=0)
                vat_scr[2 * pair + 1, :, c0:c0 + FOX_TK] = jnp.concatenate([ones, vt[HEAD_DIM:LANES]], axis=0)
        build_queries(q_ref)

    tri = (lax.broadcasted_iota(jnp.int32, (half, half), 0)
           <= lax.broadcasted_iota(jnp.int32, (half, half), 1))

    def scores(c, slot, diagonal):
        keys = slice(c * FOX_TK, (c + 1) * FOX_TK)
        for h in heads:
            if diagonal:
                lo_keys = slice(c * FOX_TK, c * FOX_TK + half)
                st_scr[slot, h, 0:half, 0:half] = lax.dot_general(ka_scr[h, lo_keys, :], qa_scr[h, 0:half, :], _NT,
                                                                  preferred_element_type=jnp.float32)
                st_scr[slot, h, :, half:FOX_TQ] = lax.dot_general(ka_scr[h, keys, :], qa_scr[h, half:FOX_TQ, :], _NT,
                                                                  preferred_element_type=jnp.float32)
            else:
                st = lax.dot_general(ka_scr[h, keys, :], qa_scr[h], _NT,
                                     preferred_element_type=jnp.float32)
                st_scr[slot, h] = st
                cmax_scr[slot, h] = jnp.max(st, axis=0, keepdims=True)

    def accumulate(h, st, cmax, keys, cols, first):
        m_new = cmax if first else jnp.maximum(m_scr[h, :, cols], cmax)
        pt = jnp.exp2(st - m_new).astype(jnp.bfloat16)
        pv = jnp.dot(vat_scr[h, :, keys], pt, preferred_element_type=jnp.float32)
        if first:
            acc_scr[h, :, cols] = pv
        else:
            acc_scr[h, :, cols] = jnp.exp2(m_scr[h, :, cols] - m_new) * acc_scr[h, :, cols] + pv
        m_scr[h, :, cols] = m_new

    def update(c, slot, diagonal):
        k0 = c * FOX_TK
        first = c == 0
        for h in heads:
            if diagonal:
                st_lo = jnp.where(tri, st_scr[slot, h, 0:half, 0:half], NEG)
                accumulate(h, st_lo, jnp.max(st_lo, axis=0, keepdims=True), slice(k0, k0 + half), slice(0, half),
                           first)
                st_hi = jnp.concatenate([st_scr[slot, h, 0:half, half:FOX_TQ],
                                         jnp.where(tri, st_scr[slot, h, half:FOX_TK, half:FOX_TQ], NEG)], axis=0)
                accumulate(h, st_hi, jnp.max(st_hi, axis=0, keepdims=True), slice(k0, k0 + FOX_TK),
                           slice(half, FOX_TQ), first)
            else:
                accumulate(h, st_scr[slot, h], cmax_scr[slot, h], slice(k0, k0 + FOX_TK), slice(0, FOX_TQ), first)

    def tile(t):
        flip = (t * (t + 1) // 2) % 2
        slot = lambda c: (c + flip) % 2
        if t == 0:
            scores(0, slot(0), True)
        for c in range(t + 1):
            if c < t:
                scores(c + 1, slot(c + 1), c + 1 == t)
            elif t + 1 < n_tiles:
                build_queries(qn_ref)
                scores(0, slot(t + 1), False)
            update(c, slot(c), c == t)

    for t in range(n_tiles):
        pl.when(qi == t)(functools.partial(tile, t))

    for pair in range(FOX_PAIRS):
        acc0 = acc_scr[2 * pair]
        acc1 = acc_scr[2 * pair + 1]
        ot = jnp.concatenate([acc0[0:HEAD_DIM] / acc0[HEAD_DIM:HEAD_DIM + 1],
                              acc1[FOX_VROWS - HEAD_DIM:FOX_VROWS] / acc1[0:1]], axis=0)
        o_ref[0, :, pair * LANES:(pair + 1) * LANES] = ot.T


def _fox(proj3, aug):
    b, s, _ = proj3.shape
    width = FOX_PAIRS * LANES
    ngroups = D_FOX // width
    nheads = 2 * FOX_PAIRS
    n_tiles = s // FOX_TQ
    return pl.pallas_call(
        _fox_kernel,
        grid=(b, ngroups, n_tiles),
        in_specs=[
            pl.BlockSpec((1, FOX_TQ, width), lambda bi, p, qi: (bi, qi, COL_QB // FOX_PAIRS + p)),
            pl.BlockSpec((1, FOX_TQ, width),
                         lambda bi, p, qi: (bi, jnp.minimum(qi + 1, n_tiles - 1), COL_QB // FOX_PAIRS + p)),
            pl.BlockSpec((1, s, width), lambda bi, p, qi: (bi, 0, COL_KB // FOX_PAIRS + p)),
            pl.BlockSpec((1, s, width), lambda bi, p, qi: (bi, 0, COL_VB // FOX_PAIRS + p)),
            pl.BlockSpec((1, s, width), lambda bi, p, qi: (bi, 0, p)),
        ],
        out_specs=pl.BlockSpec((1, FOX_TQ, width), lambda bi, p, qi: (bi, qi, p)),
        out_shape=jax.ShapeDtypeStruct((b, s, D_FOX), jnp.float32),
        scratch_shapes=[
            pltpu.VMEM((nheads, s, LANES), jnp.bfloat16),
            pltpu.VMEM((nheads, FOX_VROWS, s), jnp.bfloat16),
            pltpu.VMEM((nheads, FOX_TQ, LANES), jnp.bfloat16),
            pltpu.VMEM((nheads, 1, FOX_TQ), jnp.float32),
            pltpu.VMEM((nheads, FOX_VROWS, FOX_TQ), jnp.float32),
            pltpu.VMEM((2, nheads, FOX_TK, FOX_TQ), jnp.float32),
            pltpu.VMEM((2, nheads, 1, FOX_TQ), jnp.float32),
        ],
        compiler_params=_params("arbitrary", "arbitrary", "arbitrary"),
        name="fox_attention",
    )(proj3, proj3, proj3, proj3, aug)


OUT_TM = 512


def _out_proj_kernel(oa_ref, ob_ref, x_ref, ga_ref, gb_ref, w_ref, gp_ref, x1_ref):
    na = _rms(oa_ref[...], ga_ref[...]).astype(jnp.bfloat16)
    nb = _rms(ob_ref[...], gb_ref[...]).astype(jnp.bfloat16)
    mix = (jnp.dot(na, w_ref[0:D_SWA, :], preferred_element_type=jnp.float32)
           + jnp.dot(nb, w_ref[D_SWA:D_MIX, :], preferred_element_type=jnp.float32))
    x1_ref[...] = x_ref[...] + _rms(mix, gp_ref[...])


def _out_proj(oa, ob, x2d, ga, gb, w_out, gp):
    m = x2d.shape[0]
    row = lambda i: (i, 0)
    fixed = lambda i: (0, 0)
    return pl.pallas_call(
        _out_proj_kernel,
        grid=(m // OUT_TM,),
        in_specs=[
            pl.BlockSpec((OUT_TM, D_SWA), row),
            pl.BlockSpec((OUT_TM, D_FOX), row),
            pl.BlockSpec((OUT_TM, D_MODEL), row),
            pl.BlockSpec((1, D_SWA), fixed),
            pl.BlockSpec((1, D_FOX), fixed),
            pl.BlockSpec((D_MIX, D_MODEL), fixed, pipeline_mode=pl.Buffered(1)),
            pl.BlockSpec((1, D_MODEL), fixed),
        ],
        out_specs=pl.BlockSpec((OUT_TM, D_MODEL), row),
        out_shape=jax.ShapeDtypeStruct((m, D_MODEL), jnp.float32),
        compiler_params=_params("arbitrary"),
        name="out_proj",
    )(oa, ob, x2d, ga, gb, w_out, gp)


FFN_TM = 512
FFN_TF = 512
CARRY = 8
ROW_CHUNK = 32
N_FT = D_FF // FFN_TF


def _gelu_tanh(x):
    return 0.5 * x * (1.0 + jnp.tanh(np.sqrt(2.0 / np.pi) * (x + 0.044715 * (x * x * x))))


def _ffn_kernel(x_ref, g_ref, wg_ref, wv_ref, cwg_ref, cwv_ref, cbg_ref, cbv_ref, wd_ref, gp_ref,
                o_ref, h_scr, ug_scr, uv_scr, tail_scr, *, tiles_per_seq):
    i = pl.program_id(0)
    j = pl.program_id(1)

    @pl.when(j == 0)
    def _():
        h_scr[...] = _rms(x_ref[...], g_ref[...]).astype(jnp.bfloat16)
        o_ref[...] = jnp.zeros_like(o_ref)

    @pl.when((i == 0) & (j == 0))
    def _():
        tail_scr[...] = jnp.zeros_like(tail_scr)

    h = h_scr[...]
    seq_start = i % tiles_per_seq == 0
    for k, (u_scr, w_ref) in enumerate(((ug_scr, wg_ref), (uv_scr, wv_ref))):
        u_scr[0:CARRY, :] = jnp.where(seq_start, 0.0, tail_scr[j, k])
        u_scr[CARRY:, :] = jnp.dot(h, w_ref[...], preferred_element_type=jnp.float32)
        tail_scr[j, k] = u_scr[FFN_TM:FFN_TM + CARRY, :]

    def conv(u_scr, cw_ref, cb_ref):
        y = cb_ref[...]
        for kk in range(CONV_WIDTH):
            y = y + u_scr[pl.ds(CARRY - (CONV_WIDTH - 1) + kk, FFN_TM), :] * cw_ref[kk:kk + 1, :]
        return y

    gate = conv(ug_scr, cwg_ref, cbg_ref)
    val = conv(uv_scr, cwv_ref, cbv_ref)
    a = (_gelu_tanh(gate) * val).astype(jnp.bfloat16)
    o_ref[...] += jnp.dot(a, wd_ref[...], preferred_element_type=jnp.float32)

    @pl.when(j == N_FT - 1)
    def _():
        gp = gp_ref[...]
        for r0 in range(0, FFN_TM, ROW_CHUNK):
            rows = slice(r0, r0 + ROW_CHUNK)
            o_ref[rows, :] = x_ref[rows, :] + _rms(o_ref[rows, :], gp)


def _ffn(x1, g, w_up, conv_w, conv_b, w_down, gp, seq_len):
    m = x1.shape[0]
    kern = functools.partial(_ffn_kernel, tiles_per_seq=seq_len // FFN_TM)
    return pl.pallas_call(
        kern,
        grid=(m // FFN_TM, N_FT),
        in_specs=[
            pl.BlockSpec((FFN_TM, D_MODEL), lambda i, j: (i, 0)),
            pl.BlockSpec((1, D_MODEL), lambda i, j: (0, 0)),
            pl.BlockSpec((D_MODEL, FFN_TF), lambda i, j: (0, j)),
            pl.BlockSpec((D_MODEL, FFN_TF), lambda i, j: (0, j + N_FT)),
            pl.BlockSpec((CONV_WIDTH, FFN_TF), lambda i, j: (0, j)),
            pl.BlockSpec((CONV_WIDTH, FFN_TF), lambda i, j: (0, j + N_FT)),
            pl.BlockSpec((1, FFN_TF), lambda i, j: (0, j)),
            pl.BlockSpec((1, FFN_TF), lambda i, j: (0, j + N_FT)),
            pl.BlockSpec((FFN_TF, D_MODEL), lambda i, j: (j, 0)),
            pl.BlockSpec((1, D_MODEL), lambda i, j: (0, 0)),
        ],
        out_specs=pl.BlockSpec((FFN_TM, D_MODEL), lambda i, j: (i, 0)),
        out_shape=jax.ShapeDtypeStruct((m, D_MODEL), jnp.float32),
        scratch_shapes=[
            pltpu.VMEM((FFN_TM, D_MODEL), jnp.bfloat16),
            pltpu.VMEM((FFN_TM + CARRY, FFN_TF), jnp.float32),
            pltpu.VMEM((FFN_TM + CARRY, FFN_TF), jnp.float32),
            pltpu.VMEM((N_FT, 2, CARRY, FFN_TF), jnp.float32),
        ],
        compiler_params=_params("arbitrary", "arbitrary"),
        name="conv_geglu_ffn",
    )(x1, g, w_up, w_up, conv_w, conv_w, conv_b, conv_b, w_down, gp)


def kernel(x, pre_mix_g, w_in, b_forget, sinks, grp_swa_g, grp_fox_g, w_out, post_mix_g,
           pre_ffn_g, w_up, conv_w, conv_b, w_down, post_ffn_g):
    b, s, d = x.shape
    depth = w_in.shape[0]
    xf = x.reshape(b * s, d)
    for l in range(depth):
        w_in_bf = _cast_pad_rows(jnp.swapaxes(w_in[l], 0, 1), D_IN_PAD)
        b_f = jnp.pad(b_forget[l], (0, LANES - N_HEADS_FOX)).reshape(1, LANES)

        proj, logf, w_out_bf, w_up_bf, w_down_bf = _in_proj(xf, pre_mix_g[l].reshape(1, d), w_in_bf, b_f,
                                                            (w_out[l], w_up[l], w_down[l]))
        proj3 = proj.reshape(b, s, D_QKV)
        aug = _cumsum(logf.reshape(b, s, LANES))

        o_a = _swa(proj3, sinks[l])
        o_b = _fox(proj3, aug)

        x1 = _out_proj(o_a.reshape(b * s, D_SWA), o_b.reshape(b * s, D_FOX), xf,
                       grp_swa_g[l].reshape(1, D_SWA), grp_fox_g[l].reshape(1, D_FOX),
                       w_out_bf, post_mix_g[l].reshape(1, d))
        xf = _ffn(x1, pre_ffn_g[l].reshape(1, d), w_up_bf, conv_w[l],
                  conv_b[l].reshape(1, 2 * D_FF), w_down_bf, post_ffn_g[l].reshape(1, d), s)
    return xf.reshape(b, s, d)
```

```python
import functools

import numpy as np
import jax
import jax.numpy as jnp
from jax import lax
from jax.experimental import pallas as pl
from jax.experimental.pallas import tpu as pltpu

D_MODEL = 2048
HEAD_DIM = 64
N_HEADS_SWA = 16
N_KV_SWA = 2
N_HEADS_FOX = 16
WINDOW = 128
BLOCK = 128
D_FF = 5632
CONV_WIDTH = 3
EPS = 1e-6
D_SWA = N_HEADS_SWA * HEAD_DIM
D_KV_SWA = N_KV_SWA * HEAD_DIM
D_FOX = N_HEADS_FOX * HEAD_DIM
D_MIX = D_SWA + D_FOX
D_QKV = D_SWA + 2 * D_KV_SWA + 3 * D_FOX

LANES = 128
COL_QA = 0
COL_KA = COL_QA + D_SWA // LANES
COL_VA = COL_KA + 1
COL_QB = COL_VA + 1
COL_KB = COL_QB + D_FOX // LANES
COL_VB = COL_KB + D_FOX // LANES
D_IN_PAD = D_QKV + LANES

NEG = -1e30
VMEM_LIMIT = 56 * 1024 * 1024

ALIBI_SLOPES = [float(v) for v in np.asarray(2.0 ** (-8.0 * np.arange(1, N_HEADS_SWA + 1) / N_HEADS_SWA),
                                            dtype=np.float32)]

_NT = (((1,), (1,)), ((), ()))


def _rms(xf, g):
    return xf * lax.rsqrt(jnp.mean(xf * xf, axis=-1, keepdims=True) + EPS) * g


def _params(*sem):
    return pltpu.CompilerParams(dimension_semantics=sem, vmem_limit_bytes=VMEM_LIMIT)


CAST_TM = 5 * LANES


def _cast_pad_kernel(w_ref, o_ref, *, valid_rows):
    row = pl.program_id(0) * CAST_TM + lax.broadcasted_iota(jnp.int32, w_ref.shape, 0)
    o_ref[...] = jnp.where(row < valid_rows, w_ref[...], 0.0).astype(jnp.bfloat16)


def _cast_pad_rows(w, padded_rows):
    rows, cols = w.shape
    assert padded_rows % CAST_TM == 0 and padded_rows - rows < CAST_TM
    return pl.pallas_call(
        functools.partial(_cast_pad_kernel, valid_rows=rows),
        grid=(padded_rows // CAST_TM,),
        in_specs=[pl.BlockSpec((CAST_TM, cols), lambda j: (j, 0))],
        out_specs=pl.BlockSpec((CAST_TM, cols), lambda j: (j, 0)),
        out_shape=jax.ShapeDtypeStruct((padded_rows, cols), jnp.bfloat16),
        compiler_params=_params("arbitrary"),
        name="cast_w_in",
    )(w)


IN_TM = 512
IN_CHUNK = 1024
LOG2E = float(np.log2(np.e))
Q_MULT = LOG2E / float(np.sqrt(HEAD_DIM))


Q_RANGES = ((COL_QA * LANES, COL_QA * LANES + D_SWA), (COL_QB * LANES, COL_QB * LANES + D_FOX))


def _in_proj_kernel(x_ref, g_ref, w_ref, wf_ref, bf_ref, *refs):
    n_side = (len(refs) - 2) // 2
    side_in, (proj_ref, logf_ref), side_out = refs[:n_side], refs[n_side:n_side + 2], refs[n_side + 2:]
    for src, dst in zip(side_in, side_out):
        dst[...] = src[...].astype(jnp.bfloat16)
    h = _rms(x_ref[...], g_ref[...]).astype(jnp.bfloat16)
    for n0 in range(0, D_QKV, IN_CHUNK):
        n1 = min(n0 + IN_CHUNK, D_QKV)
        acc = lax.dot_general(h, w_ref[n0:n1, :], _NT, preferred_element_type=jnp.float32)
        overlaps = [(max(lo, n0), min(hi, n1)) for lo, hi in Q_RANGES if max(lo, n0) < min(hi, n1)]
        if overlaps:
            col = n0 + lax.broadcasted_iota(jnp.int32, (1, n1 - n0), 1)
            mult = jnp.ones((1, n1 - n0), jnp.float32)
            for lo, hi in overlaps:
                mult = jnp.where((col >= lo) & (col < hi), Q_MULT, mult)
            acc = acc * mult
        proj_ref[:, n0:n1] = acc.astype(jnp.bfloat16)
    f = lax.dot_general(h, wf_ref[...], _NT, preferred_element_type=jnp.float32) + bf_ref[...]
    logf_ref[...] = jnp.minimum(f, 0.0) - jnp.log1p(jnp.exp(-jnp.abs(f)))


def _in_proj(x2d, g, w_in_bf, b_f, side_weights):
    m = x2d.shape[0]
    steps = m // IN_TM
    side_specs = []
    for w in side_weights:
        rows, cols = w.shape
        assert rows % (steps * 16) == 0
        side_specs.append(pl.BlockSpec((rows // steps, cols), lambda i: (i, 0)))
    return pl.pallas_call(
        _in_proj_kernel,
        grid=(steps,),
        in_specs=[
            pl.BlockSpec((IN_TM, D_MODEL), lambda i: (i, 0)),
            pl.BlockSpec((1, D_MODEL), lambda i: (0, 0)),
            pl.BlockSpec((D_QKV, D_MODEL), lambda i: (0, 0), pipeline_mode=pl.Buffered(1)),
            pl.BlockSpec((LANES, D_MODEL), lambda i: (D_QKV // LANES, 0), pipeline_mode=pl.Buffered(1)),
            pl.BlockSpec((1, LANES), lambda i: (0, 0)),
        ] + side_specs,
        out_specs=[
            pl.BlockSpec((IN_TM, D_QKV), lambda i: (i, 0)),
            pl.BlockSpec((IN_TM, LANES), lambda i: (i, 0)),
        ] + side_specs,
        out_shape=[
            jax.ShapeDtypeStruct((m, D_QKV), jnp.bfloat16),
            jax.ShapeDtypeStruct((m, LANES), jnp.float32),
        ] + [jax.ShapeDtypeStruct(w.shape, jnp.bfloat16) for w in side_weights],
        compiler_params=_params("arbitrary"),
        name="in_proj",
    )(x2d, g, w_in_bf, w_in_bf, b_f, *side_weights)


CS_BLK = 128
N_SPLIT = 3


def _bias_placement():
    place = np.zeros((N_SPLIT, LANES, D_FOX), np.float32)
    for h in range(N_HEADS_FOX):
        base = (h // 2) * LANES + (HEAD_DIM if h % 2 == 0 else 0)
        for i in range(N_SPLIT):
            place[i, h, base + i] = 1.0
    return place


def _cumsum_kernel(logf_ref, place_ref, a_ref, w_scr):
    s = logf_ref.shape[1]
    nblk = s // CS_BLK
    exact = dict(precision=lax.Precision.HIGHEST, preferred_element_type=jnp.float32)
    r = lax.broadcasted_iota(jnp.int32, (CS_BLK, CS_BLK), 0)
    c = lax.broadcasted_iota(jnp.int32, (CS_BLK, CS_BLK), 1)
    lower = (r >= c).astype(jnp.float32)
    blocks = [slice(b * CS_BLK, (b + 1) * CS_BLK) for b in range(nblk)]
    for rows in blocks:
        w_scr[rows, :] = jnp.dot(lower, logf_ref[0, rows, :], **exact)
    totals = w_scr[pl.ds(CS_BLK - 1, nblk, stride=CS_BLK), :]
    rb = lax.broadcasted_iota(jnp.int32, (nblk, nblk), 0)
    cb = lax.broadcasted_iota(jnp.int32, (nblk, nblk), 1)
    carry = jnp.dot((rb > cb).astype(jnp.float32), totals, **exact)
    for b, rows in enumerate(blocks):
        rest = (w_scr[rows, :] + carry[b:b + 1, :]) * (-LOG2E)
        parts = []
        for _ in range(N_SPLIT):
            part = rest.astype(jnp.bfloat16)
            rest = rest - part.astype(jnp.float32)
            parts.append(part)
        a_ref[0, rows, :] = jnp.dot(jnp.concatenate(parts, axis=1), place_ref[...],
                                    preferred_element_type=jnp.float32).astype(jnp.bfloat16)


def _cumsum(logf3):
    b, s, _ = logf3.shape
    place = jnp.asarray(_bias_placement().reshape(N_SPLIT * LANES, D_FOX), jnp.bfloat16)
    return pl.pallas_call(
        _cumsum_kernel,
        grid=(b,),
        in_specs=[pl.BlockSpec((1, s, LANES), lambda i: (i, 0, 0)),
                  pl.BlockSpec((N_SPLIT * LANES, D_FOX), lambda i: (0, 0))],
        out_specs=pl.BlockSpec((1, s, D_FOX), lambda i: (i, 0, 0)),
        out_shape=jax.ShapeDtypeStruct((b, s, D_FOX), jnp.bfloat16),
        scratch_shapes=[pltpu.VMEM((s, LANES), jnp.float32)],
        compiler_params=_params("arbitrary"),
        name="cumsum_logf",
    )(logf3, place)


SWA_TQ = 512
PAIRS_PER_KV = (N_HEADS_SWA // N_KV_SWA) // 2


SWA_R = SWA_TQ // BLOCK
SWA_COLS = PAIRS_PER_KV * BLOCK


def _swa_kernel(sink_ref, q_ref, k_ref, v_ref, o_ref, k_scr, vt_scr, bias_scr, st_scr):
    bi = pl.program_id(0)
    qi = pl.program_id(1)
    s_len = k_ref.shape[1]
    log2_block = int(np.log2(BLOCK))

    @pl.when((bi == 0) & (qi == 0))
    def _():
        key_loc = lax.broadcasted_iota(jnp.int32, (2 * BLOCK, SWA_COLS), 0)
        col = lax.broadcasted_iota(jnp.int32, (2 * BLOCK, SWA_COLS), 1)
        dist = BLOCK + (col & (BLOCK - 1)) - key_loc
        pair = lax.shift_right_logical(col, log2_block)
        distf = dist.astype(jnp.float32)
        in_win = (dist >= 0) & (dist < WINDOW)
        for kk in range(N_KV_SWA):
            for par in range(2):
                slope = jnp.zeros((2 * BLOCK, SWA_COLS), jnp.float32)
                for pi in range(PAIRS_PER_KV):
                    h = kk * 2 * PAIRS_PER_KV + 2 * pi + par
                    slope = jnp.where(pair == pi, ALIBI_SLOPES[h] * LOG2E, slope)
                bias_scr[kk, par] = jnp.where(in_win, -slope * distf, NEG)

    @pl.when(qi == 0)
    def _():
        low = lax.broadcasted_iota(jnp.int32, (SWA_TQ, LANES), 1) < HEAD_DIM
        halves = (low, jnp.logical_not(low))
        zero_k = jnp.zeros((BLOCK, LANES), jnp.bfloat16)
        zero_v = jnp.zeros((LANES, BLOCK), jnp.bfloat16)
        for kk in range(N_KV_SWA):
            for par in range(2):
                k_scr[kk, par, 0:BLOCK, :] = zero_k
                vt_scr[kk, par, :, 0:BLOCK] = zero_v
        for c0 in range(0, s_len, SWA_TQ):
            kc = k_ref[0, c0:c0 + SWA_TQ, :].astype(jnp.float32)
            vc = v_ref[0, c0:c0 + SWA_TQ, :].astype(jnp.float32)
            ksw = pltpu.roll(kc, HEAD_DIM, 1)
            vsw = pltpu.roll(vc, HEAD_DIM, 1)
            rows = slice(BLOCK + c0, BLOCK + c0 + SWA_TQ)
            for kk in range(N_KV_SWA):
                for par in range(2):
                    ksrc, vsrc = (kc, vc) if kk == par else (ksw, vsw)
                    k_scr[kk, par, rows, :] = jnp.where(halves[par], ksrc, 0.0).astype(jnp.bfloat16)
                    vt_scr[kk, par, :, rows] = jnp.where(halves[par], vsrc, 1.0).T.astype(jnp.bfloat16)

    pair1 = lax.shift_right_logical(lax.broadcasted_iota(jnp.int32, (1, SWA_COLS), 1), log2_block)
    pad_rows = lax.broadcasted_iota(jnp.int32, (2 * BLOCK, SWA_COLS), 0) < BLOCK
    first_tile = qi == 0
    sinks2 = {}
    for kk in range(N_KV_SWA):
        for par in range(2):
            sink = jnp.zeros((1, SWA_COLS), jnp.float32)
            for pi in range(PAIRS_PER_KV):
                sink = jnp.where(pair1 == pi, sink_ref[kk * 2 * PAIRS_PER_KV + 2 * pi + par] * LOG2E, sink)
            sinks2[kk, par] = sink

    def band_start(r):
        return pl.multiple_of(qi * SWA_TQ + r * BLOCK, BLOCK)

    def scores(r):
        rows = slice(r * BLOCK, (r + 1) * BLOCK)
        for kk in range(N_KV_SWA):
            qs = jnp.concatenate([q_ref[0, rows, (kk * PAIRS_PER_KV + pi) * LANES:(kk * PAIRS_PER_KV + pi + 1) * LANES]
                                  for pi in range(PAIRS_PER_KV)], axis=0)
            for par in range(2):
                st_scr[r, kk, par] = lax.dot_general(k_scr[kk, par, pl.ds(band_start(r), 2 * BLOCK), :], qs, _NT,
                                                     preferred_element_type=jnp.float32)

    def finish(r):
        rows = slice(r * BLOCK, (r + 1) * BLOCK)
        for kk in range(N_KV_SWA):
            outs = []
            for par in range(2):
                sink = sinks2[kk, par]
                st = st_scr[r, kk, par] + bias_scr[kk, par]
                if r == 0:
                    st = jnp.where(pad_rows & first_tile, NEG, st)
                m = jnp.maximum(jnp.max(st, axis=0, keepdims=True), sink)
                p = jnp.exp2(st - m).astype(jnp.bfloat16)
                o = jnp.dot(vt_scr[kk, par, :, pl.ds(band_start(r), 2 * BLOCK)], p,
                            preferred_element_type=jnp.float32)
                extra = jnp.exp2(sink - m)
                if par == 0:
                    outs.append(o[0:HEAD_DIM] / (o[HEAD_DIM:HEAD_DIM + 1] + extra))
                else:
                    outs.append(o[HEAD_DIM:LANES] / (o[0:1] + extra))
            ot = jnp.concatenate(outs, axis=0).T
            for pi in range(PAIRS_PER_KV):
                c0 = (kk * PAIRS_PER_KV + pi) * LANES
                o_ref[0, rows, c0:c0 + LANES] = ot[pi * BLOCK:(pi + 1) * BLOCK, :]

    scores(0)
    for r in range(SWA_R):
        if r + 1 < SWA_R:
            scores(r + 1)
        finish(r)


def _swa(proj3, sinks):
    b, s, _ = proj3.shape
    return pl.pallas_call(
        _swa_kernel,
        grid=(b, s // SWA_TQ),
        in_specs=[
            pl.BlockSpec(memory_space=pltpu.SMEM),
            pl.BlockSpec((1, SWA_TQ, D_SWA), lambda bi, qi: (bi, qi, COL_QA * LANES // D_SWA)),
            pl.BlockSpec((1, s, LANES), lambda bi, qi: (bi, 0, COL_KA)),
            pl.BlockSpec((1, s, LANES), lambda bi, qi: (bi, 0, COL_VA)),
        ],
        out_specs=pl.BlockSpec((1, SWA_TQ, D_SWA), lambda bi, qi: (bi, qi, 0)),
        scratch_shapes=[
            pltpu.VMEM((N_KV_SWA, 2, s + BLOCK, LANES), jnp.bfloat16),
            pltpu.VMEM((N_KV_SWA, 2, LANES, s + BLOCK), jnp.bfloat16),
            pltpu.VMEM((N_KV_SWA, 2, 2 * BLOCK, SWA_COLS), jnp.float32),
            pltpu.VMEM((SWA_R, N_KV_SWA, 2, 2 * BLOCK, SWA_COLS), jnp.float32),
        ],
        out_shape=jax.ShapeDtypeStruct((b, s, D_SWA), jnp.float32),
        compiler_params=_params("arbitrary", "arbitrary"),
        name="swa_attention",
    )(sinks, proj3, proj3, proj3)


FOX_TQ = 512
FOX_TK = FOX_TQ
FOX_PAIRS = 2
FOX_VROWS = HEAD_DIM + 16
assert all(col % FOX_PAIRS == 0 for col in (COL_QB, COL_KB, COL_VB))


def _fox_kernel(q_ref, qn_ref, k_ref, v_ref, a_ref, o_ref, ka_scr, vat_scr, qa_scr, m_scr, acc_scr, st_scr, cmax_scr):
    qi = pl.program_id(2)
    s_len = k_ref.shape[1]
    n_tiles = s_len // FOX_TQ
    half = FOX_TQ // 2
    low_k = lax.broadcasted_iota(jnp.int32, (FOX_TK, LANES), 1) < HEAD_DIM
    keeps = (low_k, jnp.logical_not(low_k))
    heads = range(2 * FOX_PAIRS)

    lane_q = lax.broadcasted_iota(jnp.int32, (FOX_TQ, LANES), 1)
    ones_hi = jnp.where((lane_q >= HEAD_DIM) & (lane_q < HEAD_DIM + N_SPLIT), 1.0, 0.0).astype(jnp.bfloat16)
    ones_lo = jnp.where(lane_q < N_SPLIT, 1.0, 0.0).astype(jnp.bfloat16)

    def build_queries(ref):
        for pair in range(FOX_PAIRS):
            q = ref[0, :, pair * LANES:(pair + 1) * LANES]
            qa_scr[2 * pair] = jnp.where(lane_q < HEAD_DIM, q, ones_hi)
            qa_scr[2 * pair + 1] = jnp.where(lane_q < HEAD_DIM, ones_lo, q)

    @pl.when(qi == 0)
    def _():
        for c0 in range(0, s_len, FOX_TK):
            for pair in range(FOX_PAIRS):
                cols = slice(pair * LANES, (pair + 1) * LANES)
                kc = k_ref[0, c0:c0 + FOX_TK, cols]
                ac = a_ref[0, c0:c0 + FOX_TK, cols]
                vt = v_ref[0, c0:c0 + FOX_TK, cols].astype(jnp.float32).T.astype(jnp.bfloat16)
                ones = jnp.ones((FOX_VROWS - HEAD_DIM, FOX_TK), jnp.bfloat16)
                for par in range(2):
                    h = 2 * pair + par
                    ka_scr[h, c0:c0 + FOX_TK, :] = jnp.where(keeps[par], kc, ac)
                vat_scr[2 * pair, :, c0:c0 + FOX_TK] = jnp.concatenate([vt[0:HEAD_DIM], ones], axis=0)
                vat_scr[2 * pair + 1, :, c0:c0 + FOX_TK] = jnp.concatenate([ones, vt[HEAD_DIM:LANES]], axis=0)
        build_queries(q_ref)

    tri = (lax.broadcasted_iota(jnp.int32, (half, half), 0)
           <= lax.broadcasted_iota(jnp.int32, (half, half), 1))

    def scores(c, slot, diagonal):
        keys = slice(c * FOX_TK, (c + 1) * FOX_TK)
        for h in heads:
            if diagonal:
                lo_keys = slice(c * FOX_TK, c * FOX_TK + half)
                st_scr[slot, h, 0:half, 0:half] = lax.dot_general(ka_scr[h, lo_keys, :], qa_scr[h, 0:half, :], _NT,
                                                                  preferred_element_type=jnp.float32)
                st_scr[slot, h, :, half:FOX_TQ] = lax.dot_general(ka_scr[h, keys, :], qa_scr[h, half:FOX_TQ, :], _NT,
                                                                  preferred_element_type=jnp.float32)
            else:
                st = lax.dot_general(ka_scr[h, keys, :], qa_scr[h], _NT,
                                     preferred_element_type=jnp.float32)
                st_scr[slot, h] = st
                cmax_scr[slot, h] = jnp.max(st, axis=0, keepdims=True)

    def accumulate(h, st, cmax, keys, cols, first):
        m_new = cmax if first else jnp.maximum(m_scr[h, :, cols], cmax)
        pt = jnp.exp2(st - m_new).astype(jnp.bfloat16)
        pv = jnp.dot(vat_scr[h, :, keys], pt, preferred_element_type=jnp.float32)
        if first:
            acc_scr[h, :, cols] = pv
        else:
            acc_scr[h, :, cols] = jnp.exp2(m_scr[h, :, cols] - m_new) * acc_scr[h, :, cols] + pv
        m_scr[h, :, cols] = m_new

    def update(c, slot, diagonal):
        k0 = c * FOX_TK
        first = c == 0
        for h in heads:
            if diagonal:
                st_lo = jnp.where(tri, st_scr[slot, h, 0:half, 0:half], NEG)
                accumulate(h, st_lo, jnp.max(st_lo, axis=0, keepdims=True), slice(k0, k0 + half), slice(0, half),
                           first)
                st_hi = jnp.concatenate([st_scr[slot, h, 0:half, half:FOX_TQ],
                                         jnp.where(tri, st_scr[slot, h, half:FOX_TK, half:FOX_TQ], NEG)], axis=0)
                accumulate(h, st_hi, jnp.max(st_hi, axis=0, keepdims=True), slice(k0, k0 + FOX_TK),
                           slice(half, FOX_TQ), first)
            else:
                accumulate(h, st_scr[slot, h], cmax_scr[slot, h], slice(k0, k0 + FOX_TK), slice(0, FOX_TQ), first)

    def tile(t):
        flip = (t * (t + 1) // 2) % 2
        slot = lambda c: (c + flip) % 2
        if t == 0:
            scores(0, slot(0), True)
        for c in range(t + 1):
            if c < t:
                scores(c + 1, slot(c + 1), c + 1 == t)
            elif t + 1 < n_tiles:
                build_queries(qn_ref)
                scores(0, slot(t + 1), False)
            update(c, slot(c), c == t)

    for t in range(n_tiles):
        pl.when(qi == t)(functools.partial(tile, t))

    for pair in range(FOX_PAIRS):
        acc0 = acc_scr[2 * pair]
        acc1 = acc_scr[2 * pair + 1]
        ot = jnp.concatenate([acc0[0:HEAD_DIM] / acc0[HEAD_DIM:HEAD_DIM + 1],
                              acc1[FOX_VROWS - HEAD_DIM:FOX_VROWS] / acc1[0:1]], axis=0)
        o_ref[0, :, pair * LANES:(pair + 1) * LANES] = ot.T


def _fox(proj3, aug):
    b, s, _ = proj3.shape
    width = FOX_PAIRS * LANES
    ngroups = D_FOX // width
    nheads = 2 * FOX_PAIRS
    n_tiles = s // FOX_TQ
    return pl.pallas_call(
        _fox_kernel,
        grid=(b, ngroups, n_tiles),
        in_specs=[
            pl.BlockSpec((1, FOX_TQ, width), lambda bi, p, qi: (bi, qi, COL_QB // FOX_PAIRS + p)),
            pl.BlockSpec((1, FOX_TQ, width),
                         lambda bi, p, qi: (bi, jnp.minimum(qi + 1, n_tiles - 1), COL_QB // FOX_PAIRS + p)),
            pl.BlockSpec((1, s, width), lambda bi, p, qi: (bi, 0, COL_KB // FOX_PAIRS + p)),
            pl.BlockSpec((1, s, width), lambda bi, p, qi: (bi, 0, COL_VB // FOX_PAIRS + p)),
            pl.BlockSpec((1, s, width), lambda bi, p, qi: (bi, 0, p)),
        ],
        out_specs=pl.BlockSpec((1, FOX_TQ, width), lambda bi, p, qi: (bi, qi, p)),
        out_shape=jax.ShapeDtypeStruct((b, s, D_FOX), jnp.float32),
        scratch_shapes=[
            pltpu.VMEM((nheads, s, LANES), jnp.bfloat16),
            pltpu.VMEM((nheads, FOX_VROWS, s), jnp.bfloat16),
            pltpu.VMEM((nheads, FOX_TQ, LANES), jnp.bfloat16),
            pltpu.VMEM((nheads, 1, FOX_TQ), jnp.float32),
            pltpu.VMEM((nheads, FOX_VROWS, FOX_TQ), jnp.float32),
            pltpu.VMEM((2, nheads, FOX_TK, FOX_TQ), jnp.float32),
            pltpu.VMEM((2, nheads, 1, FOX_TQ), jnp.float32),
        ],
        compiler_params=_params("arbitrary", "arbitrary", "arbitrary"),
        name="fox_attention",
    )(proj3, proj3, proj3, proj3, aug)


OUT_TM = 512


def _out_proj_kernel(oa_ref, ob_ref, x_ref, ga_ref, gb_ref, w_ref, gp_ref, x1_ref):
    na = _rms(oa_ref[...], ga_ref[...]).astype(jnp.bfloat16)
    nb = _rms(ob_ref[...], gb_ref[...]).astype(jnp.bfloat16)
    mix = (jnp.dot(na, w_ref[0:D_SWA, :], preferred_element_type=jnp.float32)
           + jnp.dot(nb, w_ref[D_SWA:D_MIX, :], preferred_element_type=jnp.float32))
    x1_ref[...] = x_ref[...] + _rms(mix, gp_ref[...])


def _out_proj(oa, ob, x2d, ga, gb, w_out, gp):
    m = x2d.shape[0]
    row = lambda i: (i, 0)
    fixed = lambda i: (0, 0)
    return pl.pallas_call(
        _out_proj_kernel,
        grid=(m // OUT_TM,),
        in_specs=[
            pl.BlockSpec((OUT_TM, D_SWA), row),
            pl.BlockSpec((OUT_TM, D_FOX), row),
            pl.BlockSpec((OUT_TM, D_MODEL), row),
            pl.BlockSpec((1, D_SWA), fixed),
            pl.BlockSpec((1, D_FOX), fixed),
            pl.BlockSpec((D_MIX, D_MODEL), fixed, pipeline_mode=pl.Buffered(1)),
            pl.BlockSpec((1, D_MODEL), fixed),
        ],
        out_specs=pl.BlockSpec((OUT_TM, D_MODEL), row),
        out_shape=jax.ShapeDtypeStruct((m, D_MODEL), jnp.float32),
        compiler_params=_params("arbitrary"),
        name="out_proj",
    )(oa, ob, x2d, ga, gb, w_out, gp)


FFN_TM = 512
FFN_TF = 512
CARRY = 8
ROW_CHUNK = 32
N_FT = D_FF // FFN_TF


def _gelu_tanh(x):
    return 0.5 * x * (1.0 + jnp.tanh(np.sqrt(2.0 / np.pi) * (x + 0.044715 * (x * x * x))))


def _ffn_kernel(x_ref, g_ref, wg_ref, wv_ref, cwg_ref, cwv_ref, cbg_ref, cbv_ref, wd_ref, gp_ref,
                o_ref, h_scr, ug_scr, uv_scr, tail_scr, *, tiles_per_seq):
    i = pl.program_id(0)
    j = pl.program_id(1)

    @pl.when(j == 0)
    def _():
        h_scr[...] = _rms(x_ref[...], g_ref[...]).astype(jnp.bfloat16)
        o_ref[...] = jnp.zeros_like(o_ref)

    @pl.when((i == 0) & (j == 0))
    def _():
        tail_scr[...] = jnp.zeros_like(tail_scr)

    h = h_scr[...]
    seq_start = i % tiles_per_seq == 0
    for k, (u_scr, w_ref) in enumerate(((ug_scr, wg_ref), (uv_scr, wv_ref))):
        u_scr[0:CARRY, :] = jnp.where(seq_start, 0.0, tail_scr[j, k])
        u_scr[CARRY:, :] = jnp.dot(h, w_ref[...], preferred_element_type=jnp.float32)
        tail_scr[j, k] = u_scr[FFN_TM:FFN_TM + CARRY, :]

    def conv(u_scr, cw_ref, cb_ref, r0, nrows):
        y = cb_ref[...]
        for kk in range(CONV_WIDTH):
            y = y + u_scr[pl.ds(r0 + CARRY - (CONV_WIDTH - 1) + kk, nrows), :] * cw_ref[kk:kk + 1, :]
        return y

    half_rows = FFN_TM // 2
    a = jnp.concatenate(
        [(_gelu_tanh(conv(ug_scr, cwg_ref, cbg_ref, r0, half_rows))
          * conv(uv_scr, cwv_ref, cbv_ref, r0, half_rows)).astype(jnp.bfloat16)
         for r0 in (0, half_rows)], axis=0)
    o_ref[...] += jnp.dot(a, wd_ref[...], preferred_element_type=jnp.float32)

    @pl.when(j == N_FT - 1)
    def _():
        gp = gp_ref[...]
        for r0 in range(0, FFN_TM, ROW_CHUNK):
            rows = slice(r0, r0 + ROW_CHUNK)
            o_ref[rows, :] = x_ref[rows, :] + _rms(o_ref[rows, :], gp)


def _ffn(x1, g, w_up, conv_w, conv_b, w_down, gp, seq_len):
    m = x1.shape[0]
    kern = functools.partial(_ffn_kernel, tiles_per_seq=seq_len // FFN_TM)
    return pl.pallas_call(
        kern,
        grid=(m // FFN_TM, N_FT),
        in_specs=[
            pl.BlockSpec((FFN_TM, D_MODEL), lambda i, j: (i, 0)),
            pl.BlockSpec((1, D_MODEL), lambda i, j: (0, 0)),
            pl.BlockSpec((D_MODEL, FFN_TF), lambda i, j: (0, j)),
            pl.BlockSpec((D_MODEL, FFN_TF), lambda i, j: (0, j + N_FT)),
            pl.BlockSpec((CONV_WIDTH, FFN_TF), lambda i, j: (0, j)),
            pl.BlockSpec((CONV_WIDTH, FFN_TF), lambda i, j: (0, j + N_FT)),
            pl.BlockSpec((1, FFN_TF), lambda i, j: (0, j)),
            pl.BlockSpec((1, FFN_TF), lambda i, j: (0, j + N_FT)),
            pl.BlockSpec((FFN_TF, D_MODEL), lambda i, j: (j, 0)),
            pl.BlockSpec((1, D_MODEL), lambda i, j: (0, 0)),
        ],
        out_specs=pl.BlockSpec((FFN_TM, D_MODEL), lambda i, j: (i, 0)),
        out_shape=jax.ShapeDtypeStruct((m, D_MODEL), jnp.float32),
        scratch_shapes=[
            pltpu.VMEM((FFN_TM, D_MODEL), jnp.bfloat16),
            pltpu.VMEM((FFN_TM + CARRY, FFN_TF), jnp.float32),
            pltpu.VMEM((FFN_TM + CARRY, FFN_TF), jnp.float32),
            pltpu.VMEM((N_FT, 2, CARRY, FFN_TF), jnp.float32),
        ],
        compiler_params=_params("arbitrary", "arbitrary"),
        name="conv_geglu_ffn",
    )(x1, g, w_up, w_up, conv_w, conv_w, conv_b, conv_b, w_down, gp)


def kernel(x, pre_mix_g, w_in, b_forget, sinks, grp_swa_g, grp_fox_g, w_out, post_mix_g,
           pre_ffn_g, w_up, conv_w, conv_b, w_down, post_ffn_g):
    b, s, d = x.shape
    depth = w_in.shape[0]
    xf = x.reshape(b * s, d)
    for l in range(depth):
        w_in_bf = _cast_pad_rows(jnp.swapaxes(w_in[l], 0, 1), D_IN_PAD)
        b_f = jnp.pad(b_forget[l], (0, LANES - N_HEADS_FOX)).reshape(1, LANES)

        proj, logf, w_out_bf, w_up_bf, w_down_bf = _in_proj(xf, pre_mix_g[l].reshape(1, d), w_in_bf, b_f,
                                                            (w_out[l], w_up[l], w_down[l]))
        proj3 = proj.reshape(b, s, D_QKV)
        aug = _cumsum(logf.reshape(b, s, LANES))

        o_a = _swa(proj3, sinks[l])
        o_b = _fox(proj3, aug)

        x1 = _out_proj(o_a.reshape(b * s, D_SWA), o_b.reshape(b * s, D_FOX), xf,
                       grp_swa_g[l].reshape(1, D_SWA), grp_fox_g[l].reshape(1, D_FOX),
                       w_out_bf, post_mix_g[l].reshape(1, d))
        xf = _ffn(x1, pre_ffn_g[l].reshape(1, d), w_up_bf, conv_w[l],
                  conv_b[l].reshape(1, 2 * D_FF), w_down_bf, post_ffn_g[l].reshape(1, d), s)
    return xf.reshape(b, s, d)
```
